```python
import math
import jax, jax.numpy as jnp
from jax import lax
import numpy as np

D_MODEL = 1024
BATCH = 2
SEQ = 8192
DEPTH = 2

HEAD_DIM = 64
N_HEADS_MOBA = 6
N_HEADS_SSD = 6
N_HEADS_DIL = 6
N_HEADS_DELTA = 6
W_MOBA = N_HEADS_MOBA * HEAD_DIM
W_SSD = N_HEADS_SSD * HEAD_DIM
W_DIL = N_HEADS_DIL * HEAD_DIM
W_DELTA = N_HEADS_DELTA * HEAD_DIM
D_MIX = W_MOBA + W_SSD + W_DIL + W_DELTA

MOBA_BLOCK = 256
MOBA_TOPK = 3
MOBA_Q_CHUNK = 64

SSD_STATE = 128
SSD_GROUPS = 2
SSD_CHUNK = 128
SSD_XBC = W_SSD + 2 * SSD_GROUPS * SSD_STATE

CONV_WIDTH = 4

DIL_PAIRS = ((128, 1), (512, 4), (2048, 16))
DIL_Q_CHUNK = 128

DELTA_CHUNK = 64
DELTA_QKV = 3 * W_DELTA

LN_EPS = 1e-5
RMS_EPS = 1e-6
DEEPNORM_ALPHA = (2.0 * DEPTH) ** 0.25
DEEPNORM_BETA = (8.0 * DEPTH) ** -0.25

IN_SPLIT_SIZES = (
    W_MOBA, W_MOBA, W_MOBA, W_MOBA,
    SSD_XBC, W_SSD, N_HEADS_SSD,
    W_DIL, W_DIL, W_DIL, W_DIL,
    DELTA_QKV, W_DELTA, N_HEADS_DELTA, N_HEADS_DELTA,
)
IN_COLS = int(sum(IN_SPLIT_SIZES))
IN_SPLIT_POINTS = tuple(int(i) for i in np.cumsum(IN_SPLIT_SIZES)[:-1])

kernel_name = "hymba_style_moba_ssd_dilated_gdn_deepnorm"

NEG_INF = -jnp.inf


def alibi_slopes():
    n = N_HEADS_DIL + N_HEADS_MOBA
    s = 2.0 ** (-8.0 * (np.arange(n) + 1) / n)
    s = jnp.asarray(s, dtype=jnp.float32)
    return s[:N_HEADS_DIL], s[N_HEADS_DIL:]


def layer_norm(x, g, b):
    xf = x.astype(jnp.float32)
    mu = xf.mean(-1, keepdims=True)
    var = jnp.square(xf - mu).mean(-1, keepdims=True)
    return ((xf - mu) * lax.rsqrt(var + LN_EPS) * g + b).astype(x.dtype)


def rms_norm(x, w):
    xf = x.astype(jnp.float32)
    return (xf * lax.rsqrt(jnp.mean(xf * xf, -1, keepdims=True) + RMS_EPS) * w).astype(x.dtype)


def l2_normalize(x):
    xf = x.astype(jnp.float32)
    return xf * lax.rsqrt(jnp.sum(xf * xf, -1, keepdims=True) + RMS_EPS)


def causal_dw_conv(x, w, b):
    K, C = w.shape
    y = lax.conv_general_dilated(
        x, w[:, None, :].astype(x.dtype), window_strides=(1,), padding=[(K - 1, 0)],
        dimension_numbers=("NWC", "WIO", "NWC"), feature_group_count=C)
    return y + b.astype(x.dtype)


def moba_attention(q, k, v, slopes):
    Bsz, S, H, Dh = q.shape
    L = MOBA_BLOCK
    nblk = -(-S // L)
    pad = nblk * L - S
    scale = Dh ** -0.5
    kp = jnp.pad(k, ((0, 0), (0, pad), (0, 0), (0, 0)))
    vp = jnp.pad(v, ((0, 0), (0, pad), (0, 0), (0, 0)))
    kb = kp.reshape(Bsz, nblk, L, H, Dh).transpose(0, 3, 1, 2, 4)
    vb = vp.reshape(Bsz, nblk, L, H, Dh).transpose(0, 3, 1, 2, 4)
    kmean = kb.astype(jnp.float32).mean(axis=3)
    qt = q.transpose(0, 2, 1, 3)

    gate = jnp.einsum("bhsd,bhnd->bhsn", qt.astype(jnp.float32), kmean)
    qblk = jnp.arange(S) // L
    past = jnp.arange(nblk)[None, :] < qblk[:, None]
    gate = jnp.where(past[None, None], gate, NEG_INF)
    k_sel = min(MOBA_TOPK, nblk)
    _, sel = lax.top_k(gate, k_sel)
    valid = sel < qblk[None, None, :, None]

    bi = jnp.arange(Bsz)[:, None, None, None]
    hi = jnp.arange(H)[None, :, None, None]
    Qc = MOBA_Q_CHUNK
    sl = slopes[None, :, None, None]

    def chunk(c):
        t0 = c * Qc
        tq = t0 + jnp.arange(Qc)
        qc = lax.dynamic_slice_in_dim(qt, t0, Qc, axis=2)
        selc = lax.dynamic_slice_in_dim(sel, t0, Qc, axis=2)
        validc = lax.dynamic_slice_in_dim(valid, t0, Qc, axis=2)
        kg = kb[bi, hi, selc]
        vg = vb[bi, hi, selc]
        pos_g = selc[..., None] * L + jnp.arange(L)
        dist_g = (tq[None, None, :, None, None] - pos_g).astype(jnp.float32)
        s_g = jnp.einsum("bhqd,bhqjld->bhqjl", qc, kg).astype(jnp.float32) * scale
        s_g = s_g - sl[..., None] * dist_g
        s_g = jnp.where(validc[..., None], s_g, NEG_INF).reshape(Bsz, H, Qc, k_sel * L)
        own = t0 // L
        ko = lax.dynamic_index_in_dim(kb, own, axis=2, keepdims=False)
        vo = lax.dynamic_index_in_dim(vb, own, axis=2, keepdims=False)
        pos_o = own * L + jnp.arange(L)
        dist_o = (tq[:, None] - pos_o[None, :]).astype(jnp.float32)
        s_o = jnp.einsum("bhqd,bhld->bhql", qc, ko).astype(jnp.float32) * scale
        s_o = s_o - sl * dist_o[None, None]
        s_o = jnp.where((pos_o[None, :] <= tq[:, None])[None, None], s_o, NEG_INF)
        p = jax.nn.softmax(jnp.concatenate([s_g, s_o], axis=-1), axis=-1).astype(v.dtype)
        p_g = p[..., :k_sel * L].reshape(Bsz, H, Qc, k_sel, L)
        p_o = p[..., k_sel * L:]
        return (jnp.einsum("bhqjl,bhqjld->bhqd", p_g, vg)
                + jnp.einsum("bhql,bhld->bhqd", p_o, vo))

    outs = lax.map(chunk, jnp.arange(S // Qc))
    return outs.transpose(1, 0, 3, 2, 4).reshape(Bsz, S, H, Dh)


def ssd_chunked(X, a, Bh, Ch):
    Bsz, S, H, P = X.shape
    N = Bh.shape[-1]
    L = SSD_CHUNK
    nc = S // L
    X = X.reshape(Bsz, nc, L, H, P)
    Bc = Bh.reshape(Bsz, nc, L, H, N)
    Cc = Ch.reshape(Bsz, nc, L, H, N)
    a = a.reshape(Bsz, nc, L, H).transpose(0, 3, 1, 2)
    a_cum = jnp.cumsum(a, axis=-1)
    causal = jnp.tril(jnp.ones((L, L), dtype=bool))
    Lmat = jnp.exp(jnp.where(causal, a_cum[..., :, None] - a_cum[..., None, :], NEG_INF))
    scores = jnp.einsum("bclhn,bcshn->bhcls", Cc, Bc)
    y_diag = jnp.einsum("bhcls,bcshp->bclhp", scores * Lmat, X)
    decay_states = jnp.exp(a_cum[..., -1:] - a_cum)
    states = jnp.einsum("bclhn,bhcl,bclhp->bchpn", Bc, decay_states, X)
    chunk_decay = jnp.exp(a_cum[..., -1])

    def step(h, inp):
        st, dec = inp
        return h * dec[..., None, None] + st, h

    h0 = jnp.zeros((Bsz, H, P, N), dtype=X.dtype)
    _, h_prev = lax.scan(step, h0, (jnp.moveaxis(states, 1, 0), jnp.moveaxis(chunk_decay, 2, 0)))
    h_prev = jnp.moveaxis(h_prev, 0, 1)
    y_off = jnp.einsum("bclhn,bchpn,bhcl->bclhp", Cc, h_prev, jnp.exp(a_cum))
    return (y_diag + y_off).reshape(Bsz, S, H, P)


def ssd_mixer(xbc, dt_raw, dt_bias, A_log, D_skip):
    Bsz, S, _ = xbc.shape
    xbc = xbc.astype(jnp.float32)
    xs, Bm, Cm = jnp.split(xbc, [W_SSD, W_SSD + SSD_GROUPS * SSD_STATE], axis=-1)
    xs = xs.reshape(Bsz, S, N_HEADS_SSD, HEAD_DIM)
    hpg = N_HEADS_SSD // SSD_GROUPS
    Bh = jnp.repeat(Bm.reshape(Bsz, S, SSD_GROUPS, SSD_STATE), hpg, axis=2)
    Ch = jnp.repeat(Cm.reshape(Bsz, S, SSD_GROUPS, SSD_STATE), hpg, axis=2)
    dt = jax.nn.softplus(dt_raw.astype(jnp.float32) + dt_bias)
    A = -jnp.exp(A_log.astype(jnp.float32))
    y = ssd_chunked(xs * dt[..., None], dt * A, Bh, Ch)
    y = y + D_skip.astype(jnp.float32)[:, None] * xs
    return y.reshape(Bsz, S, W_SSD)


def dilated_attention(q, k, v, slopes):
    Bsz, S, H, Dh = q.shape
    scale = Dh ** -0.5
    Qc = DIL_Q_CHUNK
    sl = slopes[None, :, None, None]

    def chunk(c):
        t0 = c * Qc
        tq = t0 + jnp.arange(Qc)
        qc = lax.dynamic_slice_in_dim(q, t0, Qc, axis=1)
        outs, lses = [], []
        for (w, d) in DIL_PAIRS:
            nk = w // d + 1
            dist = d * jnp.arange(nk)
            idx = tq[:, None] - dist[None, :]
            ok = idx >= 0
            idxc = jnp.maximum(idx, 0)
            kg = jnp.take(k, idxc, axis=1)
            vg = jnp.take(v, idxc, axis=1)
            s = jnp.einsum("bqhd,bqnhd->bhqn", qc, kg).astype(jnp.float32) * scale
            s = s - sl * dist.astype(jnp.float32)[None, None, None, :]
            s = jnp.where(ok[None, None], s, NEG_INF)
            m = jnp.max(s, axis=-1, keepdims=True)
            e = jnp.exp(s - m)
            den = jnp.sum(e, axis=-1, keepdims=True)
            outs.append(jnp.einsum("bhqn,bqnhd->bhqd", (e / den).astype(v.dtype), vg).astype(jnp.float32))
            lses.append(m + jnp.log(den))
        wts = jax.nn.softmax(jnp.concatenate(lses, axis=-1), axis=-1)
        out = sum(wts[..., g:g + 1] * outs[g] for g in range(len(DIL_PAIRS)))
        return out.astype(v.dtype)

    outs = lax.map(chunk, jnp.arange(S // Qc))
    return outs.transpose(1, 0, 3, 2, 4).reshape(Bsz, S, H, Dh)


def gated_delta_chunked(q, k, v, beta, g):
    Bsz, S, H, Dk = q.shape
    Dv = v.shape[-1]
    C = DELTA_CHUNK
    nc = S // C

    def to_chunks(t):
        return t.reshape(Bsz, nc, C, H, -1).transpose(0, 3, 1, 2, 4)

    q, k, v = to_chunks(q), to_chunks(k), to_chunks(v)
    beta = beta.reshape(Bsz, nc, C, H).transpose(0, 3, 1, 2)
    g_cum = jnp.cumsum(g.reshape(Bsz, nc, C, H).transpose(0, 3, 1, 2), axis=-1)
    incl = jnp.tril(jnp.ones((C, C), dtype=bool))
    strict = jnp.tril(jnp.ones((C, C), dtype=bool), -1)
    decay = jnp.exp(jnp.where(incl, g_cum[..., :, None] - g_cum[..., None, :], NEG_INF))
    kb = k * beta[..., None]
    Lm = jnp.where(strict, jnp.einsum("bhcid,bhcjd->bhcij", kb, k) * decay, 0.0)
    eye = jnp.eye(C, dtype=q.dtype)
    T = lax.linalg.triangular_solve(eye + Lm, jnp.broadcast_to(eye, Lm.shape),
                                    left_side=True, lower=True, unit_diagonal=True)
    u = T @ (v * beta[..., None])
    w = T @ (kb * jnp.exp(g_cum)[..., None])
    attn = jnp.where(incl, jnp.einsum("bhcid,bhcjd->bhcij", q, k) * decay, 0.0)

    def step(St, inp):
        qc, kc, uc, wc, gc, ac = inp
        v_new = uc - wc @ St
        o = (qc * jnp.exp(gc)[..., None]) @ St + ac @ v_new
        glast = gc[..., -1]
        St = St * jnp.exp(glast)[..., None, None] + jnp.einsum(
            "bhcd,bhce->bhde", kc * jnp.exp(glast[..., None] - gc)[..., None], v_new)
        return St, o

    xs = tuple(jnp.moveaxis(t, 2, 0) for t in (q, k, u, w, g_cum, attn))
    S0 = jnp.zeros((Bsz, H, Dk, Dv), dtype=q.dtype)
    _, o = lax.scan(step, S0, xs)
    return o.transpose(1, 0, 3, 2, 4).reshape(Bsz, S, H, Dv)


def hybrid_layer(x, w_in, w_out, ssm_conv_w, ssm_conv_b, ssm_dt_bias, ssm_A_log, ssm_D,
                 ssm_norm_w, dn_conv_w, dn_conv_b, dn_dt_bias, dn_A_log, dn_norm_w, ln_g, ln_b):
    Bsz, S, _ = x.shape
    slopes_dil, slopes_moba = alibi_slopes()
    proj = x @ w_in
    (a_q, a_k, a_v, a_g, s_xbc, s_z, s_dt, c_q, c_k, c_v, c_g,
     d_qkv, d_g, d_b, d_a) = jnp.split(proj, IN_SPLIT_POINTS, axis=-1)

    def heads(t):
        return t.reshape(Bsz, S, -1, HEAD_DIM)

    y_a = moba_attention(heads(a_q), heads(a_k), heads(a_v), slopes_moba).reshape(Bsz, S, W_MOBA)
    y_a = y_a * jax.nn.silu(a_g)

    xbc = jax.nn.silu(causal_dw_conv(s_xbc, ssm_conv_w, ssm_conv_b))
    y_b = ssd_mixer(xbc, s_dt, ssm_dt_bias, ssm_A_log, ssm_D)
    y_b = rms_norm(y_b * jax.nn.silu(s_z.astype(jnp.float32)), ssm_norm_w).astype(x.dtype)

    y_c = dilated_attention(heads(c_q), heads(c_k), heads(c_v), slopes_dil).reshape(Bsz, S, W_DIL)
    y_c = y_c * jax.nn.silu(c_g)

    qkv = jax.nn.silu(causal_dw_conv(d_qkv, dn_conv_w, dn_conv_b))
    dq, dk, dv = jnp.split(qkv, [W_DELTA, 2 * W_DELTA], axis=-1)
    dq = l2_normalize(heads(dq)) * (HEAD_DIM ** -0.5)
    dk = l2_normalize(heads(dk))
    dv = heads(dv).astype(jnp.float32)
    d_beta = jax.nn.sigmoid(d_b.astype(jnp.float32))
    d_logdecay = -jnp.exp(dn_A_log.astype(jnp.float32)) * jax.nn.softplus(
        d_a.astype(jnp.float32) + dn_dt_bias)
    o_d = gated_delta_chunked(dq, dk, dv, d_beta, d_logdecay)
    y_d = rms_norm(o_d, dn_norm_w).reshape(Bsz, S, W_DELTA).astype(x.dtype) * jax.nn.silu(d_g)

    mix = jnp.concatenate([y_a, y_b, y_c, y_d], axis=-1) @ w_out
    return layer_norm(DEEPNORM_ALPHA * x + mix, ln_g, ln_b)


def setup_inputs(seed: int = 0) -> dict:
    key = jax.random.key(seed)
    ks = jax.random.split(key, 20)
    f32 = jnp.float32

    def inv_softplus_dt(k, shape):
        dt = jnp.exp(jax.random.uniform(k, shape, f32, math.log(1e-3), math.log(1e-1)))
        return dt + jnp.log(-jnp.expm1(-dt))

    return {
        "x": jax.random.normal(ks[0], (BATCH, SEQ, D_MODEL), f32),
        "w_in": jax.random.normal(ks[1], (DEPTH, D_MODEL, IN_COLS), f32) * D_MODEL ** -0.5,
        "w_out": jax.random.normal(ks[2], (DEPTH, D_MIX, D_MODEL), f32) * (D_MIX ** -0.5) * DEEPNORM_BETA,
        "ssm_conv_w": jax.random.normal(ks[3], (DEPTH, CONV_WIDTH, SSD_XBC), f32) * CONV_WIDTH ** -0.5,
        "ssm_conv_b": jax.random.normal(ks[4], (DEPTH, SSD_XBC), f32) * 0.01,
        "ssm_dt_bias": inv_softplus_dt(ks[5], (DEPTH, N_HEADS_SSD)),
        "ssm_A_log": jnp.log(jax.random.uniform(ks[6], (DEPTH, N_HEADS_SSD), f32, 1.0, 16.0)),
        "ssm_D": 1.0 + 0.01 * jax.random.normal(ks[7], (DEPTH, N_HEADS_SSD), f32),
        "ssm_norm_w": 1.0 + 0.01 * jax.random.normal(ks[8], (DEPTH, W_SSD), f32),
        "dn_conv_w": jax.random.normal(ks[9], (DEPTH, CONV_WIDTH, DELTA_QKV), f32) * CONV_WIDTH ** -0.5,
        "dn_conv_b": jax.random.normal(ks[10], (DEPTH, DELTA_QKV), f32) * 0.01,
        "dn_dt_bias": inv_softplus_dt(ks[11], (DEPTH, N_HEADS_DELTA)),
        "dn_A_log": jnp.log(jax.random.uniform(ks[12], (DEPTH, N_HEADS_DELTA), f32, 1.0, 16.0)),
        "dn_norm_w": 1.0 + 0.01 * jax.random.normal(ks[13], (DEPTH, HEAD_DIM), f32),
        "ln_g": 1.0 + 0.01 * jax.random.normal(ks[14], (DEPTH, D_MODEL), f32),
        "ln_b": 0.01 * jax.random.normal(ks[15], (DEPTH, D_MODEL), f32),
    }


def reference(x, w_in, w_out, ssm_conv_w, ssm_conv_b, ssm_dt_bias, ssm_A_log, ssm_D,
              ssm_norm_w, dn_conv_w, dn_conv_b, dn_dt_bias, dn_A_log, dn_norm_w, ln_g, ln_b):
    for l in range(DEPTH):
        x = hybrid_layer(x, w_in[l], w_out[l], ssm_conv_w[l], ssm_conv_b[l], ssm_dt_bias[l],
                         ssm_A_log[l], ssm_D[l], ssm_norm_w[l], dn_conv_w[l], dn_conv_b[l],
                         dn_dt_bias[l], dn_A_log[l], dn_norm_w[l], ln_g[l], ln_b[l])
    return x
```

```python
import functools
import math

import numpy as np
import jax
import jax.numpy as jnp
from jax import lax
from jax.experimental import pallas as pl
from jax.experimental.pallas import tpu as pltpu

F32 = jnp.float32
BF16 = jnp.bfloat16
HIGHEST = lax.Precision.HIGHEST

D_MODEL = 1024
DEPTH = 2
HEAD_DIM = 64
N_HEADS = 6
W_MIX = N_HEADS * HEAD_DIM
N_PAIRS = W_MIX // 128
D_MIX = 4 * W_MIX
MOBA_BLOCK = 256
MOBA_TOPK = 3
SSD_STATE = 128
SSD_GROUPS = 2
SSD_CHUNK = 128
SSD_XBC = W_MIX + 2 * SSD_GROUPS * SSD_STATE
CONV_WIDTH = 4
DIL_GROUPS = 3
DIL_SPAN = 128
DELTA_CHUNK = 64
LN_EPS = 1e-5
RMS_EPS = 1e-6
DEEPNORM_ALPHA = (2.0 * DEPTH) ** 0.25
ATTN_SCALE = HEAD_DIM ** -0.5
NEG = -1e30

LANES = 128
TAIL = 8
VMEM_LIMIT = 48 * 1024 * 1024

C_MOBA = 0
C_SSD_X = 1536
C_SSD_Z = 1920
C_DN_QKV = 2304
C_DN_G = 3456
C_DIL = 3840
C_SSD_B = 5376
C_SSD_C = 5632
C_SMALL = 5888
PROJ_COLS = 6144
SM_DT, SM_BETA, SM_DECAY = 0, 6, 12


def _repack_w_in(w):
    o = np.cumsum([0, 384, 384, 384, 384, SSD_XBC, 384, 6, 384, 384, 384, 384, 1152, 384, 6, 6])
    a_q, _, _, _, s_xbc, s_z, s_dt, c_q, _, _, _, d_qkv, d_g, d_b, d_a, end = [int(v) for v in o]
    pad = jnp.zeros((w.shape[0], PROJ_COLS - C_SMALL - 18), w.dtype)
    parts = [
        w[:, a_q:s_xbc],
        w[:, s_xbc:s_xbc + 384],
        w[:, s_z:s_dt],
        w[:, d_qkv:d_g],
        w[:, d_g:d_b],
        w[:, c_q:d_qkv],
        w[:, s_xbc + 384:s_z],
        w[:, s_dt:c_q], w[:, d_b:d_a], w[:, d_a:end], pad,
    ]
    return jnp.concatenate(parts, axis=1).astype(BF16)


def _silu(x):
    return x * jax.nn.sigmoid(x)


def _softplus(x):
    return jnp.maximum(x, 0.0) + jnp.log(1.0 + jnp.exp(-jnp.abs(x)))


def _bdot(a, b):
    return jnp.dot(a.astype(BF16), b.astype(BF16), preferred_element_type=F32)


def _bdot_nt(a, b):
    return lax.dot_general(a.astype(BF16), b.astype(BF16), (((1,), (1,)), ((), ())),
                           preferred_element_type=F32)


def _bdot_tn(a, b):
    return lax.dot_general(a.astype(BF16), b.astype(BF16), (((0,), (0,)), ((), ())),
                           preferred_element_type=F32)


def _fdot(a, b):
    return jnp.dot(a, b, precision=HIGHEST, preferred_element_type=F32)


def _causal_conv_silu(x, tail, w, b):
    L = x.shape[0]
    xx = jnp.concatenate([tail, x], axis=0)
    y = b
    for j in range(CONV_WIDTH):
        off = TAIL - (CONV_WIDTH - 1) + j
        y = y + w[j:j + 1, :] * xx[off:off + L, :]
    return _silu(y)


def _inproj_kernel(x_ref, w_ref, o_ref, xb_ref):
    @pl.when(pl.program_id(1) == 0)
    def _():
        xb_ref[...] = x_ref[...].astype(BF16)

    o_ref[...] = jnp.dot(xb_ref[...], w_ref[...], preferred_element_type=F32)


def _inproj(x2d, w):
    T = x2d.shape[0]
    tm = min(1024, T)
    tn = 1536
    return pl.pallas_call(
        _inproj_kernel,
        out_shape=jax.ShapeDtypeStruct((T, PROJ_COLS), F32),
        grid=(T // tm, PROJ_COLS // tn),
        in_specs=[pl.BlockSpec((tm, D_MODEL), lambda i, j: (i, 0)),
                  pl.BlockSpec((D_MODEL, tn), lambda i, j: (0, j))],
        out_specs=pl.BlockSpec((tm, tn), lambda i, j: (i, j)),
        scratch_shapes=[pltpu.VMEM((tm, D_MODEL), BF16)],
        compiler_params=pltpu.CompilerParams(
            dimension_semantics=("parallel", "arbitrary"), vmem_limit_bytes=VMEM_LIMIT),
        name="inproj",
    )(x2d, w)


def _moba_kernel(slopes_ref, q_ref, k_ref, v_ref, g_ref, o_ref, kb_ref, vb_ref, kmean_ref, *, nblk):
    p = pl.program_id(1)
    i = pl.program_id(2)
    L = MOBA_BLOCK

    @pl.when(i == 0)
    def _():
        kb_ref[...] = k_ref[...].astype(BF16)
        vb_ref[...] = v_ref[...].astype(BF16)
        for n in range(nblk):
            kmean_ref[n:n + 1, :] = jnp.mean(k_ref[n * L:(n + 1) * L, :], axis=0, keepdims=True)

    lane = lax.broadcasted_iota(jnp.int32, (1, LANES), 1)
    row = lax.broadcasted_iota(jnp.int32, (L, L), 0)
    col = lax.broadcasted_iota(jnp.int32, (L, L), 1)
    rel = (row - col).astype(F32)
    causal = col <= row
    bcol = lax.broadcasted_iota(jnp.int32, (L, nblk), 1)
    q = q_ref[...]
    kmean = kmean_ref[...]
    row0 = pl.multiple_of(i * L, L)
    k_own = kb_ref[pl.ds(row0, L), :]
    v_own = vb_ref[pl.ds(row0, L), :]

    qs, bias0, sels, slopes, init = [], [], [], [], []
    for h in range(2):
        slope = slopes_ref[2 * p + h]
        head = (lane >= HEAD_DIM * h) & (lane < HEAD_DIM * (h + 1))
        qh = jnp.where(head, q, 0.0)
        gate = lax.dot_general(qh, kmean, (((1,), (1,)), ((), ())), precision=HIGHEST,
                               preferred_element_type=F32)
        gate = jnp.where(bcol < i, gate, -jnp.inf)
        sel_h = []
        for kk in range(MOBA_TOPK):
            mx = jnp.max(gate, axis=1, keepdims=True)
            idx = jnp.min(jnp.where(gate == mx, bcol, nblk), axis=1, keepdims=True)
            sel_h.append(jnp.where(kk < i, idx, -1))
            gate = jnp.where(bcol == idx, -jnp.inf, gate)
        sels.append(sel_h)
        qs.append((qh * ATTN_SCALE).astype(BF16))
        bias0.append(-slope * rel)
        slopes.append(slope)
        s = _bdot_nt(qs[h], k_own) + bias0[h]
        s = jnp.where(causal, s, NEG)
        m = jnp.max(s, axis=1, keepdims=True)
        e = jnp.exp(s - m)
        init += [m, jnp.sum(e, axis=1, keepdims=True), _bdot(e, v_own)]

    def body(n, carry):
        start = pl.multiple_of(n * L, L)
        kblk = kb_ref[pl.ds(start, L), :]
        vblk = vb_ref[pl.ds(start, L), :]
        blocks_back = (i - n).astype(F32)
        out = []
        for h in range(2):
            m, l, acc = carry[3 * h:3 * h + 3]
            picked = (sels[h][0] == n) | (sels[h][1] == n) | (sels[h][2] == n)
            s = _bdot_nt(qs[h], kblk) + bias0[h]
            s = s + jnp.where(picked, -slopes[h] * L * blocks_back, NEG)
            m_new = jnp.maximum(m, jnp.max(s, axis=1, keepdims=True))
            alpha = jnp.exp(m - m_new)
            e = jnp.exp(s - m_new)
            l = alpha * l + jnp.sum(e, axis=1, keepdims=True)
            acc = alpha * acc + _bdot(e, vblk)
            out += [m_new, l, acc]
        return tuple(out)

    res = lax.fori_loop(0, i, body, tuple(init))
    out = jnp.where(lane < HEAD_DIM, res[2] / res[1], res[5] / res[4])
    o_ref[...] = out * _silu(g_ref[...])


def _moba(proj3, slopes):
    B, S, _ = proj3.shape
    L = MOBA_BLOCK
    nblk = S // L
    cb = C_MOBA // LANES
    return pl.pallas_call(
        functools.partial(_moba_kernel, nblk=nblk),
        out_shape=jax.ShapeDtypeStruct((B, S, W_MIX), F32),
        grid=(B, N_PAIRS, nblk),
        in_specs=[pl.BlockSpec(memory_space=pltpu.SMEM),
                  pl.BlockSpec((None, L, LANES), lambda b, p, i: (b, i, cb + p)),
                  pl.BlockSpec((None, S, LANES), lambda b, p, i: (b, 0, cb + N_PAIRS + p)),
                  pl.BlockSpec((None, S, LANES), lambda b, p, i: (b, 0, cb + 2 * N_PAIRS + p)),
                  pl.BlockSpec((None, L, LANES), lambda b, p, i: (b, i, cb + 3 * N_PAIRS + p))],
        out_specs=pl.BlockSpec((None, L, LANES), lambda b, p, i: (b, i, p)),
        scratch_shapes=[pltpu.VMEM((S, LANES), BF16), pltpu.VMEM((S, LANES), BF16),
                        pltpu.VMEM((nblk, LANES), F32)],
        compiler_params=pltpu.CompilerParams(
            dimension_semantics=("parallel", "parallel", "arbitrary"), vmem_limit_bytes=VMEM_LIMIT),
        name="moba",
    )(slopes, proj3, proj3, proj3, proj3)


def _dil_kernel(slopes_ref, q_ref, kc_ref, kp_ref, vc_ref, vp_ref, o_ref, lse_ref, *, tq, tiles_per_seq):
    g = pl.program_id(0)
    p = pl.program_id(2)
    t = pl.program_id(3)
    dil = jnp.left_shift(1, 2 * g)
    tps = jnp.right_shift(tiles_per_seq, 2 * g)
    first = lax.rem(t, tps) == 0
    SB = DIL_SPAN
    lane = lax.broadcasted_iota(jnp.int32, (1, LANES), 1)
    row = lax.broadcasted_iota(jnp.int32, (SB, SB), 0)
    col = lax.broadcasted_iota(jnp.int32, (SB, SB), 1)
    rel = (row - col).astype(F32)
    in_cur = col <= row
    in_prev = col >= row
    in_prev_first = in_prev & jnp.logical_not(first)
    for j in range(tq // SB):
        rows = slice(j * SB, (j + 1) * SB)
        q = q_ref[rows, :]
        kc = kc_ref[rows, :]
        vc = vc_ref[rows, :]
        if j == 0:
            kp, vp, pmask = kp_ref[...], vp_ref[...], in_prev_first
        else:
            prev = slice((j - 1) * SB, j * SB)
            kp, vp, pmask = kc_ref[prev, :], vc_ref[prev, :], in_prev
        outs, lses = [], []
        for h in range(2):
            slope = slopes_ref[2 * p + h] * dil.astype(F32)
            head = (lane >= HEAD_DIM * h) & (lane < HEAD_DIM * (h + 1))
            qh = (jnp.where(head, q, 0.0) * ATTN_SCALE).astype(BF16)
            s_c = jnp.where(in_cur, _bdot_nt(qh, kc) - slope * rel, NEG)
            s_p = jnp.where(pmask, _bdot_nt(qh, kp) - slope * (rel + SB), NEG)
            m = jnp.maximum(jnp.max(s_c, axis=1, keepdims=True), jnp.max(s_p, axis=1, keepdims=True))
            e_c = jnp.exp(s_c - m)
            e_p = jnp.exp(s_p - m)
            den = jnp.sum(e_c, axis=1, keepdims=True) + jnp.sum(e_p, axis=1, keepdims=True)
            outs.append((_bdot(e_c, vc) + _bdot(e_p, vp)) / den)
            lses.append(m + jnp.log(den))
        o_ref[rows, :] = jnp.where(lane < HEAD_DIM, outs[0], outs[1])
        lse_ref[rows, :] = jnp.where(lane < HEAD_DIM, lses[0], lses[1])


def _dilated(qkv, slopes):
    G, B, S, _ = qkv.shape
    SB = DIL_SPAN
    tq = min(512, S // 16)
    r = tq // SB
    nt = S // tq
    out_sd = jax.ShapeDtypeStruct((G, B, S, W_MIX), F32)
    cur = lambda off: pl.BlockSpec((None, None, tq, LANES), lambda g, b, p, t: (g, b, t, off + p))
    prev = lambda off: pl.BlockSpec((None, None, SB, LANES),
                                    lambda g, b, p, t: (g, b, jnp.maximum(t * r - 1, 0), off + p))
    return pl.pallas_call(
        functools.partial(_dil_kernel, tq=tq, tiles_per_seq=nt),
        out_shape=(out_sd, out_sd),
        grid=(G, B, N_PAIRS, nt),
        in_specs=[pl.BlockSpec(memory_space=pltpu.SMEM),
                  cur(0), cur(N_PAIRS), prev(N_PAIRS), cur(2 * N_PAIRS), prev(2 * N_PAIRS)],
        out_specs=(pl.BlockSpec((None, None, tq, LANES), lambda g, b, p, t: (g, b, t, p)),
                   pl.BlockSpec((None, None, tq, LANES), lambda g, b, p, t: (g, b, t, p))),
        compiler_params=pltpu.CompilerParams(
            dimension_semantics=("parallel", "parallel", "parallel", "parallel"),
            vmem_limit_bytes=VMEM_LIMIT),
        name="dilated",
    )(slopes, qkv, qkv, qkv, qkv, qkv)


def _deinterleave(a, d):
    B, S, C = a.shape
    return a.reshape(B, S // d, d, C).transpose(0, 2, 1, 3).reshape(B, S, C)


def _interleave(a, d):
    B, S, C = a.shape
    return a.reshape(B, d, S // d, C).transpose(0, 2, 1, 3).reshape(B, S, C)


def _ssd_kernel(x_ref, z_ref, b_ref, c_ref, sm_ref, cw_ref, cb_ref, dtb_ref, alog_ref, dskip_ref, nw_ref,
                o_ref, tx_ref, tb_ref, tc_ref, st_ref):
    L = SSD_CHUNK
    N = SSD_STATE

    @pl.when(pl.program_id(1) == 0)
    def _():
        tx_ref[...] = jnp.zeros_like(tx_ref)
        tb_ref[...] = jnp.zeros_like(tb_ref)
        tc_ref[...] = jnp.zeros_like(tc_ref)
        st_ref[...] = jnp.zeros_like(st_ref)

    cw = cw_ref[...]
    cb = cb_ref[...]
    xr, br, cr = x_ref[...], b_ref[...], c_ref[...]
    xs = _causal_conv_silu(xr, tx_ref[...], cw[:, :W_MIX], cb[:, :W_MIX])
    Bm = _causal_conv_silu(br, tb_ref[...], cw[:, W_MIX:W_MIX + 2 * N], cb[:, W_MIX:W_MIX + 2 * N])
    Cm = _causal_conv_silu(cr, tc_ref[...], cw[:, W_MIX + 2 * N:], cb[:, W_MIX + 2 * N:])
    tx_ref[...] = xr[L - TAIL:, :]
    tb_ref[...] = br[L - TAIL:, :]
    tc_ref[...] = cr[L - TAIL:, :]

    dt = _softplus(sm_ref[...] + dtb_ref[...])
    a = dt * (-jnp.exp(alog_ref[...]))
    row = lax.broadcasted_iota(jnp.int32, (L, L), 0)
    col = lax.broadcasted_iota(jnp.int32, (L, L), 1)
    causal = col <= row
    a_cum = _fdot(causal.astype(F32), a)
    e_row = lax.broadcasted_iota(jnp.int32, (LANES, W_MIX), 0)
    e_col = lax.broadcasted_iota(jnp.int32, (LANES, W_MIX), 1)
    expand = (e_col // HEAD_DIM == e_row).astype(F32)
    dt_w = _fdot(dt, expand)
    acum_w = _fdot(a_cum, expand)
    atot_w = acum_w[L - 1:L, :]
    xdt = xs * dt_w
    a_cum_t = a_cum.T

    lane = lax.broadcasted_iota(jnp.int32, (1, LANES), 1)
    lane_w = lax.broadcasted_iota(jnp.int32, (1, W_MIX), 1)
    scores = [_bdot_nt(Cm[:, g * N:(g + 1) * N], Bm[:, g * N:(g + 1) * N]) for g in range(SSD_GROUPS)]
    y_pairs = []
    for p in range(N_PAIRS):
        res = []
        for hh in range(2):
            h = 2 * p + hh
            diff = a_cum[:, h:h + 1] - a_cum_t[h:h + 1, :]
            lmat = jnp.exp(jnp.where(causal, diff, NEG))
            res.append(_bdot(scores[h // (N_HEADS // SSD_GROUPS)] * lmat, xdt[:, p * LANES:(p + 1) * LANES]))
        y_pairs.append(jnp.where(lane < HEAD_DIM, res[0], res[1]))
    y_diag = jnp.concatenate(y_pairs, axis=1)

    state = st_ref[...]
    in_g0 = lane_w < W_MIX // SSD_GROUPS
    y_off = jnp.where(in_g0, _bdot(Cm[:, :N], state), _bdot(Cm[:, N:], state)) * jnp.exp(acum_w)
    xdec = xdt * jnp.exp(atot_w - acum_w)
    contrib = jnp.where(in_g0, _bdot_tn(Bm[:, :N], xdec), _bdot_tn(Bm[:, N:], xdec))
    st_ref[...] = state * jnp.exp(atot_w) + contrib

    y = y_diag + y_off + dskip_ref[...] * xs
    y = y * _silu(z_ref[...])
    y = y * lax.rsqrt(jnp.mean(y * y, axis=1, keepdims=True) + RMS_EPS) * nw_ref[...]
    o_ref[...] = y


def _pad_lanes(v, offset):
    return jnp.zeros((1, LANES), F32).at[0, offset:offset + v.shape[0]].set(v.astype(F32))


def _ssd(proj3, conv_w, conv_b, dt_bias, a_log, d_skip, norm_w):
    B, S, _ = proj3.shape
    L = SSD_CHUNK
    N2 = 2 * SSD_STATE
    full = lambda shape: pl.BlockSpec(shape, lambda b, c: (0, 0))
    return pl.pallas_call(
        _ssd_kernel,
        out_shape=jax.ShapeDtypeStruct((B, S, W_MIX), F32),
        grid=(B, S // L),
        in_specs=[pl.BlockSpec((None, L, W_MIX), lambda b, c: (b, c, C_SSD_X // W_MIX)),
                  pl.BlockSpec((None, L, W_MIX), lambda b, c: (b, c, C_SSD_Z // W_MIX)),
                  pl.BlockSpec((None, L, N2), lambda b, c: (b, c, C_SSD_B // N2)),
                  pl.BlockSpec((None, L, N2), lambda b, c: (b, c, C_SSD_C // N2)),
                  pl.BlockSpec((None, L, LANES), lambda b, c: (b, c, C_SMALL // LANES)),
                  full((CONV_WIDTH, SSD_XBC)), full((1, SSD_XBC)), full((1, LANES)), full((1, LANES)),
                  full((1, W_MIX)), full((1, W_MIX))],
        out_specs=pl.BlockSpec((None, L, W_MIX), lambda b, c: (b, c, 0)),
        scratch_shapes=[pltpu.VMEM((TAIL, W_MIX), F32), pltpu.VMEM((TAIL, N2), F32),
                        pltpu.VMEM((TAIL, N2), F32), pltpu.VMEM((SSD_STATE, W_MIX), F32)],
        compiler_params=pltpu.CompilerParams(
            dimension_semantics=("parallel", "arbitrary"), vmem_limit_bytes=VMEM_LIMIT),
        name="ssd",
    )(proj3, proj3, proj3, proj3, proj3, conv_w, conv_b.reshape(1, -1),
      _pad_lanes(dt_bias, SM_DT), _pad_lanes(a_log, SM_DT),
      jnp.repeat(d_skip.astype(F32), HEAD_DIM).reshape(1, -1), norm_w.reshape(1, -1))


def _unit_lower_inverse(lm):
    C = lm.shape[0]
    row = lax.broadcasted_iota(jnp.int32, (C, C), 0)
    col = lax.broadcasted_iota(jnp.int32, (C, C), 1)
    eye = (row == col).astype(F32)
    on_diag_block = (row // 16) == (col // 16)
    d = jnp.where(on_diag_block, lm, 0.0)
    off = lm - d
    d2 = _fdot(d, d)
    d4 = _fdot(d2, d2)
    d8 = _fdot(d4, d4)
    dinv = _fdot(_fdot(eye - d, eye + d2), _fdot(eye + d4, eye + d8))
    m = _fdot(dinv, off)
    m2 = _fdot(m, m)
    return _fdot(_fdot(eye - m, eye + m2), dinv)


def _gdn_kernel(qkv_ref, g_ref, sm_ref, cw_ref, cb_ref, dtb_ref, alog_ref, nw_ref, o_ref, tail_ref, st_ref, *, tl):
    C = DELTA_CHUNK

    @pl.when(pl.program_id(1) == 0)
    def _():
        tail_ref[...] = jnp.zeros_like(tail_ref)
        st_ref[...] = jnp.zeros_like(st_ref)

    raw = qkv_ref[...]
    qkv = _causal_conv_silu(raw, tail_ref[...], cw_ref[...], cb_ref[...])
    tail_ref[...] = raw[tl - TAIL:, :]
    q, k, v = qkv[:, :W_MIX], qkv[:, W_MIX:2 * W_MIX], qkv[:, 2 * W_MIX:]

    w_row = lax.broadcasted_iota(jnp.int32, (W_MIX, W_MIX), 0)
    w_col = lax.broadcasted_iota(jnp.int32, (W_MIX, W_MIX), 1)
    same_head = (w_row // HEAD_DIM == w_col // HEAD_DIM).astype(F32)
    q = q * lax.rsqrt(_fdot(q * q, same_head) + RMS_EPS) * ATTN_SCALE
    k = k * lax.rsqrt(_fdot(k * k, same_head) + RMS_EPS)

    sm = sm_ref[...]
    beta = jax.nn.sigmoid(sm)
    gdec = -jnp.exp(alog_ref[...]) * _softplus(sm + dtb_ref[...])
    t_row = lax.broadcasted_iota(jnp.int32, (tl, tl), 0)
    t_col = lax.broadcasted_iota(jnp.int32, (tl, tl), 1)
    chunk_tri = ((t_col <= t_row) & (t_row // C == t_col // C)).astype(F32)
    gc = _fdot(chunk_tri, gdec)
    e_row = lax.broadcasted_iota(jnp.int32, (LANES, W_MIX), 0)
    e_col = lax.broadcasted_iota(jnp.int32, (LANES, W_MIX), 1)
    beta_w = _fdot(beta, (e_col // HEAD_DIM + SM_BETA == e_row).astype(F32))
    gc_w = _fdot(gc, (e_col // HEAD_DIM + SM_DECAY == e_row).astype(F32))
    gc_t = gc.T

    kb = k * beta_w
    vb = v * beta_w
    kbe = kb * jnp.exp(gc_w)
    qe = q * jnp.exp(gc_w)

    lane = lax.broadcasted_iota(jnp.int32, (1, LANES), 1)
    lo = lane < HEAD_DIM
    row = lax.broadcasted_iota(jnp.int32, (C, C), 0)
    col = lax.broadcasted_iota(jnp.int32, (C, C), 1)
    incl = col <= row
    strict = col < row
    s_row = lax.broadcasted_iota(jnp.int32, (LANES, LANES), 0)
    s_col = lax.broadcasted_iota(jnp.int32, (LANES, LANES), 1)
    pair_diag = (s_row // HEAD_DIM) == (s_col // HEAD_DIM)

    for c in range(tl // C):
        rows = slice(c * C, (c + 1) * C)
        last = (c + 1) * C - 1
        o_pairs = []
        for p in range(N_PAIRS):
            cols = slice(p * LANES, (p + 1) * LANES)
            q2, k2, kb2 = q[rows, cols], k[rows, cols], kb[rows, cols]
            u_h, w_h, attn_h = [], [], []
            for hh in range(2):
                h = 2 * p + hh
                head = lo if hh == 0 else jnp.logical_not(lo)
                gcol = gc[rows, SM_DECAY + h:SM_DECAY + h + 1]
                grow = gc_t[SM_DECAY + h:SM_DECAY + h + 1, rows]
                decay = jnp.exp(jnp.where(incl, gcol - grow, NEG))
                k_head = jnp.where(head, k2, 0.0)
                lm = jnp.where(strict, _bdot_nt(kb2, k_head) * decay, 0.0)
                t_inv = _unit_lower_inverse(lm)
                u_h.append(_fdot(t_inv, vb[rows, cols]))
                w_h.append(_fdot(t_inv, kbe[rows, cols]))
                attn_h.append(jnp.where(incl, _bdot_nt(q2, k_head) * decay, 0.0))
            u2 = jnp.where(lo, u_h[0], u_h[1])
            w2 = jnp.where(lo, w_h[0], w_h[1])
            state = st_ref[p]
            v_new = u2 - _bdot(w2, state)
            o2 = _bdot(qe[rows, cols], state) + jnp.where(lo, _bdot(attn_h[0], v_new), _bdot(attn_h[1], v_new))
            g_last = gc_w[last:last + 1, cols]
            k_dec = k2 * jnp.exp(g_last - gc_w[rows, cols])
            st_ref[p] = state * jnp.exp(g_last) + jnp.where(pair_diag, _bdot_tn(k_dec, v_new), 0.0)
            o_pairs.append(o2)
        o_ref[rows, :] = jnp.concatenate(o_pairs, axis=1)

    o = o_ref[...]
    o = o * lax.rsqrt(_fdot(o * o, same_head) * (1.0 / HEAD_DIM) + RMS_EPS) * nw_ref[...]
    o_ref[...] = o * _silu(g_ref[...])


def _gdn(proj3, conv_w, conv_b, dt_bias, a_log, norm_w):
    B, S, _ = proj3.shape
    tl = 128
    full = lambda shape: pl.BlockSpec(shape, lambda b, c: (0, 0))
    return pl.pallas_call(
        functools.partial(_gdn_kernel, tl=tl),
        out_shape=jax.ShapeDtypeStruct((B, S, W_MIX), F32),
        grid=(B, S // tl),
        in_specs=[pl.BlockSpec((None, tl, 3 * W_MIX), lambda b, c: (b, c, C_DN_QKV // (3 * W_MIX))),
                  pl.BlockSpec((None, tl, W_MIX), lambda b, c: (b, c, C_DN_G // W_MIX)),
                  pl.BlockSpec((None, tl, LANES), lambda b, c: (b, c, C_SMALL // LANES)),
                  full((CONV_WIDTH, 3 * W_MIX)), full((1, 3 * W_MIX)), full((1, LANES)), full((1, LANES)),
                  full((1, W_MIX))],
        out_specs=pl.BlockSpec((None, tl, W_MIX), lambda b, c: (b, c, 0)),
        scratch_shapes=[pltpu.VMEM((TAIL, 3 * W_MIX), F32), pltpu.VMEM((N_PAIRS, LANES, LANES), F32)],
        compiler_params=pltpu.CompilerParams(
            dimension_semantics=("parallel", "arbitrary"), vmem_limit_bytes=VMEM_LIMIT),
        name="gdn",
    )(proj3, proj3, proj3, conv_w, conv_b.reshape(1, -1),
      _pad_lanes(dt_bias, SM_DECAY), _pad_lanes(a_log, SM_DECAY), jnp.tile(norm_w.astype(F32), N_HEADS).reshape(1, -1))


def _outproj_kernel(x_ref, ya_ref, yb_ref, yd_ref, dil_ref, lse_ref, cg_ref, w_ref, lg_ref, lb_ref, o_ref):
    lse = [lse_ref[g] for g in range(DIL_GROUPS)]
    m = jnp.maximum(jnp.maximum(lse[0], lse[1]), lse[2])
    e = [jnp.exp(l - m) for l in lse]
    yc = (e[0] * dil_ref[0] + e[1] * dil_ref[1] + e[2] * dil_ref[2]) / (e[0] + e[1] + e[2])
    yc = yc * _silu(cg_ref[...])
    mix = (_bdot(ya_ref[...], w_ref[0]) + _bdot(yb_ref[...], w_ref[1])
           + _bdot(yc, w_ref[2]) + _bdot(yd_ref[...], w_ref[3]))
    r = DEEPNORM_ALPHA * x_ref[...] + mix
    mu = jnp.mean(r, axis=1, keepdims=True)
    var = jnp.mean(jnp.square(r - mu), axis=1, keepdims=True)
    o_ref[...] = (r - mu) * lax.rsqrt(var + LN_EPS) * lg_ref[...] + lb_ref[...]


def _outproj(x2d, ya, yb, yd, dil_o, dil_lse, proj2d, w_out, ln_g, ln_b):
    T = x2d.shape[0]
    tm = min(512, T)
    rowblk = lambda width, cblk=0: pl.BlockSpec((tm, width), lambda i: (i, cblk))
    grp = pl.BlockSpec((DIL_GROUPS, tm, W_MIX), lambda i: (0, i, 0))
    return pl.pallas_call(
        _outproj_kernel,
        out_shape=jax.ShapeDtypeStruct((T, D_MODEL), F32),
        grid=(T // tm,),
        in_specs=[rowblk(D_MODEL), rowblk(W_MIX), rowblk(W_MIX), rowblk(W_MIX), grp, grp,
                  rowblk(W_MIX, (C_DIL + 3 * W_MIX) // W_MIX),
                  pl.BlockSpec((4, W_MIX, D_MODEL), lambda i: (0, 0, 0)),
                  pl.BlockSpec((1, D_MODEL), lambda i: (0, 0)), pl.BlockSpec((1, D_MODEL), lambda i: (0, 0))],
        out_specs=rowblk(D_MODEL),
        compiler_params=pltpu.CompilerParams(
            dimension_semantics=("parallel",), vmem_limit_bytes=VMEM_LIMIT),
        name="outproj",
    )(x2d, ya, yb, yd, dil_o, dil_lse, proj2d, w_out, ln_g.reshape(1, -1), ln_b.reshape(1, -1))


def _alibi_slopes():
    n = 2 * N_HEADS
    s = (2.0 ** (-8.0 * (np.arange(n) + 1) / n)).astype(np.float32)
    return jnp.asarray(s[:N_HEADS]), jnp.asarray(s[N_HEADS:])


def _layer(x, w_in, w_out, ssm_conv_w, ssm_conv_b, ssm_dt_bias, ssm_A_log, ssm_D, ssm_norm_w,
           dn_conv_w, dn_conv_b, dn_dt_bias, dn_A_log, dn_norm_w, ln_g, ln_b):
    B, S, _ = x.shape
    T = B * S
    slopes_dil, slopes_moba = _alibi_slopes()
    x2d = x.reshape(T, D_MODEL)
    proj2d = _inproj(x2d, _repack_w_in(w_in))
    proj3 = proj2d.reshape(B, S, PROJ_COLS)

    ya = _moba(proj3, slopes_moba)
    yb = _ssd(proj3, ssm_conv_w, ssm_conv_b, ssm_dt_bias, ssm_A_log, ssm_D, ssm_norm_w)
    yd = _gdn(proj3, dn_conv_w, dn_conv_b, dn_dt_bias, dn_A_log, dn_norm_w)

    dqkv = proj3[:, :, C_DIL:C_DIL + 3 * W_MIX]
    dil_in = jnp.stack([dqkv, _deinterleave(dqkv, 4), _deinterleave(dqkv, 16)])
    dil_o, dil_lse = _dilated(dil_in, slopes_dil)
    back = lambda a: jnp.stack([a[0], _interleave(a[1], 4), _interleave(a[2], 16)]).reshape(DIL_GROUPS, T, W_MIX)

    out = _outproj(x2d, ya.reshape(T, W_MIX), yb.reshape(T, W_MIX), yd.reshape(T, W_MIX),
                   back(dil_o), back(dil_lse), proj2d,
                   w_out.reshape(4, W_MIX, D_MODEL).astype(BF16), ln_g, ln_b)
    return out.reshape(B, S, D_MODEL)


def kernel(x, w_in, w_out, ssm_conv_w, ssm_conv_b, ssm_dt_bias, ssm_A_log, ssm_D, ssm_norm_w,
           dn_conv_w, dn_conv_b, dn_dt_bias, dn_A_log, dn_norm_w, ln_g, ln_b):
    assert x.shape[1] % 2048 == 0 and x.shape[2] == D_MODEL
    for l in range(DEPTH):
        x = _layer(x, w_in[l], w_out[l], ssm_conv_w[l], ssm_conv_b[l], ssm_dt_bias[l], ssm_A_log[l],
                   ssm_D[l], ssm_norm_w[l], dn_conv_w[l], dn_conv_b[l], dn_dt_bias[l], dn_A_log[l],
                   dn_norm_w[l], ln_g[l], ln_b[l])
    return x
```

```python
import functools
import math

import numpy as np
import jax
import jax.numpy as jnp
from jax import lax
from jax.experimental import pallas as pl
from jax.experimental.pallas import tpu as pltpu

F32 = jnp.float32
BF16 = jnp.bfloat16
HIGHEST = lax.Precision.HIGHEST

D_MODEL = 1024
DEPTH = 2
HEAD_DIM = 64
N_HEADS = 6
W_MIX = N_HEADS * HEAD_DIM
N_PAIRS = W_MIX // 128
D_MIX = 4 * W_MIX
MOBA_BLOCK = 256
MOBA_TOPK = 3
MOBA_NFEAT = 3
MOBA_UNROLL = 4
MOBA_LROWS = 16
SSD_STATE = 128
SSD_GROUPS = 2
SSD_CHUNK = 128
SSD_XBC = W_MIX + 2 * SSD_GROUPS * SSD_STATE
CONV_WIDTH = 4
DIL_GROUPS = 3
DIL_SPAN = 128
DELTA_CHUNK = 64
LN_EPS = 1e-5
RMS_EPS = 1e-6
DEEPNORM_ALPHA = (2.0 * DEPTH) ** 0.25
ATTN_SCALE = HEAD_DIM ** -0.5
NEG = -1e30

LANES = 128
TAIL = 8
VMEM_LIMIT = 48 * 1024 * 1024

C_MOBA = 0
C_SSD_X = 1536
C_SSD_Z = 1920
C_DN_QKV = 2304
C_DN_G = 3456
C_DIL = 3840
C_SSD_B = 5376
C_SSD_C = 5632
C_SMALL = 5888
PROJ_COLS = 6144
SM_DT, SM_BETA, SM_DECAY = 0, 6, 12


def _repack_w_in(w):
    o = np.cumsum([0, 384, 384, 384, 384, SSD_XBC, 384, 6, 384, 384, 384, 384, 1152, 384, 6, 6])
    a_q, _, _, _, s_xbc, s_z, s_dt, c_q, _, _, _, d_qkv, d_g, d_b, d_a, end = [int(v) for v in o]
    pad = jnp.zeros((w.shape[0], PROJ_COLS - C_SMALL - 18), w.dtype)
    parts = [
        w[:, a_q:s_xbc],
        w[:, s_xbc:s_xbc + 384],
        w[:, s_z:s_dt],
        w[:, d_qkv:d_g],
        w[:, d_g:d_b],
        w[:, c_q:d_qkv],
        w[:, s_xbc + 384:s_z],
        w[:, s_dt:c_q], w[:, d_b:d_a], w[:, d_a:end], pad,
    ]
    return jnp.concatenate(parts, axis=1).astype(BF16)


def _silu(x):
    return x * jax.nn.sigmoid(x)


def _softplus(x):
    return jnp.maximum(x, 0.0) + jnp.log(1.0 + jnp.exp(-jnp.abs(x)))


def _bdot(a, b):
    return jnp.dot(a.astype(BF16), b.astype(BF16), preferred_element_type=F32)


def _bdot_nt(a, b):
    return lax.dot_general(a.astype(BF16), b.astype(BF16), (((1,), (1,)), ((), ())),
                           preferred_element_type=F32)


def _bdot_tn(a, b):
    return lax.dot_general(a.astype(BF16), b.astype(BF16), (((0,), (0,)), ((), ())),
                           preferred_element_type=F32)


def _fdot(a, b):
    return jnp.dot(a, b, precision=HIGHEST, preferred_element_type=F32)


def _causal_conv_silu(x, tail, w, b):
    L = x.shape[0]
    xx = jnp.concatenate([tail, x], axis=0)
    y = b
    for j in range(CONV_WIDTH):
        off = TAIL - (CONV_WIDTH - 1) + j
        y = y + w[j:j + 1, :] * xx[off:off + L, :]
    return _silu(y)


def _inproj_kernel(x_ref, w_ref, o_ref, xb_ref):
    @pl.when(pl.program_id(1) == 0)
    def _():
        xb_ref[...] = x_ref[...].astype(BF16)

    o_ref[...] = jnp.dot(xb_ref[...], w_ref[...], preferred_element_type=F32)


def _inproj(x2d, w):
    T = x2d.shape[0]
    tm = min(1024, T)
    tn = 1536
    return pl.pallas_call(
        _inproj_kernel,
        out_shape=jax.ShapeDtypeStruct((T, PROJ_COLS), F32),
        grid=(T // tm, PROJ_COLS // tn),
        in_specs=[pl.BlockSpec((tm, D_MODEL), lambda i, j: (i, 0)),
                  pl.BlockSpec((D_MODEL, tn), lambda i, j: (0, j))],
        out_specs=pl.BlockSpec((tm, tn), lambda i, j: (i, j)),
        scratch_shapes=[pltpu.VMEM((tm, D_MODEL), BF16)],
        compiler_params=pltpu.CompilerParams(
            dimension_semantics=("parallel", "arbitrary"), vmem_limit_bytes=VMEM_LIMIT),
        name="inproj",
    )(x2d, w)


def _moba_kernel(slopes_ref, q_ref, k_ref, v_ref, g_ref, o_ref, kaug_ref, vt_ref, kmean_ref, *, nblk):
    p = pl.program_id(1)
    i = pl.program_id(2)
    L = MOBA_BLOCK

    NF = MOBA_NFEAT

    @pl.when(i == 0)
    def _():
        lane = lax.broadcasted_iota(jnp.int32, (L, LANES), 1)
        feat_lane = lane % HEAD_DIM
        c_loc = lax.broadcasted_iota(jnp.int32, (L, LANES), 0).astype(F32)
        feat = jnp.where(feat_lane < NF, 1.0, jnp.where(feat_lane < 2 * NF, c_loc, 0.0))
        ones = jnp.ones((MOBA_LROWS, L), F32)
        for n in range(nblk):
            kblk = k_ref[n * L:(n + 1) * L, :]
            kmean_ref[n:n + 1, :] = jnp.mean(kblk, axis=0, keepdims=True)
            kaug_ref[0, n] = jnp.where(lane < HEAD_DIM, kblk, feat).astype(BF16)
            kaug_ref[1, n] = jnp.where(lane < HEAD_DIM, feat, kblk).astype(BF16)
            vt = v_ref[n * L:(n + 1) * L, :].T
            for h in range(2):
                vt_ref[h, n] = jnp.concatenate([ones, vt[h * HEAD_DIM:(h + 1) * HEAD_DIM, :]], axis=0).astype(BF16)

    qt = q_ref[...].T
    qrow = lax.broadcasted_iota(jnp.int32, (LANES, L), 0)
    r_loc = lax.broadcasted_iota(jnp.int32, (1, L), 1).astype(F32)
    brow = lax.broadcasted_iota(jnp.int32, (nblk, L), 0).astype(F32)
    frow = lax.broadcasted_iota(jnp.int32, (2 * TAIL, L), 0)
    kmean = kmean_ref[...]
    i_f = i.astype(F32)

    qparts, sels, slope_rows, zeros48 = [], [], [], jnp.zeros((HEAD_DIM - 2 * TAIL, L), BF16)
    for h in range(2):
        in_head = (qrow >= HEAD_DIM * h) & (qrow < HEAD_DIM * (h + 1))
        gate = jnp.dot(kmean, jnp.where(in_head, qt, 0.0), precision=HIGHEST, preferred_element_type=F32)
        gate = jnp.where(brow < i_f, gate, -jnp.inf)
        sel_h = []
        for kk in range(MOBA_TOPK):
            mx = jnp.max(gate, axis=0, keepdims=True)
            idx = jnp.min(jnp.where(gate == mx, brow, float(nblk)), axis=0, keepdims=True)
            sel_h.append(jnp.where(kk < i, idx, -1.0))
            gate = jnp.where(brow == idx, -jnp.inf, gate)
        sels.append(sel_h)
        qparts.append((qt[h * HEAD_DIM:(h + 1) * HEAD_DIM, :] * ATTN_SCALE).astype(BF16))
        slope_rows.append(jnp.full((1, L), slopes_ref[2 * p + h], F32))

    def q_aug(h, t_row):
        terms = _split3(t_row) + _split3(slope_rows[h])
        feat = jnp.zeros((2 * TAIL, L), F32)
        for f, term in enumerate(terms):
            feat = jnp.where(frow == f, term.astype(F32), feat)
        feat = feat.astype(BF16)
        if h == 0:
            return jnp.concatenate([qparts[0], feat, zeros48], axis=0)
        return jnp.concatenate([feat, zeros48, qparts[1]], axis=0)

    def attend(blocks, t_rows, carry, mask=None):
        nb = len(blocks)
        ss = [[jnp.dot(kaug_ref[h, n], q_aug(h, t_rows[h][j]), preferred_element_type=F32)
               for j, n in enumerate(blocks)] for h in range(2)]
        if mask is not None:
            ss = [[jnp.where(mask, s, NEG) for s in row] for row in ss]
        out = []
        smax = [functools.reduce(jnp.maximum, [jnp.max(s, axis=0, keepdims=True) for s in ss[h]]) for h in range(2)]
        m_new = [smax[h] if carry is None else jnp.maximum(carry[2 * h], smax[h]) for h in range(2)]
        ps = [[jnp.exp(s - m_new[h]).astype(BF16) for s in ss[h]] for h in range(2)]
        pv = [jnp.dot(jnp.concatenate([vt_ref[h, n] for n in blocks], axis=1) if nb > 1 else vt_ref[h, blocks[0]],
                      jnp.concatenate(ps[h], axis=0) if nb > 1 else ps[h][0], preferred_element_type=F32)
              for h in range(2)]
        for h in range(2):
            acc = pv[h] if carry is None else jnp.exp(carry[2 * h] - m_new[h]) * carry[2 * h + 1] + pv[h]
            out += [m_new[h], acc]
        return tuple(out)

    c_idx = lax.broadcasted_iota(jnp.int32, (L, L), 0)
    r_idx = lax.broadcasted_iota(jnp.int32, (L, L), 1)
    init = attend([i], [[-slope_rows[h] * r_loc] for h in range(2)], None, c_idx <= r_idx)

    def body(j, carry):
        blocks, t_rows = [], [[], []]
        for u in range(MOBA_UNROLL):
            n = MOBA_UNROLL * j + u
            n_f = n.astype(F32)
            dist0 = r_loc + L * (i_f - n_f)
            for h in range(2):
                picked = (sels[h][0] == n_f) | (sels[h][1] == n_f) | (sels[h][2] == n_f)
                t_rows[h].append(jnp.where(picked, -slope_rows[h] * dist0, NEG))
            blocks.append(jnp.minimum(n, nblk - 1))
        return attend(blocks, t_rows, carry)

    res = lax.fori_loop(0, (i + MOBA_UNROLL - 1) // MOBA_UNROLL, body, init)
    out_t = jnp.concatenate([res[2 * h + 1][MOBA_LROWS:] / res[2 * h + 1][0:1] for h in range(2)], axis=0)
    o_ref[...] = out_t.T * _silu(g_ref[...])


def _moba(proj3, slopes):
    B, S, _ = proj3.shape
    L = MOBA_BLOCK
    nblk = S // L
    cb = C_MOBA // LANES
    return pl.pallas_call(
        functools.partial(_moba_kernel, nblk=nblk),
        out_shape=jax.ShapeDtypeStruct((B, S, W_MIX), F32),
        grid=(B, N_PAIRS, nblk),
        in_specs=[pl.BlockSpec(memory_space=pltpu.SMEM),
                  pl.BlockSpec((None, L, LANES), lambda b, p, i: (b, i, cb + p)),
                  pl.BlockSpec((None, S, LANES), lambda b, p, i: (b, 0, cb + N_PAIRS + p)),
                  pl.BlockSpec((None, S, LANES), lambda b, p, i: (b, 0, cb + 2 * N_PAIRS + p)),
                  pl.BlockSpec((None, L, LANES), lambda b, p, i: (b, i, cb + 3 * N_PAIRS + p))],
        out_specs=pl.BlockSpec((None, L, LANES), lambda b, p, i: (b, i, p)),
        scratch_shapes=[pltpu.VMEM((2, nblk, L, LANES), BF16),
                        pltpu.VMEM((2, nblk, MOBA_LROWS + HEAD_DIM, L), BF16),
                        pltpu.VMEM((nblk, LANES), F32)],
        compiler_params=pltpu.CompilerParams(
            dimension_semantics=("parallel", "parallel", "arbitrary"), vmem_limit_bytes=VMEM_LIMIT),
        name="moba",
    )(slopes, proj3, proj3, proj3, proj3)


def _dil_kernel(slopes_ref, q_ref, kc_ref, kp_ref, vc_ref, vp_ref, o_ref, lse_ref, *, tq, tiles_per_seq):
    g = pl.program_id(0)
    p = pl.program_id(2)
    t = pl.program_id(3)
    dil = jnp.left_shift(1, 2 * g)
    tps = jnp.right_shift(tiles_per_seq, 2 * g)
    first = lax.rem(t, tps) == 0
    SB = DIL_SPAN
    lane = lax.broadcasted_iota(jnp.int32, (1, LANES), 1)
    row = lax.broadcasted_iota(jnp.int32, (SB, SB), 0)
    col = lax.broadcasted_iota(jnp.int32, (SB, SB), 1)
    rel = (row - col).astype(F32)
    in_cur = col <= row
    in_prev = col >= row
    in_prev_first = in_prev & jnp.logical_not(first)
    for j in range(tq // SB):
        rows = slice(j * SB, (j + 1) * SB)
        q = q_ref[rows, :]
        kc = kc_ref[rows, :]
        vc = vc_ref[rows, :]
        if j == 0:
            kp, vp, pmask = kp_ref[...], vp_ref[...], in_prev_first
        else:
            prev = slice((j - 1) * SB, j * SB)
            kp, vp, pmask = kc_ref[prev, :], vc_ref[prev, :], in_prev
        outs, lses = [], []
        for h in range(2):
            slope = slopes_ref[2 * p + h] * dil.astype(F32)
            head = (lane >= HEAD_DIM * h) & (lane < HEAD_DIM * (h + 1))
            qh = (jnp.where(head, q, 0.0) * ATTN_SCALE).astype(BF16)
            s_c = jnp.where(in_cur, _bdot_nt(qh, kc) - slope * rel, NEG)
            s_p = jnp.where(pmask, _bdot_nt(qh, kp) - slope * (rel + SB), NEG)
            m = jnp.maximum(jnp.max(s_c, axis=1, keepdims=True), jnp.max(s_p, axis=1, keepdims=True))
            e_c = jnp.exp(s_c - m)
            e_p = jnp.exp(s_p - m)
            den = jnp.sum(e_c, axis=1, keepdims=True) + jnp.sum(e_p, axis=1, keepdims=True)
            outs.append((_bdot(e_c, vc) + _bdot(e_p, vp)) / den)
            lses.append(m + jnp.log(den))
        o_ref[rows, :] = jnp.where(lane < HEAD_DIM, outs[0], outs[1])
        lse_ref[rows, :] = jnp.where(lane < HEAD_DIM, lses[0], lses[1])


def _dilated(qkv, slopes):
    G, B, S, _ = qkv.shape
    SB = DIL_SPAN
    tq = min(512, S // 16)
    r = tq // SB
    nt = S // tq
    out_sd = jax.ShapeDtypeStruct((G, B, S, W_MIX), F32)
    cur = lambda off: pl.BlockSpec((None, None, tq, LANES), lambda g, b, p, t: (g, b, t, off + p))
    prev = lambda off: pl.BlockSpec((None, None, SB, LANES),
                                    lambda g, b, p, t: (g, b, jnp.maximum(t * r - 1, 0), off + p))
    return pl.pallas_call(
        functools.partial(_dil_kernel, tq=tq, tiles_per_seq=nt),
        out_shape=(out_sd, out_sd),
        grid=(G, B, N_PAIRS, nt),
        in_specs=[pl.BlockSpec(memory_space=pltpu.SMEM),
                  cur(0), cur(N_PAIRS), prev(N_PAIRS), cur(2 * N_PAIRS), prev(2 * N_PAIRS)],
        out_specs=(pl.BlockSpec((None, None, tq, LANES), lambda g, b, p, t: (g, b, t, p)),
                   pl.BlockSpec((None, None, tq, LANES), lambda g, b, p, t: (g, b, t, p))),
        compiler_params=pltpu.CompilerParams(
            dimension_semantics=("parallel", "parallel", "parallel", "parallel"),
            vmem_limit_bytes=VMEM_LIMIT),
        name="dilated",
    )(slopes, qkv, qkv, qkv, qkv, qkv)


def _deinterleave(a, d):
    B, S, C = a.shape
    return a.reshape(B, S // d, d, C).transpose(0, 2, 1, 3).reshape(B, S, C)


def _interleave(a, d):
    B, S, C = a.shape
    return a.reshape(B, d, S // d, C).transpose(0, 2, 1, 3).reshape(B, S, C)


def _ssd_kernel(x_ref, z_ref, b_ref, c_ref, sm_ref, cw_ref, cb_ref, dtb_ref, alog_ref, dskip_ref, nw_ref,
                o_ref, tx_ref, tb_ref, tc_ref, st_ref):
    L = SSD_CHUNK
    N = SSD_STATE

    @pl.when(pl.program_id(1) == 0)
    def _():
        tx_ref[...] = jnp.zeros_like(tx_ref)
        tb_ref[...] = jnp.zeros_like(tb_ref)
        tc_ref[...] = jnp.zeros_like(tc_ref)
        st_ref[...] = jnp.zeros_like(st_ref)

    cw = cw_ref[...]
    cb = cb_ref[...]
    xr, br, cr = x_ref[...], b_ref[...], c_ref[...]
    xs = _causal_conv_silu(xr, tx_ref[...], cw[:, :W_MIX], cb[:, :W_MIX])
    Bm = _causal_conv_silu(br, tb_ref[...], cw[:, W_MIX:W_MIX + 2 * N], cb[:, W_MIX:W_MIX + 2 * N])
    Cm = _causal_conv_silu(cr, tc_ref[...], cw[:, W_MIX + 2 * N:], cb[:, W_MIX + 2 * N:])
    tx_ref[...] = xr[L - TAIL:, :]
    tb_ref[...] = br[L - TAIL:, :]
    tc_ref[...] = cr[L - TAIL:, :]

    dt = _softplus(sm_ref[...] + dtb_ref[...])
    a = dt * (-jnp.exp(alog_ref[...]))
    row = lax.broadcasted_iota(jnp.int32, (L, L), 0)
    col = lax.broadcasted_iota(jnp.int32, (L, L), 1)
    causal = col <= row
    a_cum = _fdot(causal.astype(F32), a)
    e_row = lax.broadcasted_iota(jnp.int32, (LANES, W_MIX), 0)
    e_col = lax.broadcasted_iota(jnp.int32, (LANES, W_MIX), 1)
    expand = (e_col // HEAD_DIM == e_row).astype(F32)
    dt_w = _fdot(dt, expand)
    acum_w = _fdot(a_cum, expand)
    atot_w = acum_w[L - 1:L, :]
    xdt = xs * dt_w
    a_cum_t = a_cum.T

    lane = lax.broadcasted_iota(jnp.int32, (1, LANES), 1)
    lane_w = lax.broadcasted_iota(jnp.int32, (1, W_MIX), 1)
    scores = [_bdot_nt(Cm[:, g * N:(g + 1) * N], Bm[:, g * N:(g + 1) * N]) for g in range(SSD_GROUPS)]
    y_pairs = []
    for p in range(N_PAIRS):
        res = []
        for hh in range(2):
            h = 2 * p + hh
            diff = a_cum[:, h:h + 1] - a_cum_t[h:h + 1, :]
            lmat = jnp.exp(jnp.where(causal, diff, NEG))
            res.append(_bdot(scores[h // (N_HEADS // SSD_GROUPS)] * lmat, xdt[:, p * LANES:(p + 1) * LANES]))
        y_pairs.append(jnp.where(lane < HEAD_DIM, res[0], res[1]))
    y_diag = jnp.concatenate(y_pairs, axis=1)

    state = st_ref[...]
    in_g0 = lane_w < W_MIX // SSD_GROUPS
    y_off = jnp.where(in_g0, _bdot(Cm[:, :N], state), _bdot(Cm[:, N:], state)) * jnp.exp(acum_w)
    xdec = xdt * jnp.exp(atot_w - acum_w)
    contrib = jnp.where(in_g0, _bdot_tn(Bm[:, :N], xdec), _bdot_tn(Bm[:, N:], xdec))
    st_ref[...] = state * jnp.exp(atot_w) + contrib

    y = y_diag + y_off + dskip_ref[...] * xs
    y = y * _silu(z_ref[...])
    y = y * lax.rsqrt(jnp.mean(y * y, axis=1, keepdims=True) + RMS_EPS) * nw_ref[...]
    o_ref[...] = y


def _pad_lanes(v, offset):
    return jnp.zeros((1, LANES), F32).at[0, offset:offset + v.shape[0]].set(v.astype(F32))


def _ssd(proj3, conv_w, conv_b, dt_bias, a_log, d_skip, norm_w):
    B, S, _ = proj3.shape
    L = SSD_CHUNK
    N2 = 2 * SSD_STATE
    full = lambda shape: pl.BlockSpec(shape, lambda b, c: (0, 0))
    return pl.pallas_call(
        _ssd_kernel,
        out_shape=jax.ShapeDtypeStruct((B, S, W_MIX), F32),
        grid=(B, S // L),
        in_specs=[pl.BlockSpec((None, L, W_MIX), lambda b, c: (b, c, C_SSD_X // W_MIX)),
                  pl.BlockSpec((None, L, W_MIX), lambda b, c: (b, c, C_SSD_Z // W_MIX)),
                  pl.BlockSpec((None, L, N2), lambda b, c: (b, c, C_SSD_B // N2)),
                  pl.BlockSpec((None, L, N2), lambda b, c: (b, c, C_SSD_C // N2)),
                  pl.BlockSpec((None, L, LANES), lambda b, c: (b, c, C_SMALL // LANES)),
                  full((CONV_WIDTH, SSD_XBC)), full((1, SSD_XBC)), full((1, LANES)), full((1, LANES)),
                  full((1, W_MIX)), full((1, W_MIX))],
        out_specs=pl.BlockSpec((None, L, W_MIX), lambda b, c: (b, c, 0)),
        scratch_shapes=[pltpu.VMEM((TAIL, W_MIX), F32), pltpu.VMEM((TAIL, N2), F32),
                        pltpu.VMEM((TAIL, N2), F32), pltpu.VMEM((SSD_STATE, W_MIX), F32)],
        compiler_params=pltpu.CompilerParams(
            dimension_semantics=("parallel", "arbitrary"), vmem_limit_bytes=VMEM_LIMIT),
        name="ssd",
    )(proj3, proj3, proj3, proj3, proj3, conv_w, conv_b.reshape(1, -1),
      _pad_lanes(dt_bias, SM_DT), _pad_lanes(a_log, SM_DT),
      jnp.repeat(d_skip.astype(F32), HEAD_DIM).reshape(1, -1), norm_w.reshape(1, -1))


INV_LEAF = 16


def _pair_blockdiag(x):
    lo = lax.broadcasted_iota(jnp.int32, (1, LANES), 1) < HEAD_DIM
    xb = x.astype(BF16)
    zero = jnp.zeros_like(xb)
    return jnp.concatenate([jnp.where(lo, xb, zero), jnp.where(lo, zero, xb)], axis=0)


def _pair_mm(a, b):
    return jnp.dot(a.astype(BF16), _pair_blockdiag(b), preferred_element_type=F32)


def _pair_unit_lower_inverse(lms, eye, on_diag_block):
    mm = lambda xs, ys: [_pair_mm(x, y) for x, y in zip(xs, ys)]
    d = [jnp.where(on_diag_block, lm, 0.0) for lm in lms]
    off = [lm - x for lm, x in zip(lms, d)]
    d2 = mm(d, d)
    d4 = mm(d2, d2)
    d8 = mm(d4, d4)
    left = mm([eye - x for x in d], [eye + x for x in d2])
    right = mm([eye + x for x in d4], [eye + x for x in d8])
    dinv = mm(left, right)
    m = mm(dinv, off)
    m2 = mm(m, m)
    return mm(mm([eye - x for x in m], [eye + x for x in m2]), dinv)


def _split3(x):
    hi = x.astype(BF16)
    r = x - hi.astype(F32)
    mid = r.astype(BF16)
    lo = (r - mid.astype(F32)).astype(BF16)
    return hi, mid, lo


def _xdot(a01, x):
    a = a01.astype(BF16)
    return sum(jnp.dot(a, t, preferred_element_type=F32) for t in _split3(x))


def _xdot_r(x, b01):
    b = b01.astype(BF16)
    return sum(jnp.dot(t, b, preferred_element_type=F32) for t in _split3(x))


def _gdn_kernel(qkv_ref, g_ref, sm_ref, cw_ref, cb_ref, dtb_ref, alog_ref, nw_ref, o_ref, tail_ref, st_ref, *, tl):
    C = DELTA_CHUNK

    @pl.when(pl.program_id(1) == 0)
    def _():
        tail_ref[...] = jnp.zeros_like(tail_ref)
        st_ref[...] = jnp.zeros_like(st_ref)

    raw = qkv_ref[...]
    qkv = _causal_conv_silu(raw, tail_ref[...], cw_ref[...], cb_ref[...])
    tail_ref[...] = raw[tl - TAIL:, :]
    q, k, v = qkv[:, :W_MIX], qkv[:, W_MIX:2 * W_MIX], qkv[:, 2 * W_MIX:]

    w_row = lax.broadcasted_iota(jnp.int32, (W_MIX, W_MIX), 0)
    w_col = lax.broadcasted_iota(jnp.int32, (W_MIX, W_MIX), 1)
    same_head = (w_row // HEAD_DIM == w_col // HEAD_DIM).astype(BF16)
    q = q * lax.rsqrt(_bdot(q * q, same_head) + RMS_EPS) * ATTN_SCALE
    k = k * lax.rsqrt(_bdot(k * k, same_head) + RMS_EPS)

    sm = sm_ref[...]
    beta = jax.nn.sigmoid(sm)
    gdec = -jnp.exp(alog_ref[...]) * _softplus(sm + dtb_ref[...])
    t_row = lax.broadcasted_iota(jnp.int32, (tl, tl), 0)
    t_col = lax.broadcasted_iota(jnp.int32, (tl, tl), 1)
    chunk_tri = (t_col <= t_row) & (t_row // C == t_col // C)
    gc = _xdot(chunk_tri, gdec)
    e_row = lax.broadcasted_iota(jnp.int32, (LANES, W_MIX), 0)
    e_col = lax.broadcasted_iota(jnp.int32, (LANES, W_MIX), 1)
    beta_w = _xdot_r(beta, e_col // HEAD_DIM + SM_BETA == e_row)
    gc_w = _xdot_r(gc, e_col // HEAD_DIM + SM_DECAY == e_row)

    kb = k * beta_w
    vb = v * beta_w
    egc = jnp.exp(gc_w)
    kbe = kb * egc
    qe = q * egc

    row = lax.broadcasted_iota(jnp.int32, (C, LANES), 0)
    col = lax.broadcasted_iota(jnp.int32, (C, LANES), 1) % HEAD_DIM
    eye = (row == col).astype(F32)
    incl = col <= row
    strict = col < row
    on_diag_block = (row // INV_LEAF) == (col // INV_LEAF)
    s_row = lax.broadcasted_iota(jnp.int32, (LANES, LANES), 0)
    s_col = lax.broadcasted_iota(jnp.int32, (LANES, LANES), 1)
    pair_diag = (s_row // HEAD_DIM) == (s_col // HEAD_DIM)

    tiles = [(slice(c * C, (c + 1) * C), slice(p * LANES, (p + 1) * LANES))
             for c in range(tl // C) for p in range(N_PAIRS)]
    grams, decays = [], []
    for rows, cols in tiles:
        gcol = gc_w[rows, cols]
        grow = jnp.sum(gcol * eye, axis=0, keepdims=True)
        decays.append(jnp.exp(jnp.where(incl, gcol - grow, NEG)))
        grams.append(lax.dot_general(jnp.concatenate([kb[rows, cols], q[rows, cols]], axis=0).astype(BF16),
                                     _pair_blockdiag(k[rows, cols]), (((1,), (1,)), ((), ())),
                                     preferred_element_type=F32))
    lms = [jnp.where(strict, g[:C] * d, 0.0) for g, d in zip(grams, decays)]
    attns = [jnp.where(incl, g[C:] * d, 0.0) for g, d in zip(grams, decays)]
    t_invs = _pair_unit_lower_inverse(lms, eye, on_diag_block)
    uws = [jnp.dot(t.astype(BF16),
                   jnp.concatenate([_pair_blockdiag(vb[rows, cols]), _pair_blockdiag(kbe[rows, cols])], axis=1),
                   preferred_element_type=F32) for t, (rows, cols) in zip(t_invs, tiles)]
    pre = []
    for (rows, cols), uw, attn in zip(tiles, uws, attns):
        g_last = gc_w[rows.stop - 1:rows.stop, cols]
        k_dec = k[rows, cols] * jnp.exp(g_last - gc_w[rows, cols])
        pre.append((uw[:, :LANES], uw[:, LANES:], attn, qe[rows, cols], k_dec, jnp.exp(g_last)))

    for c in range(tl // C):
        o_pairs = []
        for p in range(N_PAIRS):
            u2, w2, attn, qe2, k_dec2, last_decay = pre[c * N_PAIRS + p]
            state = st_ref[p]
            sb = state.astype(BF16)
            v_new = u2 - jnp.dot(w2.astype(BF16), sb, preferred_element_type=F32)
            o_pairs.append(jnp.dot(qe2.astype(BF16), sb, preferred_element_type=F32) + _pair_mm(attn, v_new))
            st_ref[p] = state * last_decay + jnp.where(pair_diag, _bdot_tn(k_dec2, v_new), 0.0)
        o_ref[c * C:(c + 1) * C, :] = jnp.concatenate(o_pairs, axis=1)

    o = o_ref[...]
    o = o * lax.rsqrt(_bdot(o * o, same_head) * (1.0 / HEAD_DIM) + RMS_EPS) * nw_ref[...]
    o_ref[...] = o * _silu(g_ref[...])


def _gdn(proj3, conv_w, conv_b, dt_bias, a_log, norm_w):
    B, S, _ = proj3.shape
    tl = 256
    full = lambda shape: pl.BlockSpec(shape, lambda b, c: (0, 0))
    return pl.pallas_call(
        functools.partial(_gdn_kernel, tl=tl),
        out_shape=jax.ShapeDtypeStruct((B, S, W_MIX), F32),
        grid=(B, S // tl),
        in_specs=[pl.BlockSpec((None, tl, 3 * W_MIX), lambda b, c: (b, c, C_DN_QKV // (3 * W_MIX))),
                  pl.BlockSpec((None, tl, W_MIX), lambda b, c: (b, c, C_DN_G // W_MIX)),
                  pl.BlockSpec((None, tl, LANES), lambda b, c: (b, c, C_SMALL // LANES)),
                  full((CONV_WIDTH, 3 * W_MIX)), full((1, 3 * W_MIX)), full((1, LANES)), full((1, LANES)),
                  full((1, W_MIX))],
        out_specs=pl.BlockSpec((None, tl, W_MIX), lambda b, c: (b, c, 0)),
        scratch_shapes=[pltpu.VMEM((TAIL, 3 * W_MIX), F32), pltpu.VMEM((N_PAIRS, LANES, LANES), F32)],
        compiler_params=pltpu.CompilerParams(
            dimension_semantics=("parallel", "arbitrary"), vmem_limit_bytes=VMEM_LIMIT),
        name="gdn",
    )(proj3, proj3, proj3, conv_w, conv_b.reshape(1, -1),
      _pad_lanes(dt_bias, SM_DECAY), _pad_lanes(a_log, SM_DECAY), jnp.tile(norm_w.astype(F32), N_HEADS).reshape(1, -1))


def _outproj_kernel(x_ref, ya_ref, yb_ref, yd_ref, dil_ref, lse_ref, cg_ref, w_ref, lg_ref, lb_ref, o_ref):
    lse = [lse_ref[g] for g in range(DIL_GROUPS)]
    m = jnp.maximum(jnp.maximum(lse[0], lse[1]), lse[2])
    e = [jnp.exp(l - m) for l in lse]
    yc = (e[0] * dil_ref[0] + e[1] * dil_ref[1] + e[2] * dil_ref[2]) / (e[0] + e[1] + e[2])
    yc = yc * _silu(cg_ref[...])
    mix = (_bdot(ya_ref[...], w_ref[0]) + _bdot(yb_ref[...], w_ref[1])
           + _bdot(yc, w_ref[2]) + _bdot(yd_ref[...], w_ref[3]))
    r = DEEPNORM_ALPHA * x_ref[...] + mix
    mu = jnp.mean(r, axis=1, keepdims=True)
    var = jnp.mean(jnp.square(r - mu), axis=1, keepdims=True)
    o_ref[...] = (r - mu) * lax.rsqrt(var + LN_EPS) * lg_ref[...] + lb_ref[...]


def _outproj(x2d, ya, yb, yd, dil_o, dil_lse, proj2d, w_out, ln_g, ln_b):
    T = x2d.shape[0]
    tm = min(512, T)
    rowblk = lambda width, cblk=0: pl.BlockSpec((tm, width), lambda i: (i, cblk))
    grp = pl.BlockSpec((DIL_GROUPS, tm, W_MIX), lambda i: (0, i, 0))
    return pl.pallas_call(
        _outproj_kernel,
        out_shape=jax.ShapeDtypeStruct((T, D_MODEL), F32),
        grid=(T // tm,),
        in_specs=[rowblk(D_MODEL), rowblk(W_MIX), rowblk(W_MIX), rowblk(W_MIX), grp, grp,
                  rowblk(W_MIX, (C_DIL + 3 * W_MIX) // W_MIX),
                  pl.BlockSpec((4, W_MIX, D_MODEL), lambda i: (0, 0, 0)),
                  pl.BlockSpec((1, D_MODEL), lambda i: (0, 0)), pl.BlockSpec((1, D_MODEL), lambda i: (0, 0))],
        out_specs=rowblk(D_MODEL),
        compiler_params=pltpu.CompilerParams(
            dimension_semantics=("parallel",), vmem_limit_bytes=VMEM_LIMIT),
        name="outproj",
    )(x2d, ya, yb, yd, dil_o, dil_lse, proj2d, w_out, ln_g.reshape(1, -1), ln_b.reshape(1, -1))


def _alibi_slopes():
    n = 2 * N_HEADS
    s = (2.0 ** (-8.0 * (np.arange(n) + 1) / n)).astype(np.float32)
    return jnp.asarray(s[:N_HEADS]), jnp.asarray(s[N_HEADS:])


def _layer(x, w_in, w_out, ssm_conv_w, ssm_conv_b, ssm_dt_bias, ssm_A_log, ssm_D, ssm_norm_w,
           dn_conv_w, dn_conv_b, dn_dt_bias, dn_A_log, dn_norm_w, ln_g, ln_b):
    B, S, _ = x.shape
    T = B * S
    slopes_dil, slopes_moba = _alibi_slopes()
    x2d = x.reshape(T, D_MODEL)
    proj2d = _inproj(x2d, _repack_w_in(w_in))
    proj3 = proj2d.reshape(B, S, PROJ_COLS)

    ya = _moba(proj3, slopes_moba)
    yb = _ssd(proj3, ssm_conv_w, ssm_conv_b, ssm_dt_bias, ssm_A_log, ssm_D, ssm_norm_w)
    yd = _gdn(proj3, dn_conv_w, dn_conv_b, dn_dt_bias, dn_A_log, dn_norm_w)

    dqkv = proj3[:, :, C_DIL:C_DIL + 3 * W_MIX]
    dil_in = jnp.stack([dqkv, _deinterleave(dqkv, 4), _deinterleave(dqkv, 16)])
    dil_o, dil_lse = _dilated(dil_in, slopes_dil)
    back = lambda a: jnp.stack([a[0], _interleave(a[1], 4), _interleave(a[2], 16)]).reshape(DIL_GROUPS, T, W_MIX)

    out = _outproj(x2d, ya.reshape(T, W_MIX), yb.reshape(T, W_MIX), yd.reshape(T, W_MIX),
                   back(dil_o), back(dil_lse), proj2d,
                   w_out.reshape(4, W_MIX, D_MODEL).astype(BF16), ln_g, ln_b)
    return out.reshape(B, S, D_MODEL)


def kernel(x, w_in, w_out, ssm_conv_w, ssm_conv_b, ssm_dt_bias, ssm_A_log, ssm_D, ssm_norm_w,
           dn_conv_w, dn_conv_b, dn_dt_bias, dn_A_log, dn_norm_w, ln_g, ln_b):
    assert x.shape[1] % 2048 == 0 and x.shape[2] == D_MODEL
    for l in range(DEPTH):
        x = _layer(x, w_in[l], w_out[l], ssm_conv_w[l], ssm_conv_b[l], ssm_dt_bias[l], ssm_A_log[l],
                   ssm_D[l], ssm_norm_w[l], dn_conv_w[l], dn_conv_b[l], dn_dt_bias[l], dn_A_log[l],
                   dn_norm_w[l], ln_g[l], ln_b[l])
    return x
```

```python
import functools
import math

import numpy as np
import jax
import jax.numpy as jnp
from jax import lax
from jax.experimental import pallas as pl
from jax.experimental.pallas import tpu as pltpu

F32 = jnp.float32
BF16 = jnp.bfloat16
HIGHEST = lax.Precision.HIGHEST

D_MODEL = 1024
DEPTH = 2
HEAD_DIM = 64
N_HEADS = 6
W_MIX = N_HEADS * HEAD_DIM
N_PAIRS = W_MIX // 128
D_MIX = 4 * W_MIX
MOBA_BLOCK = 256
MOBA_TOPK = 3
MOBA_NFEAT = 3
MOBA_UNROLL = 4
MOBA_LROWS = 16
SSD_STATE = 128
SSD_GROUPS = 2
SSD_CHUNK = 128
SSD_XBC = W_MIX + 2 * SSD_GROUPS * SSD_STATE
CONV_WIDTH = 4
DIL_DILATIONS = (1, 4, 16)
DIL_GROUPS = len(DIL_DILATIONS)
DIL_BATCH = 4
DIL_MERGE_ROWS = 256
DIL_SPAN = 128
DELTA_CHUNK = 64
LN_EPS = 1e-5
RMS_EPS = 1e-6
DEEPNORM_ALPHA = (2.0 * DEPTH) ** 0.25
ATTN_SCALE = HEAD_DIM ** -0.5
NEG = -1e30

LANES = 128
TAIL = 8
VMEM_LIMIT = 48 * 1024 * 1024

C_MOBA = 0
C_SSD_X = 1536
C_SSD_Z = 1920
C_DN_QKV = 2304
C_DN_G = 3456
C_DIL = 3840
C_SSD_B = 5376
C_SSD_C = 5632
C_SMALL = 5888
PROJ_COLS = 6144
SM_DT, SM_BETA, SM_DECAY = 0, 6, 12


def _repack_w_in(w):
    o = np.cumsum([0, 384, 384, 384, 384, SSD_XBC, 384, 6, 384, 384, 384, 384, 1152, 384, 6, 6])
    a_q, _, _, _, s_xbc, s_z, s_dt, c_q, _, _, _, d_qkv, d_g, d_b, d_a, end = [int(v) for v in o]
    pad = jnp.zeros((w.shape[0], PROJ_COLS - C_SMALL - 18), w.dtype)
    parts = [
        w[:, a_q:s_xbc],
        w[:, s_xbc:s_xbc + 384],
        w[:, s_z:s_dt],
        w[:, d_qkv:d_g],
        w[:, d_g:d_b],
        w[:, c_q:d_qkv],
        w[:, s_xbc + 384:s_z],
        w[:, s_dt:c_q], w[:, d_b:d_a], w[:, d_a:end], pad,
    ]
    return jnp.concatenate(parts, axis=1).astype(BF16)


def _silu(x):
    return x * jax.nn.sigmoid(x)


def _softplus(x):
    return jnp.maximum(x, 0.0) + jnp.log(1.0 + jnp.exp(-jnp.abs(x)))


def _bdot(a, b):
    return jnp.dot(a.astype(BF16), b.astype(BF16), preferred_element_type=F32)


def _bdot_nt(a, b):
    return lax.dot_general(a.astype(BF16), b.astype(BF16), (((1,), (1,)), ((), ())),
                           preferred_element_type=F32)


def _bdot_tn(a, b):
    return lax.dot_general(a.astype(BF16), b.astype(BF16), (((0,), (0,)), ((), ())),
                           preferred_element_type=F32)


def _fdot(a, b):
    return jnp.dot(a, b, precision=HIGHEST, preferred_element_type=F32)


def _causal_conv_silu(x, tail, w, b):
    L = x.shape[0]
    xx = jnp.concatenate([tail, x], axis=0)
    y = b
    for j in range(CONV_WIDTH):
        off = TAIL - (CONV_WIDTH - 1) + j
        y = y + w[j:j + 1, :] * xx[off:off + L, :]
    return _silu(y)


def _inproj_kernel(x_ref, w_ref, o_ref, xb_ref):
    @pl.when(pl.program_id(1) == 0)
    def _():
        xb_ref[...] = x_ref[...].astype(BF16)

    o_ref[...] = jnp.dot(xb_ref[...], w_ref[...], preferred_element_type=F32)


def _inproj(x2d, w):
    T = x2d.shape[0]
    tm = min(1024, T)
    tn = 1536
    return pl.pallas_call(
        _inproj_kernel,
        out_shape=jax.ShapeDtypeStruct((T, PROJ_COLS), F32),
        grid=(T // tm, PROJ_COLS // tn),
        in_specs=[pl.BlockSpec((tm, D_MODEL), lambda i, j: (i, 0)),
                  pl.BlockSpec((D_MODEL, tn), lambda i, j: (0, j))],
        out_specs=pl.BlockSpec((tm, tn), lambda i, j: (i, j)),
        scratch_shapes=[pltpu.VMEM((tm, D_MODEL), BF16)],
        compiler_params=pltpu.CompilerParams(
            dimension_semantics=("parallel", "arbitrary"), vmem_limit_bytes=VMEM_LIMIT),
        name="inproj",
    )(x2d, w)


def _moba_kernel(slopes_ref, q_ref, k_ref, v_ref, g_ref, o_ref, kaug_ref, vt_ref, kmean_ref, *, nblk):
    p = pl.program_id(1)
    i = pl.program_id(2)
    L = MOBA_BLOCK

    NF = MOBA_NFEAT

    @pl.when(i == 0)
    def _():
        lane = lax.broadcasted_iota(jnp.int32, (L, LANES), 1)
        feat_lane = lane % HEAD_DIM
        c_loc = lax.broadcasted_iota(jnp.int32, (L, LANES), 0).astype(F32)
        feat = jnp.where(feat_lane < NF, 1.0, jnp.where(feat_lane < 2 * NF, c_loc, 0.0))
        ones = jnp.ones((MOBA_LROWS, L), F32)
        for n in range(nblk):
            kblk = k_ref[n * L:(n + 1) * L, :]
            kmean_ref[n:n + 1, :] = jnp.mean(kblk, axis=0, keepdims=True)
            kaug_ref[0, n] = jnp.where(lane < HEAD_DIM, kblk, feat).astype(BF16)
            kaug_ref[1, n] = jnp.where(lane < HEAD_DIM, feat, kblk).astype(BF16)
            vt = v_ref[n * L:(n + 1) * L, :].T
            for h in range(2):
                vt_ref[h, n] = jnp.concatenate([ones, vt[h * HEAD_DIM:(h + 1) * HEAD_DIM, :]], axis=0).astype(BF16)

    qt = q_ref[...].T
    qrow = lax.broadcasted_iota(jnp.int32, (LANES, L), 0)
    r_loc = lax.broadcasted_iota(jnp.int32, (1, L), 1).astype(F32)
    brow = lax.broadcasted_iota(jnp.int32, (nblk, L), 0).astype(F32)
    frow = lax.broadcasted_iota(jnp.int32, (2 * TAIL, L), 0)
    kmean = kmean_ref[...]
    i_f = i.astype(F32)

    qparts, sels, slope_rows, zeros48 = [], [], [], jnp.zeros((HEAD_DIM - 2 * TAIL, L), BF16)
    for h in range(2):
        in_head = (qrow >= HEAD_DIM * h) & (qrow < HEAD_DIM * (h + 1))
        gate = jnp.dot(kmean, jnp.where(in_head, qt, 0.0), precision=HIGHEST, preferred_element_type=F32)
        gate = jnp.where(brow < i_f, gate, -jnp.inf)
        sel_h = []
        for kk in range(MOBA_TOPK):
            mx = jnp.max(gate, axis=0, keepdims=True)
            idx = jnp.min(jnp.where(gate == mx, brow, float(nblk)), axis=0, keepdims=True)
            sel_h.append(jnp.where(kk < i, idx, -1.0))
            gate = jnp.where(brow == idx, -jnp.inf, gate)
        sels.append(sel_h)
        qparts.append((qt[h * HEAD_DIM:(h + 1) * HEAD_DIM, :] * ATTN_SCALE).astype(BF16))
        slope_rows.append(jnp.full((1, L), slopes_ref[2 * p + h], F32))

    def q_aug(h, t_row):
        terms = _split3(t_row) + _split3(slope_rows[h])
        feat = jnp.zeros((2 * TAIL, L), F32)
        for f, term in enumerate(terms):
            feat = jnp.where(frow == f, term.astype(F32), feat)
        feat = feat.astype(BF16)
        if h == 0:
            return jnp.concatenate([qparts[0], feat, zeros48], axis=0)
        return jnp.concatenate([feat, zeros48, qparts[1]], axis=0)

    def attend(blocks, t_rows, carry, mask=None):
        nb = len(blocks)
        ss = [[jnp.dot(kaug_ref[h, n], q_aug(h, t_rows[h][j]), preferred_element_type=F32)
               for j, n in enumerate(blocks)] for h in range(2)]
        if mask is not None:
            ss = [[jnp.where(mask, s, NEG) for s in row] for row in ss]
        out = []
        smax = [functools.reduce(jnp.maximum, [jnp.max(s, axis=0, keepdims=True) for s in ss[h]]) for h in range(2)]
        m_new = [smax[h] if carry is None else jnp.maximum(carry[2 * h], smax[h]) for h in range(2)]
        ps = [[jnp.exp(s - m_new[h]).astype(BF16) for s in ss[h]] for h in range(2)]
        pv = [jnp.dot(jnp.concatenate([vt_ref[h, n] for n in blocks], axis=1) if nb > 1 else vt_ref[h, blocks[0]],
                      jnp.concatenate(ps[h], axis=0) if nb > 1 else ps[h][0], preferred_element_type=F32)
              for h in range(2)]
        for h in range(2):
            acc = pv[h] if carry is None else jnp.exp(carry[2 * h] - m_new[h]) * carry[2 * h + 1] + pv[h]
            out += [m_new[h], acc]
        return tuple(out)

    c_idx = lax.broadcasted_iota(jnp.int32, (L, L), 0)
    r_idx = lax.broadcasted_iota(jnp.int32, (L, L), 1)
    init = attend([i], [[-slope_rows[h] * r_loc] for h in range(2)], None, c_idx <= r_idx)

    def body(j, carry):
        blocks, t_rows = [], [[], []]
        for u in range(MOBA_UNROLL):
            n = MOBA_UNROLL * j + u
            n_f = n.astype(F32)
            dist0 = r_loc + L * (i_f - n_f)
            for h in range(2):
                picked = (sels[h][0] == n_f) | (sels[h][1] == n_f) | (sels[h][2] == n_f)
                t_rows[h].append(jnp.where(picked, -slope_rows[h] * dist0, NEG))
            blocks.append(jnp.minimum(n, nblk - 1))
        return attend(blocks, t_rows, carry)

    res = lax.fori_loop(0, (i + MOBA_UNROLL - 1) // MOBA_UNROLL, body, init)
    out_t = jnp.concatenate([res[2 * h + 1][MOBA_LROWS:] / res[2 * h + 1][0:1] for h in range(2)], axis=0)
    o_ref[...] = out_t.T * _silu(g_ref[...])


def _moba(proj3, slopes):
    B, S, _ = proj3.shape
    L = MOBA_BLOCK
    nblk = S // L
    cb = C_MOBA // LANES
    return pl.pallas_call(
        functools.partial(_moba_kernel, nblk=nblk),
        out_shape=jax.ShapeDtypeStruct((B, S, W_MIX), F32),
        grid=(B, N_PAIRS, nblk),
        in_specs=[pl.BlockSpec(memory_space=pltpu.SMEM),
                  pl.BlockSpec((None, L, LANES), lambda b, p, i: (b, i, cb + p)),
                  pl.BlockSpec((None, S, LANES), lambda b, p, i: (b, 0, cb + N_PAIRS + p)),
                  pl.BlockSpec((None, S, LANES), lambda b, p, i: (b, 0, cb + 2 * N_PAIRS + p)),
                  pl.BlockSpec((None, L, LANES), lambda b, p, i: (b, i, cb + 3 * N_PAIRS + p))],
        out_specs=pl.BlockSpec((None, L, LANES), lambda b, p, i: (b, i, p)),
        scratch_shapes=[pltpu.VMEM((2, nblk, L, LANES), BF16),
                        pltpu.VMEM((2, nblk, MOBA_LROWS + HEAD_DIM, L), BF16),
                        pltpu.VMEM((nblk, LANES), F32)],
        compiler_params=pltpu.CompilerParams(
            dimension_semantics=("parallel", "parallel", "arbitrary"), vmem_limit_bytes=VMEM_LIMIT),
        name="moba",
    )(slopes, proj3, proj3, proj3, proj3)


def _dil_kernel(slopes_ref, q_ref, kc_ref, kp_ref, vc_ref, vp_ref, g_ref, o_ref, acc_ref, m_ref, l_ref, *, tq):
    p = pl.program_id(1)
    first = pl.program_id(2) == 0
    SB = DIL_SPAN
    lane = lax.broadcasted_iota(jnp.int32, (1, LANES), 1)
    lo = lane < HEAD_DIM
    row = lax.broadcasted_iota(jnp.int32, (SB, 2 * SB), 0)
    col = lax.broadcasted_iota(jnp.int32, (SB, 2 * SB), 1)
    dist = (row + SB - col).astype(F32)
    band = (col >= row) & (col <= row + SB)
    no_halo = jnp.logical_and(first, col < SB)

    def rows(start, d):
        return pl.ds(start, SB) if d == 1 else pl.ds(start, SB, stride=d)

    for gi, d in enumerate(DIL_DILATIONS):
        span = d * SB
        bias, bias_first = [], []
        for h in range(2):
            slope = slopes_ref[2 * p + h] * float(d)
            bias.append(jnp.where(band, -slope * dist, NEG))
            bias_first.append(jnp.where(no_halo, NEG, bias[h]))
        problems = [(r, j) for j in range(tq // span) for r in range(d)]
        for c0 in range(0, len(problems), DIL_BATCH):
            chunk = problems[c0:c0 + DIL_BATCH]
            kv, qs = [], []
            for r, j in chunk:
                cur = rows(r + span * j, d)
                if j == 0:
                    prev = rows(tq - span + r, d)
                    kp, vp = kp_ref[prev, :], vp_ref[prev, :]
                else:
                    prev = rows(r + span * (j - 1), d)
                    kp, vp = kc_ref[prev, :], vc_ref[prev, :]
                kv.append((jnp.concatenate([kp, kc_ref[cur, :]], axis=0).astype(BF16),
                           jnp.concatenate([vp, vc_ref[cur, :]], axis=0).astype(BF16)))
                qs.append(q_ref[cur, :] * ATTN_SCALE)
            work = [(c, h) for c in range(len(chunk)) for h in range(2)]
            ss = [_bdot_nt(jnp.where(lo if h == 0 else jnp.logical_not(lo), qs[c], 0.0), kv[c][0])
                  + (bias_first[h] if chunk[c][1] == 0 else bias[h]) for c, h in work]
            ms = [jnp.max(s, axis=1, keepdims=True) for s in ss]
            es = [jnp.exp(s - m) for s, m in zip(ss, ms)]
            dens = [jnp.sum(e, axis=1, keepdims=True) for e in es]
            pvs = [jnp.dot(e.astype(BF16), kv[c][1], preferred_element_type=F32) for e, (c, h) in zip(es, work)]
            for c, (r, j) in enumerate(chunk):
                cur = rows(r + span * j, d)
                acc_ref[gi, cur, :] = jnp.where(lo, pvs[2 * c], pvs[2 * c + 1])
                m_ref[gi, cur, :] = jnp.where(lo, ms[2 * c], ms[2 * c + 1])
                l_ref[gi, cur, :] = jnp.where(lo, dens[2 * c], dens[2 * c + 1])

    for c0 in range(0, tq, DIL_MERGE_ROWS):
        rs = slice(c0, c0 + DIL_MERGE_ROWS)
        ms = [m_ref[gi, rs, :] for gi in range(DIL_GROUPS)]
        top = functools.reduce(jnp.maximum, ms)
        ws = [jnp.exp(m - top) for m in ms]
        num = sum(w * acc_ref[gi, rs, :] for gi, w in enumerate(ws))
        den = sum(w * l_ref[gi, rs, :] for gi, w in enumerate(ws))
        o_ref[rs, :] = num / den * _silu(g_ref[rs, :])


def _dilated(proj3, slopes):
    B, S, _ = proj3.shape
    tq = DIL_DILATIONS[-1] * DIL_SPAN
    base = C_DIL // LANES
    cur = lambda off: pl.BlockSpec((None, tq, LANES), lambda b, p, t: (b, t, base + off + p))
    prev = lambda off: pl.BlockSpec((None, tq, LANES), lambda b, p, t: (b, jnp.maximum(t - 1, 0), base + off + p))
    scratch = pltpu.VMEM((DIL_GROUPS, tq, LANES), F32)
    return pl.pallas_call(
        functools.partial(_dil_kernel, tq=tq),
        out_shape=jax.ShapeDtypeStruct((B, S, W_MIX), F32),
        grid=(B, N_PAIRS, S // tq),
        in_specs=[pl.BlockSpec(memory_space=pltpu.SMEM),
                  cur(0), cur(N_PAIRS), prev(N_PAIRS), cur(2 * N_PAIRS), prev(2 * N_PAIRS), cur(3 * N_PAIRS)],
        out_specs=pl.BlockSpec((None, tq, LANES), lambda b, p, t: (b, t, p)),
        scratch_shapes=[scratch, scratch, scratch],
        compiler_params=pltpu.CompilerParams(
            dimension_semantics=("parallel", "parallel", "arbitrary"), vmem_limit_bytes=VMEM_LIMIT),
        name="dilated",
    )(slopes, proj3, proj3, proj3, proj3, proj3, proj3)


def _ssd_kernel(x_ref, z_ref, b_ref, c_ref, sm_ref, cw_ref, cb_ref, dtb_ref, alog_ref, dskip_ref, nw_ref,
                o_ref, tx_ref, tb_ref, tc_ref, st_ref):
    L = SSD_CHUNK
    N = SSD_STATE

    @pl.when(pl.program_id(1) == 0)
    def _():
        tx_ref[...] = jnp.zeros_like(tx_ref)
        tb_ref[...] = jnp.zeros_like(tb_ref)
        tc_ref[...] = jnp.zeros_like(tc_ref)
        st_ref[...] = jnp.zeros_like(st_ref)

    cw = cw_ref[...]
    cb = cb_ref[...]
    xr, br, cr = x_ref[...], b_ref[...], c_ref[...]
    xs = _causal_conv_silu(xr, tx_ref[...], cw[:, :W_MIX], cb[:, :W_MIX])
    Bm = _causal_conv_silu(br, tb_ref[...], cw[:, W_MIX:W_MIX + 2 * N], cb[:, W_MIX:W_MIX + 2 * N])
    Cm = _causal_conv_silu(cr, tc_ref[...], cw[:, W_MIX + 2 * N:], cb[:, W_MIX + 2 * N:])
    tx_ref[...] = xr[L - TAIL:, :]
    tb_ref[...] = br[L - TAIL:, :]
    tc_ref[...] = cr[L - TAIL:, :]

    dt = _softplus(sm_ref[...] + dtb_ref[...])
    a = dt * (-jnp.exp(alog_ref[...]))
    row = lax.broadcasted_iota(jnp.int32, (L, L), 0)
    col = lax.broadcasted_iota(jnp.int32, (L, L), 1)
    causal = col <= row
    a_cum = _fdot(causal.astype(F32), a)
    e_row = lax.broadcasted_iota(jnp.int32, (LANES, W_MIX), 0)
    e_col = lax.broadcasted_iota(jnp.int32, (LANES, W_MIX), 1)
    expand = (e_col // HEAD_DIM == e_row).astype(F32)
    dt_w = _fdot(dt, expand)
    acum_w = _fdot(a_cum, expand)
    atot_w = acum_w[L - 1:L, :]
    xdt = xs * dt_w
    a_cum_t = a_cum.T

    lane = lax.broadcasted_iota(jnp.int32, (1, LANES), 1)
    lane_w = lax.broadcasted_iota(jnp.int32, (1, W_MIX), 1)
    scores = [_bdot_nt(Cm[:, g * N:(g + 1) * N], Bm[:, g * N:(g + 1) * N]) for g in range(SSD_GROUPS)]
    y_pairs = []
    for p in range(N_PAIRS):
        res = []
        for hh in range(2):
            h = 2 * p + hh
            diff = a_cum[:, h:h + 1] - a_cum_t[h:h + 1, :]
            lmat = jnp.exp(jnp.where(causal, diff, NEG))
            res.append(_bdot(scores[h // (N_HEADS // SSD_GROUPS)] * lmat, xdt[:, p * LANES:(p + 1) * LANES]))
        y_pairs.append(jnp.where(lane < HEAD_DIM, res[0], res[1]))
    y_diag = jnp.concatenate(y_pairs, axis=1)

    state = st_ref[...]
    in_g0 = lane_w < W_MIX // SSD_GROUPS
    y_off = jnp.where(in_g0, _bdot(Cm[:, :N], state), _bdot(Cm[:, N:], state)) * jnp.exp(acum_w)
    xdec = xdt * jnp.exp(atot_w - acum_w)
    contrib = jnp.where(in_g0, _bdot_tn(Bm[:, :N], xdec), _bdot_tn(Bm[:, N:], xdec))
    st_ref[...] = state * jnp.exp(atot_w) + contrib

    y = y_diag + y_off + dskip_ref[...] * xs
    y = y * _silu(z_ref[...])
    y = y * lax.rsqrt(jnp.mean(y * y, axis=1, keepdims=True) + RMS_EPS) * nw_ref[...]
    o_ref[...] = y


def _pad_lanes(v, offset):
    return jnp.zeros((1, LANES), F32).at[0, offset:offset + v.shape[0]].set(v.astype(F32))


def _ssd(proj3, conv_w, conv_b, dt_bias, a_log, d_skip, norm_w):
    B, S, _ = proj3.shape
    L = SSD_CHUNK
    N2 = 2 * SSD_STATE
    full = lambda shape: pl.BlockSpec(shape, lambda b, c: (0, 0))
    return pl.pallas_call(
        _ssd_kernel,
        out_shape=jax.ShapeDtypeStruct((B, S, W_MIX), F32),
        grid=(B, S // L),
        in_specs=[pl.BlockSpec((None, L, W_MIX), lambda b, c: (b, c, C_SSD_X // W_MIX)),
                  pl.BlockSpec((None, L, W_MIX), lambda b, c: (b, c, C_SSD_Z // W_MIX)),
                  pl.BlockSpec((None, L, N2), lambda b, c: (b, c, C_SSD_B // N2)),
                  pl.BlockSpec((None, L, N2), lambda b, c: (b, c, C_SSD_C // N2)),
                  pl.BlockSpec((None, L, LANES), lambda b, c: (b, c, C_SMALL // LANES)),
                  full((CONV_WIDTH, SSD_XBC)), full((1, SSD_XBC)), full((1, LANES)), full((1, LANES)),
                  full((1, W_MIX)), full((1, W_MIX))],
        out_specs=pl.BlockSpec((None, L, W_MIX), lambda b, c: (b, c, 0)),
        scratch_shapes=[pltpu.VMEM((TAIL, W_MIX), F32), pltpu.VMEM((TAIL, N2), F32),
                        pltpu.VMEM((TAIL, N2), F32), pltpu.VMEM((SSD_STATE, W_MIX), F32)],
        compiler_params=pltpu.CompilerParams(
            dimension_semantics=("parallel", "arbitrary"), vmem_limit_bytes=VMEM_LIMIT),
        name="ssd",
    )(proj3, proj3, proj3, proj3, proj3, conv_w, conv_b.reshape(1, -1),
      _pad_lanes(dt_bias, SM_DT), _pad_lanes(a_log, SM_DT),
      jnp.repeat(d_skip.astype(F32), HEAD_DIM).reshape(1, -1), norm_w.reshape(1, -1))


INV_LEAF = 16


def _pair_blockdiag(x):
    lo = lax.broadcasted_iota(jnp.int32, (1, LANES), 1) < HEAD_DIM
    xb = x.astype(BF16)
    zero = jnp.zeros_like(xb)
    return jnp.concatenate([jnp.where(lo, xb, zero), jnp.where(lo, zero, xb)], axis=0)


def _pair_mm(a, b):
    return jnp.dot(a.astype(BF16), _pair_blockdiag(b), preferred_element_type=F32)


def _pair_unit_lower_inverse(lms, eye, on_diag_block):
    mm = lambda xs, ys: [_pair_mm(x, y) for x, y in zip(xs, ys)]
    d = [jnp.where(on_diag_block, lm, 0.0) for lm in lms]
    off = [lm - x for lm, x in zip(lms, d)]
    d2 = mm(d, d)
    d4 = mm(d2, d2)
    d8 = mm(d4, d4)
    left = mm([eye - x for x in d], [eye + x for x in d2])
    right = mm([eye + x for x in d4], [eye + x for x in d8])
    dinv = mm(left, right)
    m = mm(dinv, off)
    m2 = mm(m, m)
    return mm(mm([eye - x for x in m], [eye + x for x in m2]), dinv)


def _split3(x):
    hi = x.astype(BF16)
    r = x - hi.astype(F32)
    mid = r.astype(BF16)
    lo = (r - mid.astype(F32)).astype(BF16)
    return hi, mid, lo


def _xdot(a01, x):
    a = a01.astype(BF16)
    return sum(jnp.dot(a, t, preferred_element_type=F32) for t in _split3(x))


def _xdot_r(x, b01):
    b = b01.astype(BF16)
    return sum(jnp.dot(t, b, preferred_element_type=F32) for t in _split3(x))


def _gdn_kernel(qkv_ref, g_ref, sm_ref, cw_ref, cb_ref, dtb_ref, alog_ref, nw_ref, o_ref, tail_ref, st_ref, *, tl):
    C = DELTA_CHUNK

    @pl.when(pl.program_id(1) == 0)
    def _():
        tail_ref[...] = jnp.zeros_like(tail_ref)
        st_ref[...] = jnp.zeros_like(st_ref)

    raw = qkv_ref[...]
    qkv = _causal_conv_silu(raw, tail_ref[...], cw_ref[...], cb_ref[...])
    tail_ref[...] = raw[tl - TAIL:, :]
    q, k, v = qkv[:, :W_MIX], qkv[:, W_MIX:2 * W_MIX], qkv[:, 2 * W_MIX:]

    w_row = lax.broadcasted_iota(jnp.int32, (W_MIX, W_MIX), 0)
    w_col = lax.broadcasted_iota(jnp.int32, (W_MIX, W_MIX), 1)
    same_head = (w_row // HEAD_DIM == w_col // HEAD_DIM).astype(BF16)
    q = q * lax.rsqrt(_bdot(q * q, same_head) + RMS_EPS) * ATTN_SCALE
    k = k * lax.rsqrt(_bdot(k * k, same_head) + RMS_EPS)

    sm = sm_ref[...]
    beta = jax.nn.sigmoid(sm)
    gdec = -jnp.exp(alog_ref[...]) * _softplus(sm + dtb_ref[...])
    t_row = lax.broadcasted_iota(jnp.int32, (tl, tl), 0)
    t_col = lax.broadcasted_iota(jnp.int32, (tl, tl), 1)
    chunk_tri = (t_col <= t_row) & (t_row // C == t_col // C)
    gc = _xdot(chunk_tri, gdec)
    e_row = lax.broadcasted_iota(jnp.int32, (LANES, W_MIX), 0)
    e_col = lax.broadcasted_iota(jnp.int32, (LANES, W_MIX), 1)
    beta_w = _xdot_r(beta, e_col // HEAD_DIM + SM_BETA == e_row)
    gc_w = _xdot_r(gc, e_col // HEAD_DIM + SM_DECAY == e_row)

    kb = k * beta_w
    vb = v * beta_w
    egc = jnp.exp(gc_w)
    kbe = kb * egc
    qe = q * egc

    row = lax.broadcasted_iota(jnp.int32, (C, LANES), 0)
    col = lax.broadcasted_iota(jnp.int32, (C, LANES), 1) % HEAD_DIM
    eye = (row == col).astype(F32)
    incl = col <= row
    strict = col < row
    on_diag_block = (row // INV_LEAF) == (col // INV_LEAF)
    s_row = lax.broadcasted_iota(jnp.int32, (LANES, LANES), 0)
    s_col = lax.broadcasted_iota(jnp.int32, (LANES, LANES), 1)
    pair_diag = (s_row // HEAD_DIM) == (s_col // HEAD_DIM)

    tiles = [(slice(c * C, (c + 1) * C), slice(p * LANES, (p + 1) * LANES))
             for c in range(tl // C) for p in range(N_PAIRS)]
    grams, decays = [], []
    for rows, cols in tiles:
        gcol = gc_w[rows, cols]
        grow = jnp.sum(gcol * eye, axis=0, keepdims=True)
        decays.append(jnp.exp(jnp.where(incl, gcol - grow, NEG)))
        grams.append(lax.dot_general(jnp.concatenate([kb[rows, cols], q[rows, cols]], axis=0).astype(BF16),
                                     _pair_blockdiag(k[rows, cols]), (((1,), (1,)), ((), ())),
                                     preferred_element_type=F32))
    lms = [jnp.where(strict, g[:C] * d, 0.0) for g, d in zip(grams, decays)]
    attns = [jnp.where(incl, g[C:] * d, 0.0) for g, d in zip(grams, decays)]
    t_invs = _pair_unit_lower_inverse(lms, eye, on_diag_block)
    uws = [jnp.dot(t.astype(BF16),
                   jnp.concatenate([_pair_blockdiag(vb[rows, cols]), _pair_blockdiag(kbe[rows, cols])], axis=1),
                   preferred_element_type=F32) for t, (rows, cols) in zip(t_invs, tiles)]
    pre = []
    for (rows, cols), uw, attn in zip(tiles, uws, attns):
        g_last = gc_w[rows.stop - 1:rows.stop, cols]
        k_dec = k[rows, cols] * jnp.exp(g_last - gc_w[rows, cols])
        pre.append((uw[:, :LANES], uw[:, LANES:], attn, qe[rows, cols], k_dec, jnp.exp(g_last)))

    for c in range(tl // C):
        o_pairs = []
        for p in range(N_PAIRS):
            u2, w2, attn, qe2, k_dec2, last_decay = pre[c * N_PAIRS + p]
            state = st_ref[p]
            sb = state.astype(BF16)
            v_new = u2 - jnp.dot(w2.astype(BF16), sb, preferred_element_type=F32)
            o_pairs.append(jnp.dot(qe2.astype(BF16), sb, preferred_element_type=F32) + _pair_mm(attn, v_new))
            st_ref[p] = state * last_decay + jnp.where(pair_diag, _bdot_tn(k_dec2, v_new), 0.0)
        o_ref[c * C:(c + 1) * C, :] = jnp.concatenate(o_pairs, axis=1)

    o = o_ref[...]
    o = o * lax.rsqrt(_bdot(o * o, same_head) * (1.0 / HEAD_DIM) + RMS_EPS) * nw_ref[...]
    o_ref[...] = o * _silu(g_ref[...])


def _gdn(proj3, conv_w, conv_b, dt_bias, a_log, norm_w):
    B, S, _ = proj3.shape
    tl = 256
    full = lambda shape: pl.BlockSpec(shape, lambda b, c: (0, 0))
    return pl.pallas_call(
        functools.partial(_gdn_kernel, tl=tl),
        out_shape=jax.ShapeDtypeStruct((B, S, W_MIX), F32),
        grid=(B, S // tl),
        in_specs=[pl.BlockSpec((None, tl, 3 * W_MIX), lambda b, c: (b, c, C_DN_QKV // (3 * W_MIX))),
                  pl.BlockSpec((None, tl, W_MIX), lambda b, c: (b, c, C_DN_G // W_MIX)),
                  pl.BlockSpec((None, tl, LANES), lambda b, c: (b, c, C_SMALL // LANES)),
                  full((CONV_WIDTH, 3 * W_MIX)), full((1, 3 * W_MIX)), full((1, LANES)), full((1, LANES)),
                  full((1, W_MIX))],
        out_specs=pl.BlockSpec((None, tl, W_MIX), lambda b, c: (b, c, 0)),
        scratch_shapes=[pltpu.VMEM((TAIL, 3 * W_MIX), F32), pltpu.VMEM((N_PAIRS, LANES, LANES), F32)],
        compiler_params=pltpu.CompilerParams(
            dimension_semantics=("parallel", "arbitrary"), vmem_limit_bytes=VMEM_LIMIT),
        name="gdn",
    )(proj3, proj3, proj3, conv_w, conv_b.reshape(1, -1),
      _pad_lanes(dt_bias, SM_DECAY), _pad_lanes(a_log, SM_DECAY), jnp.tile(norm_w.astype(F32), N_HEADS).reshape(1, -1))


def _outproj_kernel(x_ref, ya_ref, yb_ref, yc_ref, yd_ref, w_ref, lg_ref, lb_ref, o_ref):
    mix = (_bdot(ya_ref[...], w_ref[0]) + _bdot(yb_ref[...], w_ref[1])
           + _bdot(yc_ref[...], w_ref[2]) + _bdot(yd_ref[...], w_ref[3]))
    r = DEEPNORM_ALPHA * x_ref[...] + mix
    mu = jnp.mean(r, axis=1, keepdims=True)
    var = jnp.mean(jnp.square(r - mu), axis=1, keepdims=True)
    o_ref[...] = (r - mu) * lax.rsqrt(var + LN_EPS) * lg_ref[...] + lb_ref[...]


def _outproj(x2d, ya, yb, yc, yd, w_out, ln_g, ln_b):
    T = x2d.shape[0]
    tm = min(512, T)
    rowblk = lambda width: pl.BlockSpec((tm, width), lambda i: (i, 0))
    return pl.pallas_call(
        _outproj_kernel,
        out_shape=jax.ShapeDtypeStruct((T, D_MODEL), F32),
        grid=(T // tm,),
        in_specs=[rowblk(D_MODEL), rowblk(W_MIX), rowblk(W_MIX), rowblk(W_MIX), rowblk(W_MIX),
                  pl.BlockSpec((4, W_MIX, D_MODEL), lambda i: (0, 0, 0)),
                  pl.BlockSpec((1, D_MODEL), lambda i: (0, 0)), pl.BlockSpec((1, D_MODEL), lambda i: (0, 0))],
        out_specs=rowblk(D_MODEL),
        compiler_params=pltpu.CompilerParams(
            dimension_semantics=("parallel",), vmem_limit_bytes=VMEM_LIMIT),
        name="outproj",
    )(x2d, ya, yb, yc, yd, w_out, ln_g.reshape(1, -1), ln_b.reshape(1, -1))


def _alibi_slopes():
    n = 2 * N_HEADS
    s = (2.0 ** (-8.0 * (np.arange(n) + 1) / n)).astype(np.float32)
    return jnp.asarray(s[:N_HEADS]), jnp.asarray(s[N_HEADS:])


def _layer(x, w_in, w_out, ssm_conv_w, ssm_conv_b, ssm_dt_bias, ssm_A_log, ssm_D, ssm_norm_w,
           dn_conv_w, dn_conv_b, dn_dt_bias, dn_A_log, dn_norm_w, ln_g, ln_b):
    B, S, _ = x.shape
    T = B * S
    slopes_dil, slopes_moba = _alibi_slopes()
    x2d = x.reshape(T, D_MODEL)
    proj2d = _inproj(x2d, _repack_w_in(w_in))
    proj3 = proj2d.reshape(B, S, PROJ_COLS)

    ya = _moba(proj3, slopes_moba)
    yb = _ssd(proj3, ssm_conv_w, ssm_conv_b, ssm_dt_bias, ssm_A_log, ssm_D, ssm_norm_w)
    yd = _gdn(proj3, dn_conv_w, dn_conv_b, dn_dt_bias, dn_A_log, dn_norm_w)

    yc = _dilated(proj3, slopes_dil)

    out = _outproj(x2d, ya.reshape(T, W_MIX), yb.reshape(T, W_MIX), yc.reshape(T, W_MIX), yd.reshape(T, W_MIX),
                   w_out.reshape(4, W_MIX, D_MODEL).astype(BF16), ln_g, ln_b)
    return out.reshape(B, S, D_MODEL)


def kernel(x, w_in, w_out, ssm_conv_w, ssm_conv_b, ssm_dt_bias, ssm_A_log, ssm_D, ssm_norm_w,
           dn_conv_w, dn_conv_b, dn_dt_bias, dn_A_log, dn_norm_w, ln_g, ln_b):
    assert x.shape[1] % 2048 == 0 and x.shape[2] == D_MODEL
    for l in range(DEPTH):
        x = _layer(x, w_in[l], w_out[l], ssm_conv_w[l], ssm_conv_b[l], ssm_dt_bias[l], ssm_A_log[l],
                   ssm_D[l], ssm_norm_w[l], dn_conv_w[l], dn_conv_b[l], dn_dt_bias[l], dn_A_log[l],
                   dn_norm_w[l], ln_g[l], ln_b[l])
    return x
```

```python
import functools
import math

import numpy as np
import jax
import jax.numpy as jnp
from jax import lax
from jax.experimental import pallas as pl
from jax.experimental.pallas import tpu as pltpu

F32 = jnp.float32
BF16 = jnp.bfloat16
HIGHEST = lax.Precision.HIGHEST

D_MODEL = 1024
DEPTH = 2
HEAD_DIM = 64
N_HEADS = 6
W_MIX = N_HEADS * HEAD_DIM
N_PAIRS = W_MIX // 128
D_MIX = 4 * W_MIX
MOBA_BLOCK = 256
MOBA_TOPK = 3
MOBA_NFEAT = 3
MOBA_UNROLL = 4
MOBA_LROWS = 16
SSD_STATE = 128
SSD_GROUPS = 2
SSD_CHUNK = 128
SSD_XBC = W_MIX + 2 * SSD_GROUPS * SSD_STATE
CONV_WIDTH = 4
DIL_DILATIONS = (1, 4, 16)
DIL_GROUPS = len(DIL_DILATIONS)
DIL_BATCH = 4
DIL_MERGE_ROWS = 256
DIL_SPAN = 128
DELTA_CHUNK = 64
LN_EPS = 1e-5
RMS_EPS = 1e-6
DEEPNORM_ALPHA = (2.0 * DEPTH) ** 0.25
ATTN_SCALE = HEAD_DIM ** -0.5
NEG = -1e30

LANES = 128
TAIL = 8
VMEM_LIMIT = 48 * 1024 * 1024

C_MOBA = 0
C_SSD_X = 1536
C_SSD_Z = 1920
C_DN_QKV = 2304
C_DN_G = 3456
C_DIL = 3840
C_SSD_B = 5376
C_SSD_C = 5632
C_SMALL_A = 5888
C_SMALL_B = 6016
PROJ_COLS = 6144
SM_DT, SM_BETA, SM_DECAY = 0, 116, 122

_R = [int(v) for v in np.cumsum([0, 384, 384, 384, 384, SSD_XBC, 384, 6, 384, 384, 384, 384, 1152, 384, 6, 6])]
R_SSD_XBC, R_SSD_Z, R_SSD_DT, R_DIL, R_DN_QKV, R_DN_G, R_END = _R[4], _R[5], _R[6], _R[7], _R[11], _R[12], _R[15]
_SEGMENTS = ((C_MOBA, 0, 4 * W_MIX), (C_SSD_X, R_SSD_XBC, W_MIX), (C_SSD_Z, R_SSD_Z, W_MIX),
             (C_DN_QKV, R_DN_QKV, 3 * W_MIX), (C_DN_G, R_DN_G, W_MIX), (C_DIL, R_DIL, 4 * W_MIX),
             (C_SSD_B, R_SSD_XBC + W_MIX, 2 * SSD_GROUPS * SSD_STATE), (C_SMALL_A, R_SSD_DT, LANES),
             (C_SMALL_B, R_END - LANES, LANES))
_BLOCK_SRC = np.zeros(PROJ_COLS // LANES, np.int32)
for _dst, _src, _n in _SEGMENTS:
    _BLOCK_SRC[_dst // LANES:(_dst + _n) // LANES] = _src + LANES * np.arange(_n // LANES)
assert R_SSD_DT + LANES <= R_END


def _repack_kernel(src_ref, w_ref, o_ref, *, layer):
    o_ref[...] = w_ref[:, layer, :].astype(BF16)


def _repack_w_in(w_t, layer):
    _, depth, dm = w_t.shape
    return pl.pallas_call(
        functools.partial(_repack_kernel, layer=layer),
        out_shape=jax.ShapeDtypeStruct((PROJ_COLS, dm), BF16),
        grid_spec=pltpu.PrefetchScalarGridSpec(
            num_scalar_prefetch=1, grid=(PROJ_COLS // LANES,),
            in_specs=[pl.BlockSpec((pl.Element(LANES), pl.Element(depth), pl.Element(dm)),
                                   lambda b, src: (src[b], 0, 0))],
            out_specs=pl.BlockSpec((LANES, dm), lambda b, src: (b, 0))),
        compiler_params=pltpu.CompilerParams(dimension_semantics=("parallel",), vmem_limit_bytes=VMEM_LIMIT),
        name="repack_w_in",
    )(jnp.asarray(_BLOCK_SRC), w_t)


def _silu(x):
    return x * jax.nn.sigmoid(x)


def _softplus(x):
    return jnp.maximum(x, 0.0) + jnp.log(1.0 + jnp.exp(-jnp.abs(x)))


def _bdot(a, b):
    return jnp.dot(a.astype(BF16), b.astype(BF16), preferred_element_type=F32)


def _bdot_nt(a, b):
    return lax.dot_general(a.astype(BF16), b.astype(BF16), (((1,), (1,)), ((), ())),
                           preferred_element_type=F32)


def _bdot_tn(a, b):
    return lax.dot_general(a.astype(BF16), b.astype(BF16), (((0,), (0,)), ((), ())),
                           preferred_element_type=F32)


def _fdot(a, b):
    return jnp.dot(a, b, precision=HIGHEST, preferred_element_type=F32)


def _causal_conv_silu(x, tail, w, b):
    L = x.shape[0]
    xx = jnp.concatenate([tail, x], axis=0)
    y = b
    for j in range(CONV_WIDTH):
        off = TAIL - (CONV_WIDTH - 1) + j
        y = y + w[j:j + 1, :] * xx[off:off + L, :]
    return _silu(y)


def _inproj_kernel(x_ref, w_ref, o_ref, xb_ref):
    @pl.when(pl.program_id(1) == 0)
    def _():
        xb_ref[...] = x_ref[...].astype(BF16)

    o_ref[...] = lax.dot_general(xb_ref[...], w_ref[...], (((1,), (1,)), ((), ())), preferred_element_type=F32)


def _inproj(x2d, w_t):
    T = x2d.shape[0]
    tm = min(1024, T)
    tn = 1536
    return pl.pallas_call(
        _inproj_kernel,
        out_shape=jax.ShapeDtypeStruct((T, PROJ_COLS), F32),
        grid=(T // tm, PROJ_COLS // tn),
        in_specs=[pl.BlockSpec((tm, D_MODEL), lambda i, j: (i, 0)),
                  pl.BlockSpec((tn, D_MODEL), lambda i, j: (j, 0))],
        out_specs=pl.BlockSpec((tm, tn), lambda i, j: (i, j)),
        scratch_shapes=[pltpu.VMEM((tm, D_MODEL), BF16)],
        compiler_params=pltpu.CompilerParams(
            dimension_semantics=("parallel", "arbitrary"), vmem_limit_bytes=VMEM_LIMIT),
        name="inproj",
    )(x2d, w_t)


def _moba_kernel(slopes_ref, q_ref, k_ref, v_ref, g_ref, o_ref, kaug_ref, vt_ref, kmean_ref, *, nblk):
    p = pl.program_id(1)
    i = pl.program_id(2)
    L = MOBA_BLOCK

    NF = MOBA_NFEAT

    @pl.when(i == 0)
    def _():
        lane = lax.broadcasted_iota(jnp.int32, (L, LANES), 1)
        feat_lane = lane % HEAD_DIM
        c_loc = lax.broadcasted_iota(jnp.int32, (L, LANES), 0).astype(F32)
        feat = jnp.where(feat_lane < NF, 1.0, jnp.where(feat_lane < 2 * NF, c_loc, 0.0))
        ones = jnp.ones((MOBA_LROWS, L), F32)
        for n in range(nblk):
            kblk = k_ref[n * L:(n + 1) * L, :]
            kmean_ref[n:n + 1, :] = jnp.mean(kblk, axis=0, keepdims=True)
            kaug_ref[0, n] = jnp.where(lane < HEAD_DIM, kblk, feat).astype(BF16)
            kaug_ref[1, n] = jnp.where(lane < HEAD_DIM, feat, kblk).astype(BF16)
            vt = v_ref[n * L:(n + 1) * L, :].T
            for h in range(2):
                vt_ref[h, n] = jnp.concatenate([ones, vt[h * HEAD_DIM:(h + 1) * HEAD_DIM, :]], axis=0).astype(BF16)

    qt = q_ref[...].T
    qrow = lax.broadcasted_iota(jnp.int32, (LANES, L), 0)
    r_loc = lax.broadcasted_iota(jnp.int32, (1, L), 1).astype(F32)
    brow = lax.broadcasted_iota(jnp.int32, (nblk, L), 0).astype(F32)
    frow = lax.broadcasted_iota(jnp.int32, (2 * TAIL, L), 0)
    kmean = kmean_ref[...]
    i_f = i.astype(F32)

    qparts, sels, slope_rows, zeros48 = [], [], [], jnp.zeros((HEAD_DIM - 2 * TAIL, L), BF16)
    for h in range(2):
        in_head = (qrow >= HEAD_DIM * h) & (qrow < HEAD_DIM * (h + 1))
        gate = jnp.dot(kmean, jnp.where(in_head, qt, 0.0), precision=HIGHEST, preferred_element_type=F32)
        gate = jnp.where(brow < i_f, gate, -jnp.inf)
        sel_h = []
        for kk in range(MOBA_TOPK):
            mx = jnp.max(gate, axis=0, keepdims=True)
            idx = jnp.min(jnp.where(gate == mx, brow, float(nblk)), axis=0, keepdims=True)
            sel_h.append(jnp.where(kk < i, idx, -1.0))
            gate = jnp.where(brow == idx, -jnp.inf, gate)
        sels.append(sel_h)
        qparts.append((qt[h * HEAD_DIM:(h + 1) * HEAD_DIM, :] * ATTN_SCALE).astype(BF16))
        slope_rows.append(jnp.full((1, L), slopes_ref[2 * p + h], F32))

    def q_aug(h, t_row):
        terms = _split3(t_row) + _split3(slope_rows[h])
        feat = jnp.zeros((2 * TAIL, L), F32)
        for f, term in enumerate(terms):
            feat = jnp.where(frow == f, term.astype(F32), feat)
        feat = feat.astype(BF16)
        if h == 0:
            return jnp.concatenate([qparts[0], feat, zeros48], axis=0)
        return jnp.concatenate([feat, zeros48, qparts[1]], axis=0)

    def attend(blocks, t_rows, carry, mask=None):
        nb = len(blocks)
        ss = [[jnp.dot(kaug_ref[h, n], q_aug(h, t_rows[h][j]), preferred_element_type=F32)
               for j, n in enumerate(blocks)] for h in range(2)]
        if mask is not None:
            ss = [[jnp.where(mask, s, NEG) for s in row] for row in ss]
        out = []
        smax = [functools.reduce(jnp.maximum, [jnp.max(s, axis=0, keepdims=True) for s in ss[h]]) for h in range(2)]
        m_new = [smax[h] if carry is None else jnp.maximum(carry[2 * h], smax[h]) for h in range(2)]
        ps = [[jnp.exp(s - m_new[h]).astype(BF16) for s in ss[h]] for h in range(2)]
        pv = [jnp.dot(jnp.concatenate([vt_ref[h, n] for n in blocks], axis=1) if nb > 1 else vt_ref[h, blocks[0]],
                      jnp.concatenate(ps[h], axis=0) if nb > 1 else ps[h][0], preferred_element_type=F32)
              for h in range(2)]
        for h in range(2):
            acc = pv[h] if carry is None else jnp.exp(carry[2 * h] - m_new[h]) * carry[2 * h + 1] + pv[h]
            out += [m_new[h], acc]
        return tuple(out)

    c_idx = lax.broadcasted_iota(jnp.int32, (L, L), 0)
    r_idx = lax.broadcasted_iota(jnp.int32, (L, L), 1)
    init = attend([i], [[-slope_rows[h] * r_loc] for h in range(2)], None, c_idx <= r_idx)

    def body(j, carry):
        blocks, t_rows = [], [[], []]
        for u in range(MOBA_UNROLL):
            n = MOBA_UNROLL * j + u
            n_f = n.astype(F32)
            dist0 = r_loc + L * (i_f - n_f)
            for h in range(2):
                picked = (sels[h][0] == n_f) | (sels[h][1] == n_f) | (sels[h][2] == n_f)
                t_rows[h].append(jnp.where(picked, -slope_rows[h] * dist0, NEG))
            blocks.append(jnp.minimum(n, nblk - 1))
        return attend(blocks, t_rows, carry)

    res = lax.fori_loop(0, (i + MOBA_UNROLL - 1) // MOBA_UNROLL, body, init)
    out_t = jnp.concatenate([res[2 * h + 1][MOBA_LROWS:] / res[2 * h + 1][0:1] for h in range(2)], axis=0)
    o_ref[...] = out_t.T * _silu(g_ref[...])


def _moba(proj3, slopes):
    B, S, _ = proj3.shape
    L = MOBA_BLOCK
    nblk = S // L
    cb = C_MOBA // LANES
    return pl.pallas_call(
        functools.partial(_moba_kernel, nblk=nblk),
        out_shape=jax.ShapeDtypeStruct((B, S, W_MIX), F32),
        grid=(B, N_PAIRS, nblk),
        in_specs=[pl.BlockSpec(memory_space=pltpu.SMEM),
                  pl.BlockSpec((None, L, LANES), lambda b, p, i: (b, i, cb + p)),
                  pl.BlockSpec((None, S, LANES), lambda b, p, i: (b, 0, cb + N_PAIRS + p)),
                  pl.BlockSpec((None, S, LANES), lambda b, p, i: (b, 0, cb + 2 * N_PAIRS + p)),
                  pl.BlockSpec((None, L, LANES), lambda b, p, i: (b, i, cb + 3 * N_PAIRS + p))],
        out_specs=pl.BlockSpec((None, L, LANES), lambda b, p, i: (b, i, p)),
        scratch_shapes=[pltpu.VMEM((2, nblk, L, LANES), BF16),
                        pltpu.VMEM((2, nblk, MOBA_LROWS + HEAD_DIM, L), BF16),
                        pltpu.VMEM((nblk, LANES), F32)],
        compiler_params=pltpu.CompilerParams(
            dimension_semantics=("parallel", "parallel", "arbitrary"), vmem_limit_bytes=VMEM_LIMIT),
        name="moba",
    )(slopes, proj3, proj3, proj3, proj3)


def _dil_kernel(slopes_ref, q_ref, kc_ref, kp_ref, vc_ref, vp_ref, g_ref, o_ref, acc_ref, m_ref, l_ref, *, tq):
    p = pl.program_id(1)
    first = pl.program_id(2) == 0
    SB = DIL_SPAN
    lane = lax.broadcasted_iota(jnp.int32, (1, LANES), 1)
    lo = lane < HEAD_DIM
    row = lax.broadcasted_iota(jnp.int32, (SB, 2 * SB), 0)
    col = lax.broadcasted_iota(jnp.int32, (SB, 2 * SB), 1)
    dist = (row + SB - col).astype(F32)
    band = (col >= row) & (col <= row + SB)
    no_halo = jnp.logical_and(first, col < SB)

    def rows(start, d):
        return pl.ds(start, SB) if d == 1 else pl.ds(start, SB, stride=d)

    for gi, d in enumerate(DIL_DILATIONS):
        span = d * SB
        bias, bias_first = [], []
        for h in range(2):
            slope = slopes_ref[2 * p + h] * float(d)
            bias.append(jnp.where(band, -slope * dist, NEG))
            bias_first.append(jnp.where(no_halo, NEG, bias[h]))
        problems = [(r, j) for j in range(tq // span) for r in range(d)]
        for c0 in range(0, len(problems), DIL_BATCH):
            chunk = problems[c0:c0 + DIL_BATCH]
            kv, qs = [], []
            for r, j in chunk:
                cur = rows(r + span * j, d)
                if j == 0:
                    prev = rows(tq - span + r, d)
                    kp, vp = kp_ref[prev, :], vp_ref[prev, :]
                else:
                    prev = rows(r + span * (j - 1), d)
                    kp, vp = kc_ref[prev, :], vc_ref[prev, :]
                kv.append((jnp.concatenate([kp, kc_ref[cur, :]], axis=0).astype(BF16),
                           jnp.concatenate([vp, vc_ref[cur, :]], axis=0).astype(BF16)))
                qs.append(q_ref[cur, :] * ATTN_SCALE)
            work = [(c, h) for c in range(len(chunk)) for h in range(2)]
            ss = [_bdot_nt(jnp.where(lo if h == 0 else jnp.logical_not(lo), qs[c], 0.0), kv[c][0])
                  + (bias_first[h] if chunk[c][1] == 0 else bias[h]) for c, h in work]
            ms = [jnp.max(s, axis=1, keepdims=True) for s in ss]
            es = [jnp.exp(s - m) for s, m in zip(ss, ms)]
            dens = [jnp.sum(e, axis=1, keepdims=True) for e in es]
            pvs = [jnp.dot(e.astype(BF16), kv[c][1], preferred_element_type=F32) for e, (c, h) in zip(es, work)]
            for c, (r, j) in enumerate(chunk):
                cur = rows(r + span * j, d)
                acc_ref[gi, cur, :] = jnp.where(lo, pvs[2 * c], pvs[2 * c + 1])
                m_ref[gi, cur, :] = jnp.where(lo, ms[2 * c], ms[2 * c + 1])
                l_ref[gi, cur, :] = jnp.where(lo, dens[2 * c], dens[2 * c + 1])

    for c0 in range(0, tq, DIL_MERGE_ROWS):
        rs = slice(c0, c0 + DIL_MERGE_ROWS)
        ms = [m_ref[gi, rs, :] for gi in range(DIL_GROUPS)]
        top = functools.reduce(jnp.maximum, ms)
        ws = [jnp.exp(m - top) for m in ms]
        num = sum(w * acc_ref[gi, rs, :] for gi, w in enumerate(ws))
        den = sum(w * l_ref[gi, rs, :] for gi, w in enumerate(ws))
        o_ref[rs, :] = num / den * _silu(g_ref[rs, :])


def _dilated(proj3, slopes):
    B, S, _ = proj3.shape
    tq = DIL_DILATIONS[-1] * DIL_SPAN
    base = C_DIL // LANES
    cur = lambda off: pl.BlockSpec((None, tq, LANES), lambda b, p, t: (b, t, base + off + p))
    prev = lambda off: pl.BlockSpec((None, tq, LANES), lambda b, p, t: (b, jnp.maximum(t - 1, 0), base + off + p))
    scratch = pltpu.VMEM((DIL_GROUPS, tq, LANES), F32)
    return pl.pallas_call(
        functools.partial(_dil_kernel, tq=tq),
        out_shape=jax.ShapeDtypeStruct((B, S, W_MIX), F32),
        grid=(B, N_PAIRS, S // tq),
        in_specs=[pl.BlockSpec(memory_space=pltpu.SMEM),
                  cur(0), cur(N_PAIRS), prev(N_PAIRS), cur(2 * N_PAIRS), prev(2 * N_PAIRS), cur(3 * N_PAIRS)],
        out_specs=pl.BlockSpec((None, tq, LANES), lambda b, p, t: (b, t, p)),
        scratch_shapes=[scratch, scratch, scratch],
        compiler_params=pltpu.CompilerParams(
            dimension_semantics=("parallel", "parallel", "arbitrary"), vmem_limit_bytes=VMEM_LIMIT),
        name="dilated",
    )(slopes, proj3, proj3, proj3, proj3, proj3, proj3)


def _ssd_kernel(x_ref, z_ref, b_ref, c_ref, sm_ref, cw_ref, cb_ref, dtb_ref, alog_ref, dskip_ref, nw_ref,
                o_ref, tx_ref, tb_ref, tc_ref, st_ref):
    L = SSD_CHUNK
    N = SSD_STATE

    @pl.when(pl.program_id(1) == 0)
    def _():
        tx_ref[...] = jnp.zeros_like(tx_ref)
        tb_ref[...] = jnp.zeros_like(tb_ref)
        tc_ref[...] = jnp.zeros_like(tc_ref)
        st_ref[...] = jnp.zeros_like(st_ref)

    cw = cw_ref[...]
    cb = cb_ref[...]
    xr, br, cr = x_ref[...], b_ref[...], c_ref[...]
    xs = _causal_conv_silu(xr, tx_ref[...], cw[:, :W_MIX], cb[:, :W_MIX])
    Bm = _causal_conv_silu(br, tb_ref[...], cw[:, W_MIX:W_MIX + 2 * N], cb[:, W_MIX:W_MIX + 2 * N])
    Cm = _causal_conv_silu(cr, tc_ref[...], cw[:, W_MIX + 2 * N:], cb[:, W_MIX + 2 * N:])
    tx_ref[...] = xr[L - TAIL:, :]
    tb_ref[...] = br[L - TAIL:, :]
    tc_ref[...] = cr[L - TAIL:, :]

    dt = _softplus(sm_ref[...] + dtb_ref[...])
    a = dt * (-jnp.exp(alog_ref[...]))
    row = lax.broadcasted_iota(jnp.int32, (L, L), 0)
    col = lax.broadcasted_iota(jnp.int32, (L, L), 1)
    causal = col <= row
    a_cum = _fdot(causal.astype(F32), a)
    e_row = lax.broadcasted_iota(jnp.int32, (LANES, W_MIX), 0)
    e_col = lax.broadcasted_iota(jnp.int32, (LANES, W_MIX), 1)
    expand = (e_col // HEAD_DIM == e_row).astype(F32)
    dt_w = _fdot(dt, expand)
    acum_w = _fdot(a_cum, expand)
    atot_w = acum_w[L - 1:L, :]
    xdt = xs * dt_w
    a_cum_t = a_cum.T

    lane = lax.broadcasted_iota(jnp.int32, (1, LANES), 1)
    lane_w = lax.broadcasted_iota(jnp.int32, (1, W_MIX), 1)
    scores = [_bdot_nt(Cm[:, g * N:(g + 1) * N], Bm[:, g * N:(g + 1) * N]) for g in range(SSD_GROUPS)]
    y_pairs = []
    for p in range(N_PAIRS):
        res = []
        for hh in range(2):
            h = 2 * p + hh
            diff = a_cum[:, h:h + 1] - a_cum_t[h:h + 1, :]
            lmat = jnp.exp(jnp.where(causal, diff, NEG))
            res.append(_bdot(scores[h // (N_HEADS // SSD_GROUPS)] * lmat, xdt[:, p * LANES:(p + 1) * LANES]))
        y_pairs.append(jnp.where(lane < HEAD_DIM, res[0], res[1]))
    y_diag = jnp.concatenate(y_pairs, axis=1)

    state = st_ref[...]
    in_g0 = lane_w < W_MIX // SSD_GROUPS
    y_off = jnp.where(in_g0, _bdot(Cm[:, :N], state), _bdot(Cm[:, N:], state)) * jnp.exp(acum_w)
    xdec = xdt * jnp.exp(atot_w - acum_w)
    contrib = jnp.where(in_g0, _bdot_tn(Bm[:, :N], xdec), _bdot_tn(Bm[:, N:], xdec))
    st_ref[...] = state * jnp.exp(atot_w) + contrib

    y = y_diag + y_off + dskip_ref[...] * xs
    y = y * _silu(z_ref[...])
    y = y * lax.rsqrt(jnp.mean(y * y, axis=1, keepdims=True) + RMS_EPS) * nw_ref[...]
    o_ref[...] = y


def _pad_lanes(v, offset):
    return jnp.zeros((1, LANES), F32).at[0, offset:offset + v.shape[0]].set(v.astype(F32))


def _ssd(proj3, conv_w, conv_b, dt_bias, a_log, d_skip, norm_w):
    B, S, _ = proj3.shape
    L = SSD_CHUNK
    N2 = 2 * SSD_STATE
    full = lambda shape: pl.BlockSpec(shape, lambda b, c: (0, 0))
    return pl.pallas_call(
        _ssd_kernel,
        out_shape=jax.ShapeDtypeStruct((B, S, W_MIX), F32),
        grid=(B, S // L),
        in_specs=[pl.BlockSpec((None, L, W_MIX), lambda b, c: (b, c, C_SSD_X // W_MIX)),
                  pl.BlockSpec((None, L, W_MIX), lambda b, c: (b, c, C_SSD_Z // W_MIX)),
                  pl.BlockSpec((None, L, N2), lambda b, c: (b, c, C_SSD_B // N2)),
                  pl.BlockSpec((None, L, N2), lambda b, c: (b, c, C_SSD_C // N2)),
                  pl.BlockSpec((None, L, LANES), lambda b, c: (b, c, C_SMALL_A // LANES)),
                  full((CONV_WIDTH, SSD_XBC)), full((1, SSD_XBC)), full((1, LANES)), full((1, LANES)),
                  full((1, W_MIX)), full((1, W_MIX))],
        out_specs=pl.BlockSpec((None, L, W_MIX), lambda b, c: (b, c, 0)),
        scratch_shapes=[pltpu.VMEM((TAIL, W_MIX), F32), pltpu.VMEM((TAIL, N2), F32),
                        pltpu.VMEM((TAIL, N2), F32), pltpu.VMEM((SSD_STATE, W_MIX), F32)],
        compiler_params=pltpu.CompilerParams(
            dimension_semantics=("parallel", "arbitrary"), vmem_limit_bytes=VMEM_LIMIT),
        name="ssd",
    )(proj3, proj3, proj3, proj3, proj3, conv_w, conv_b.reshape(1, -1),
      _pad_lanes(dt_bias, SM_DT), _pad_lanes(a_log, SM_DT),
      jnp.repeat(d_skip.astype(F32), HEAD_DIM).reshape(1, -1), norm_w.reshape(1, -1))


INV_LEAF = 16


def _pair_blockdiag(x):
    lo = lax.broadcasted_iota(jnp.int32, (1, LANES), 1) < HEAD_DIM
    xb = x.astype(BF16)
    zero = jnp.zeros_like(xb)
    return jnp.concatenate([jnp.where(lo, xb, zero), jnp.where(lo, zero, xb)], axis=0)


def _pair_mm(a, b):
    return jnp.dot(a.astype(BF16), _pair_blockdiag(b), preferred_element_type=F32)


def _pair_unit_lower_inverse(lms, eye, on_diag_block):
    mm = lambda xs, ys: [_pair_mm(x, y) for x, y in zip(xs, ys)]
    d = [jnp.where(on_diag_block, lm, 0.0) for lm in lms]
    off = [lm - x for lm, x in zip(lms, d)]
    d2 = mm(d, d)
    d4 = mm(d2, d2)
    d8 = mm(d4, d4)
    left = mm([eye - x for x in d], [eye + x for x in d2])
    right = mm([eye + x for x in d4], [eye + x for x in d8])
    dinv = mm(left, right)
    m = mm(dinv, off)
    m2 = mm(m, m)
    return mm(mm([eye - x for x in m], [eye + x for x in m2]), dinv)


def _split3(x):
    hi = x.astype(BF16)
    r = x - hi.astype(F32)
    mid = r.astype(BF16)
    lo = (r - mid.astype(F32)).astype(BF16)
    return hi, mid, lo


def _xdot(a01, x):
    a = a01.astype(BF16)
    return sum(jnp.dot(a, t, preferred_element_type=F32) for t in _split3(x))


def _xdot_r(x, b01):
    b = b01.astype(BF16)
    return sum(jnp.dot(t, b, preferred_element_type=F32) for t in _split3(x))


def _gdn_kernel(qkv_ref, g_ref, sm_ref, cw_ref, cb_ref, dtb_ref, alog_ref, nw_ref, o_ref, tail_ref, st_ref, *, tl):
    C = DELTA_CHUNK

    @pl.when(pl.program_id(1) == 0)
    def _():
        tail_ref[...] = jnp.zeros_like(tail_ref)
        st_ref[...] = jnp.zeros_like(st_ref)

    raw = qkv_ref[...]
    qkv = _causal_conv_silu(raw, tail_ref[...], cw_ref[...], cb_ref[...])
    tail_ref[...] = raw[tl - TAIL:, :]
    q, k, v = qkv[:, :W_MIX], qkv[:, W_MIX:2 * W_MIX], qkv[:, 2 * W_MIX:]

    w_row = lax.broadcasted_iota(jnp.int32, (W_MIX, W_MIX), 0)
    w_col = lax.broadcasted_iota(jnp.int32, (W_MIX, W_MIX), 1)
    same_head = (w_row // HEAD_DIM == w_col // HEAD_DIM).astype(BF16)
    q = q * lax.rsqrt(_bdot(q * q, same_head) + RMS_EPS) * ATTN_SCALE
    k = k * lax.rsqrt(_bdot(k * k, same_head) + RMS_EPS)

    sm = sm_ref[...]
    beta = jax.nn.sigmoid(sm)
    gdec = -jnp.exp(alog_ref[...]) * _softplus(sm + dtb_ref[...])
    t_row = lax.broadcasted_iota(jnp.int32, (tl, tl), 0)
    t_col = lax.broadcasted_iota(jnp.int32, (tl, tl), 1)
    chunk_tri = (t_col <= t_row) & (t_row // C == t_col // C)
    gc = _xdot(chunk_tri, gdec)
    e_row = lax.broadcasted_iota(jnp.int32, (LANES, W_MIX), 0)
    e_col = lax.broadcasted_iota(jnp.int32, (LANES, W_MIX), 1)
    beta_w = _xdot_r(beta, e_col // HEAD_DIM + SM_BETA == e_row)
    gc_w = _xdot_r(gc, e_col // HEAD_DIM + SM_DECAY == e_row)

    kb = k * beta_w
    vb = v * beta_w
    egc = jnp.exp(gc_w)
    kbe = kb * egc
    qe = q * egc

    row = lax.broadcasted_iota(jnp.int32, (C, LANES), 0)
    col = lax.broadcasted_iota(jnp.int32, (C, LANES), 1) % HEAD_DIM
    eye = (row == col).astype(F32)
    incl = col <= row
    strict = col < row
    on_diag_block = (row // INV_LEAF) == (col // INV_LEAF)
    s_row = lax.broadcasted_iota(jnp.int32, (LANES, LANES), 0)
    s_col = lax.broadcasted_iota(jnp.int32, (LANES, LANES), 1)
    pair_diag = (s_row // HEAD_DIM) == (s_col // HEAD_DIM)

    tiles = [(slice(c * C, (c + 1) * C), slice(p * LANES, (p + 1) * LANES))
             for c in range(tl // C) for p in range(N_PAIRS)]
    grams, decays = [], []
    for rows, cols in tiles:
        gcol = gc_w[rows, cols]
        grow = jnp.sum(gcol * eye, axis=0, keepdims=True)
        decays.append(jnp.exp(jnp.where(incl, gcol - grow, NEG)))
        grams.append(lax.dot_general(jnp.concatenate([kb[rows, cols], q[rows, cols]], axis=0).astype(BF16),
                                     _pair_blockdiag(k[rows, cols]), (((1,), (1,)), ((), ())),
                                     preferred_element_type=F32))
    lms = [jnp.where(strict, g[:C] * d, 0.0) for g, d in zip(grams, decays)]
    attns = [jnp.where(incl, g[C:] * d, 0.0) for g, d in zip(grams, decays)]
    t_invs = _pair_unit_lower_inverse(lms, eye, on_diag_block)
    uws = [jnp.dot(t.astype(BF16),
                   jnp.concatenate([_pair_blockdiag(vb[rows, cols]), _pair_blockdiag(kbe[rows, cols])], axis=1),
                   preferred_element_type=F32) for t, (rows, cols) in zip(t_invs, tiles)]
    pre = []
    for (rows, cols), uw, attn in zip(tiles, uws, attns):
        g_last = gc_w[rows.stop - 1:rows.stop, cols]
        k_dec = k[rows, cols] * jnp.exp(g_last - gc_w[rows, cols])
        pre.append((uw[:, :LANES], uw[:, LANES:], attn, qe[rows, cols], k_dec, jnp.exp(g_last)))

    for c in range(tl // C):
        o_pairs = []
        for p in range(N_PAIRS):
            u2, w2, attn, qe2, k_dec2, last_decay = pre[c * N_PAIRS + p]
            state = st_ref[p]
            sb = state.astype(BF16)
            v_new = u2 - jnp.dot(w2.astype(BF16), sb, preferred_element_type=F32)
            o_pairs.append(jnp.dot(qe2.astype(BF16), sb, preferred_element_type=F32) + _pair_mm(attn, v_new))
            st_ref[p] = state * last_decay + jnp.where(pair_diag, _bdot_tn(k_dec2, v_new), 0.0)
        o_ref[c * C:(c + 1) * C, :] = jnp.concatenate(o_pairs, axis=1)

    o = o_ref[...]
    o = o * lax.rsqrt(_bdot(o * o, same_head) * (1.0 / HEAD_DIM) + RMS_EPS) * nw_ref[...]
    o_ref[...] = o * _silu(g_ref[...])


def _gdn(proj3, conv_w, conv_b, dt_bias, a_log, norm_w):
    B, S, _ = proj3.shape
    tl = 256
    full = lambda shape: pl.BlockSpec(shape, lambda b, c: (0, 0))
    return pl.pallas_call(
        functools.partial(_gdn_kernel, tl=tl),
        out_shape=jax.ShapeDtypeStruct((B, S, W_MIX), F32),
        grid=(B, S // tl),
        in_specs=[pl.BlockSpec((None, tl, 3 * W_MIX), lambda b, c: (b, c, C_DN_QKV // (3 * W_MIX))),
                  pl.BlockSpec((None, tl, W_MIX), lambda b, c: (b, c, C_DN_G // W_MIX)),
                  pl.BlockSpec((None, tl, LANES), lambda b, c: (b, c, C_SMALL_B // LANES)),
                  full((CONV_WIDTH, 3 * W_MIX)), full((1, 3 * W_MIX)), full((1, LANES)), full((1, LANES)),
                  full((1, W_MIX))],
        out_specs=pl.BlockSpec((None, tl, W_MIX), lambda b, c: (b, c, 0)),
        scratch_shapes=[pltpu.VMEM((TAIL, 3 * W_MIX), F32), pltpu.VMEM((N_PAIRS, LANES, LANES), F32)],
        compiler_params=pltpu.CompilerParams(
            dimension_semantics=("parallel", "arbitrary"), vmem_limit_bytes=VMEM_LIMIT),
        name="gdn",
    )(proj3, proj3, proj3, conv_w, conv_b.reshape(1, -1),
      _pad_lanes(dt_bias, SM_DECAY), _pad_lanes(a_log, SM_DECAY), jnp.tile(norm_w.astype(F32), N_HEADS).reshape(1, -1))


def _outproj_kernel(x_ref, ya_ref, yb_ref, yc_ref, yd_ref, w_ref, lg_ref, lb_ref, o_ref):
    mix = (_bdot(ya_ref[...], w_ref[0]) + _bdot(yb_ref[...], w_ref[1])
           + _bdot(yc_ref[...], w_ref[2]) + _bdot(yd_ref[...], w_ref[3]))
    r = DEEPNORM_ALPHA * x_ref[...] + mix
    mu = jnp.mean(r, axis=1, keepdims=True)
    var = jnp.mean(jnp.square(r - mu), axis=1, keepdims=True)
    o_ref[...] = (r - mu) * lax.rsqrt(var + LN_EPS) * lg_ref[...] + lb_ref[...]


def _outproj(x2d, ya, yb, yc, yd, w_out, ln_g, ln_b):
    T = x2d.shape[0]
    tm = min(512, T)
    rowblk = lambda width: pl.BlockSpec((tm, width), lambda i: (i, 0))
    return pl.pallas_call(
        _outproj_kernel,
        out_shape=jax.ShapeDtypeStruct((T, D_MODEL), F32),
        grid=(T // tm,),
        in_specs=[rowblk(D_MODEL), rowblk(W_MIX), rowblk(W_MIX), rowblk(W_MIX), rowblk(W_MIX),
                  pl.BlockSpec((4, W_MIX, D_MODEL), lambda i: (0, 0, 0)),
                  pl.BlockSpec((1, D_MODEL), lambda i: (0, 0)), pl.BlockSpec((1, D_MODEL), lambda i: (0, 0))],
        out_specs=rowblk(D_MODEL),
        compiler_params=pltpu.CompilerParams(
            dimension_semantics=("parallel",), vmem_limit_bytes=VMEM_LIMIT),
        name="outproj",
    )(x2d, ya, yb, yc, yd, w_out, ln_g.reshape(1, -1), ln_b.reshape(1, -1))


def _alibi_slopes():
    n = 2 * N_HEADS
    s = (2.0 ** (-8.0 * (np.arange(n) + 1) / n)).astype(np.float32)
    return jnp.asarray(s[:N_HEADS]), jnp.asarray(s[N_HEADS:])


def _layer(x, w_in_t, w_out, ssm_conv_w, ssm_conv_b, ssm_dt_bias, ssm_A_log, ssm_D, ssm_norm_w,
           dn_conv_w, dn_conv_b, dn_dt_bias, dn_A_log, dn_norm_w, ln_g, ln_b):
    B, S, _ = x.shape
    T = B * S
    slopes_dil, slopes_moba = _alibi_slopes()
    x2d = x.reshape(T, D_MODEL)
    proj2d = _inproj(x2d, w_in_t)
    proj3 = proj2d.reshape(B, S, PROJ_COLS)

    ya = _moba(proj3, slopes_moba)
    yb = _ssd(proj3, ssm_conv_w, ssm_conv_b, ssm_dt_bias, ssm_A_log, ssm_D, ssm_norm_w)
    yd = _gdn(proj3, dn_conv_w, dn_conv_b, dn_dt_bias, dn_A_log, dn_norm_w)

    yc = _dilated(proj3, slopes_dil)

    out = _outproj(x2d, ya.reshape(T, W_MIX), yb.reshape(T, W_MIX), yc.reshape(T, W_MIX), yd.reshape(T, W_MIX),
                   w_out.reshape(4, W_MIX, D_MODEL).astype(BF16), ln_g, ln_b)
    return out.reshape(B, S, D_MODEL)


def kernel(x, w_in, w_out, ssm_conv_w, ssm_conv_b, ssm_dt_bias, ssm_A_log, ssm_D, ssm_norm_w,
           dn_conv_w, dn_conv_b, dn_dt_bias, dn_A_log, dn_norm_w, ln_g, ln_b):
    assert x.shape[1] % 2048 == 0 and x.shape[2] == D_MODEL
    w_t = jnp.transpose(w_in, (2, 0, 1))
    for l in range(DEPTH):
        x = _layer(x, _repack_w_in(w_t, l), w_out[l], ssm_conv_w[l], ssm_conv_b[l], ssm_dt_bias[l], ssm_A_log[l],
                   ssm_D[l], ssm_norm_w[l], dn_conv_w[l], dn_conv_b[l], dn_dt_bias[l], dn_A_log[l],
                   dn_norm_w[l], ln_g[l], ln_b[l])
    return x
```

```python
import functools
import math

import numpy as np
import jax
import jax.numpy as jnp
from jax import lax
from jax.experimental import pallas as pl
from jax.experimental.pallas import tpu as pltpu

F32 = jnp.float32
BF16 = jnp.bfloat16
HIGHEST = lax.Precision.HIGHEST

D_MODEL = 1024
DEPTH = 2
HEAD_DIM = 64
N_HEADS = 6
W_MIX = N_HEADS * HEAD_DIM
N_PAIRS = W_MIX // 128
D_MIX = 4 * W_MIX
MOBA_BLOCK = 256
MOBA_TOPK = 3
MOBA_NFEAT = 3
MOBA_UNROLL = 2
MOBA_LROWS = 16
SSD_STATE = 128
SSD_GROUPS = 2
SSD_CHUNK = 128
SSD_XBC = W_MIX + 2 * SSD_GROUPS * SSD_STATE
CONV_WIDTH = 4
DIL_DILATIONS = (1, 4, 16)
DIL_GROUPS = len(DIL_DILATIONS)
DIL_BATCH = 4
DIL_MERGE_ROWS = 256
DIL_SPAN = 128
DELTA_CHUNK = 64
LN_EPS = 1e-5
RMS_EPS = 1e-6
DEEPNORM_ALPHA = (2.0 * DEPTH) ** 0.25
ATTN_SCALE = HEAD_DIM ** -0.5
LOG2E = math.log2(math.e)
NEG = -1e30

LANES = 128
TAIL = 8
VMEM_LIMIT = 48 * 1024 * 1024

C_MOBA = 0
C_SSD_X = 1536
C_SSD_Z = 1920
C_DN_QKV = 2304
C_DN_G = 3456
C_DIL = 3840
C_SSD_B = 5376
C_SSD_C = 5632
C_SMALL_A = 5888
C_SMALL_B = 6016
PROJ_COLS = 6144
SM_DT, SM_BETA, SM_DECAY = 0, 116, 122

_R = [int(v) for v in np.cumsum([0, 384, 384, 384, 384, SSD_XBC, 384, 6, 384, 384, 384, 384, 1152, 384, 6, 6])]
R_SSD_XBC, R_SSD_Z, R_SSD_DT, R_DIL, R_DN_QKV, R_DN_G, R_END = _R[4], _R[5], _R[6], _R[7], _R[11], _R[12], _R[15]
_SEGMENTS = ((C_MOBA, 0, 4 * W_MIX), (C_SSD_X, R_SSD_XBC, W_MIX), (C_SSD_Z, R_SSD_Z, W_MIX),
             (C_DN_QKV, R_DN_QKV, 3 * W_MIX), (C_DN_G, R_DN_G, W_MIX), (C_DIL, R_DIL, 4 * W_MIX),
             (C_SSD_B, R_SSD_XBC + W_MIX, 2 * SSD_GROUPS * SSD_STATE), (C_SMALL_A, R_SSD_DT, LANES),
             (C_SMALL_B, R_END - LANES, LANES))
_BLOCK_SRC = np.zeros(PROJ_COLS // LANES, np.int32)
for _dst, _src, _n in _SEGMENTS:
    _BLOCK_SRC[_dst // LANES:(_dst + _n) // LANES] = _src + LANES * np.arange(_n // LANES)
assert R_SSD_DT + LANES <= R_END


def _repack_kernel(src_ref, w_ref, o_ref, *, layer):
    o_ref[...] = w_ref[:, layer, :].astype(BF16)


def _repack_w_in(w_t, layer):
    _, depth, dm = w_t.shape
    return pl.pallas_call(
        functools.partial(_repack_kernel, layer=layer),
        out_shape=jax.ShapeDtypeStruct((PROJ_COLS, dm), BF16),
        grid_spec=pltpu.PrefetchScalarGridSpec(
            num_scalar_prefetch=1, grid=(PROJ_COLS // LANES,),
            in_specs=[pl.BlockSpec((pl.Element(LANES), pl.Element(depth), pl.Element(dm)),
                                   lambda b, src: (src[b], 0, 0))],
            out_specs=pl.BlockSpec((LANES, dm), lambda b, src: (b, 0))),
        compiler_params=pltpu.CompilerParams(dimension_semantics=("parallel",), vmem_limit_bytes=VMEM_LIMIT),
        name="repack_w_in",
    )(jnp.asarray(_BLOCK_SRC), w_t)


def _silu(x):
    return x * jax.nn.sigmoid(x)


def _softplus(x):
    return jnp.maximum(x, 0.0) + jnp.log(1.0 + jnp.exp(-jnp.abs(x)))


def _bdot(a, b):
    return jnp.dot(a.astype(BF16), b.astype(BF16), preferred_element_type=F32)


def _bdot_nt(a, b):
    return lax.dot_general(a.astype(BF16), b.astype(BF16), (((1,), (1,)), ((), ())),
                           preferred_element_type=F32)


def _bdot_tn(a, b):
    return lax.dot_general(a.astype(BF16), b.astype(BF16), (((0,), (0,)), ((), ())),
                           preferred_element_type=F32)


def _fdot(a, b):
    return jnp.dot(a, b, precision=HIGHEST, preferred_element_type=F32)


def _causal_conv_silu(x, tail, w, b):
    L = x.shape[0]
    xx = jnp.concatenate([tail, x], axis=0)
    y = b
    for j in range(CONV_WIDTH):
        off = TAIL - (CONV_WIDTH - 1) + j
        y = y + w[j:j + 1, :] * xx[off:off + L, :]
    return _silu(y)


def _inproj_kernel(x_ref, w_ref, o_ref, xb_ref):
    @pl.when(pl.program_id(1) == 0)
    def _():
        xb_ref[...] = x_ref[...].astype(BF16)

    o_ref[...] = lax.dot_general(xb_ref[...], w_ref[...], (((1,), (1,)), ((), ())), preferred_element_type=F32)


def _inproj(x2d, w_t):
    T = x2d.shape[0]
    tm = min(1024, T)
    tn = 1536
    return pl.pallas_call(
        _inproj_kernel,
        out_shape=jax.ShapeDtypeStruct((T, PROJ_COLS), F32),
        grid=(T // tm, PROJ_COLS // tn),
        in_specs=[pl.BlockSpec((tm, D_MODEL), lambda i, j: (i, 0)),
                  pl.BlockSpec((tn, D_MODEL), lambda i, j: (j, 0))],
        out_specs=pl.BlockSpec((tm, tn), lambda i, j: (i, j)),
        scratch_shapes=[pltpu.VMEM((tm, D_MODEL), BF16)],
        compiler_params=pltpu.CompilerParams(
            dimension_semantics=("parallel", "arbitrary"), vmem_limit_bytes=VMEM_LIMIT),
        name="inproj",
    )(x2d, w_t)


def _moba_kernel(slopes_ref, q_ref, k_ref, v_ref, g_ref, o_ref, kaug_ref, vt_ref, kmean_ref, s_ref, *, nblk):
    p = pl.program_id(1)
    i = pl.program_id(2)
    L = MOBA_BLOCK

    NF = MOBA_NFEAT

    @pl.when(i == 0)
    def _():
        lane = lax.broadcasted_iota(jnp.int32, (L, LANES), 1)
        feat_lane = lane % HEAD_DIM
        c_loc = lax.broadcasted_iota(jnp.int32, (L, LANES), 0).astype(F32)
        feat = jnp.where(feat_lane < NF, 1.0, jnp.where(feat_lane < 2 * NF, c_loc, 0.0))
        ones = jnp.ones((MOBA_LROWS, L), F32)
        for n in range(nblk):
            kblk = k_ref[n * L:(n + 1) * L, :]
            kmean_ref[n:n + 1, :] = jnp.mean(kblk, axis=0, keepdims=True)
            kaug_ref[0, n] = jnp.where(lane < HEAD_DIM, kblk, feat).astype(BF16)
            kaug_ref[1, n] = jnp.where(lane < HEAD_DIM, feat, kblk).astype(BF16)
            vt = v_ref[n * L:(n + 1) * L, :].T
            for h in range(2):
                vt_ref[h, n] = jnp.concatenate([ones, vt[h * HEAD_DIM:(h + 1) * HEAD_DIM, :]], axis=0).astype(BF16)

    qt = q_ref[...].T
    qrow = lax.broadcasted_iota(jnp.int32, (LANES, L), 0)
    r_loc = lax.broadcasted_iota(jnp.int32, (1, L), 1).astype(F32)
    brow = lax.broadcasted_iota(jnp.int32, (nblk, L), 0).astype(F32)
    frow = lax.broadcasted_iota(jnp.int32, (2 * TAIL, L), 0)
    kmean = kmean_ref[...]
    i_f = i.astype(F32)

    qparts, sels, slope_rows, zeros48 = [], [], [], jnp.zeros((HEAD_DIM - 2 * TAIL, L), BF16)
    for h in range(2):
        in_head = (qrow >= HEAD_DIM * h) & (qrow < HEAD_DIM * (h + 1))
        gate = jnp.dot(kmean, jnp.where(in_head, qt, 0.0), precision=HIGHEST, preferred_element_type=F32)
        gate = jnp.where(brow < i_f, gate, -jnp.inf)
        sel_h = []
        for kk in range(MOBA_TOPK):
            mx = jnp.max(gate, axis=0, keepdims=True)
            idx = jnp.min(jnp.where(gate == mx, brow, float(nblk)), axis=0, keepdims=True)
            sel_h.append(jnp.where(kk < i, idx, -1.0))
            gate = jnp.where(brow == idx, -jnp.inf, gate)
        sels.append(sel_h)
        qparts.append((qt[h * HEAD_DIM:(h + 1) * HEAD_DIM, :] * (ATTN_SCALE * LOG2E)).astype(BF16))
        slope_rows.append(jnp.full((1, L), slopes_ref[2 * p + h] * LOG2E, F32))

    def q_aug(h, t_row):
        terms = _split3(t_row) + _split3(slope_rows[h])
        feat = jnp.zeros((2 * TAIL, L), F32)
        for f, term in enumerate(terms):
            feat = jnp.where(frow == f, term.astype(F32), feat)
        feat = feat.astype(BF16)
        if h == 0:
            return jnp.concatenate([qparts[0], feat, zeros48], axis=0)
        return jnp.concatenate([feat, zeros48, qparts[1]], axis=0)

    U = MOBA_UNROLL

    def put_scores(slot, n0):
        for u in range(U):
            n = n0 + u
            n_f = n.astype(F32)
            dist0 = r_loc + L * (i_f - n_f)
            for h in range(2):
                picked = (sels[h][0] == n_f) | (sels[h][1] == n_f) | (sels[h][2] == n_f)
                qa = q_aug(h, jnp.where(picked, -slope_rows[h] * dist0, NEG))
                s_ref[slot, h, u] = jnp.dot(kaug_ref[h, jnp.minimum(n, nblk - 1)], qa,
                                            preferred_element_type=F32).astype(BF16)

    def new_max(slot, carry):
        return [functools.reduce(jnp.maximum, [carry[2 * h]] + [
            jnp.max(s_ref[slot, h, u], axis=0, keepdims=True).astype(F32) for u in range(U)]) for h in range(2)]

    def absorb(slot, n0, m_new, carry):
        ps = [jnp.concatenate([jnp.exp2(s_ref[slot, h, u] - m_new[h].astype(BF16)) for u in range(U)], axis=0)
              for h in range(2)]
        vts = [jnp.concatenate([vt_ref[h, jnp.minimum(n0 + u, nblk - 1)] for u in range(U)], axis=1) for h in range(2)]
        pv = [jnp.dot(vts[h], ps[h], preferred_element_type=F32) for h in range(2)]
        out = []
        for h in range(2):
            out += [m_new[h], jnp.exp2(carry[2 * h] - m_new[h]) * carry[2 * h + 1] + pv[h]]
        return tuple(out)

    put_scores(0, jnp.int32(0))

    c_idx = lax.broadcasted_iota(jnp.int32, (L, L), 0)
    r_idx = lax.broadcasted_iota(jnp.int32, (L, L), 1)
    own = [jnp.where(c_idx <= r_idx,
                     jnp.dot(kaug_ref[h, i], q_aug(h, -slope_rows[h] * r_loc), preferred_element_type=F32),
                     NEG).astype(BF16) for h in range(2)]
    m0 = [jnp.max(s, axis=0, keepdims=True) for s in own]
    p0 = [jnp.exp2(s - m) for s, m in zip(own, m0)]
    init = []
    for h in range(2):
        init += [m0[h].astype(F32), jnp.dot(vt_ref[h, i], p0[h], preferred_element_type=F32)]

    def body(j, carry):
        n0 = 2 * U * j
        m_a = new_max(0, carry)
        put_scores(1, n0 + U)
        carry = absorb(0, n0, m_a, carry)
        m_b = new_max(1, carry)
        put_scores(0, n0 + 2 * U)
        return absorb(1, n0 + U, m_b, carry)

    res = lax.fori_loop(0, (i + 2 * U - 1) // (2 * U), body, tuple(init))
    out_t = jnp.concatenate([res[2 * h + 1][MOBA_LROWS:] / res[2 * h + 1][0:1] for h in range(2)], axis=0)
    o_ref[...] = out_t.T * _silu(g_ref[...])


def _moba(proj3, slopes):
    B, S, _ = proj3.shape
    L = MOBA_BLOCK
    nblk = S // L
    cb = C_MOBA // LANES
    return pl.pallas_call(
        functools.partial(_moba_kernel, nblk=nblk),
        out_shape=jax.ShapeDtypeStruct((B, S, W_MIX), F32),
        grid=(B, N_PAIRS, nblk),
        in_specs=[pl.BlockSpec(memory_space=pltpu.SMEM),
                  pl.BlockSpec((None, L, LANES), lambda b, p, i: (b, i, cb + p)),
                  pl.BlockSpec((None, S, LANES), lambda b, p, i: (b, 0, cb + N_PAIRS + p)),
                  pl.BlockSpec((None, S, LANES), lambda b, p, i: (b, 0, cb + 2 * N_PAIRS + p)),
                  pl.BlockSpec((None, L, LANES), lambda b, p, i: (b, i, cb + 3 * N_PAIRS + p))],
        out_specs=pl.BlockSpec((None, L, LANES), lambda b, p, i: (b, i, p)),
        scratch_shapes=[pltpu.VMEM((2, nblk, L, LANES), BF16),
                        pltpu.VMEM((2, nblk, MOBA_LROWS + HEAD_DIM, L), BF16),
                        pltpu.VMEM((nblk, LANES), F32),
                        pltpu.VMEM((2, 2, MOBA_UNROLL, L, L), BF16)],
        compiler_params=pltpu.CompilerParams(
            dimension_semantics=("parallel", "parallel", "arbitrary"), vmem_limit_bytes=VMEM_LIMIT),
        name="moba",
    )(slopes, proj3, proj3, proj3, proj3)


def _dil_kernel(slopes_ref, q_ref, kc_ref, kp_ref, vc_ref, vp_ref, g_ref, o_ref, acc_ref, m_ref, l_ref, *, tq):
    p = pl.program_id(1)
    first = pl.program_id(2) == 0
    SB = DIL_SPAN
    lane = lax.broadcasted_iota(jnp.int32, (1, LANES), 1)
    lo = lane < HEAD_DIM
    row = lax.broadcasted_iota(jnp.int32, (SB, 2 * SB), 0)
    col = lax.broadcasted_iota(jnp.int32, (SB, 2 * SB), 1)
    dist = (row + SB - col).astype(F32)
    band = (col >= row) & (col <= row + SB)
    no_halo = jnp.logical_and(first, col < SB)

    def rows(start, d):
        return pl.ds(start, SB) if d == 1 else pl.ds(start, SB, stride=d)

    for gi, d in enumerate(DIL_DILATIONS):
        span = d * SB
        bias, bias_first = [], []
        for h in range(2):
            slope = slopes_ref[2 * p + h] * float(d)
            bias.append(jnp.where(band, -slope * dist, NEG))
            bias_first.append(jnp.where(no_halo, NEG, bias[h]))
        problems = [(r, j) for j in range(tq // span) for r in range(d)]
        for c0 in range(0, len(problems), DIL_BATCH):
            chunk = problems[c0:c0 + DIL_BATCH]
            kv, qs = [], []
            for r, j in chunk:
                cur = rows(r + span * j, d)
                if j == 0:
                    prev = rows(tq - span + r, d)
                    kp, vp = kp_ref[prev, :], vp_ref[prev, :]
                else:
                    prev = rows(r + span * (j - 1), d)
                    kp, vp = kc_ref[prev, :], vc_ref[prev, :]
                kv.append((jnp.concatenate([kp, kc_ref[cur, :]], axis=0).astype(BF16),
                           jnp.concatenate([vp, vc_ref[cur, :]], axis=0).astype(BF16)))
                qs.append(q_ref[cur, :] * ATTN_SCALE)
            work = [(c, h) for c in range(len(chunk)) for h in range(2)]
            ss = [_bdot_nt(jnp.where(lo if h == 0 else jnp.logical_not(lo), qs[c], 0.0), kv[c][0])
                  + (bias_first[h] if chunk[c][1] == 0 else bias[h]) for c, h in work]
            ms = [jnp.max(s, axis=1, keepdims=True) for s in ss]
            es = [jnp.exp(s - m) for s, m in zip(ss, ms)]
            dens = [jnp.sum(e, axis=1, keepdims=True) for e in es]
            pvs = [jnp.dot(e.astype(BF16), kv[c][1], preferred_element_type=F32) for e, (c, h) in zip(es, work)]
            for c, (r, j) in enumerate(chunk):
                cur = rows(r + span * j, d)
                acc_ref[gi, cur, :] = jnp.where(lo, pvs[2 * c], pvs[2 * c + 1])
                m_ref[gi, cur, :] = jnp.where(lo, ms[2 * c], ms[2 * c + 1])
                l_ref[gi, cur, :] = jnp.where(lo, dens[2 * c], dens[2 * c + 1])

    for c0 in range(0, tq, DIL_MERGE_ROWS):
        rs = slice(c0, c0 + DIL_MERGE_ROWS)
        ms = [m_ref[gi, rs, :] for gi in range(DIL_GROUPS)]
        top = functools.reduce(jnp.maximum, ms)
        ws = [jnp.exp(m - top) for m in ms]
        num = sum(w * acc_ref[gi, rs, :] for gi, w in enumerate(ws))
        den = sum(w * l_ref[gi, rs, :] for gi, w in enumerate(ws))
        o_ref[rs, :] = num / den * _silu(g_ref[rs, :])


def _dilated(proj3, slopes):
    B, S, _ = proj3.shape
    tq = DIL_DILATIONS[-1] * DIL_SPAN
    base = C_DIL // LANES
    cur = lambda off: pl.BlockSpec((None, tq, LANES), lambda b, p, t: (b, t, base + off + p))
    prev = lambda off: pl.BlockSpec((None, tq, LANES), lambda b, p, t: (b, jnp.maximum(t - 1, 0), base + off + p))
    scratch = pltpu.VMEM((DIL_GROUPS, tq, LANES), F32)
    return pl.pallas_call(
        functools.partial(_dil_kernel, tq=tq),
        out_shape=jax.ShapeDtypeStruct((B, S, W_MIX), F32),
        grid=(B, N_PAIRS, S // tq),
        in_specs=[pl.BlockSpec(memory_space=pltpu.SMEM),
                  cur(0), cur(N_PAIRS), prev(N_PAIRS), cur(2 * N_PAIRS), prev(2 * N_PAIRS), cur(3 * N_PAIRS)],
        out_specs=pl.BlockSpec((None, tq, LANES), lambda b, p, t: (b, t, p)),
        scratch_shapes=[scratch, scratch, scratch],
        compiler_params=pltpu.CompilerParams(
            dimension_semantics=("parallel", "parallel", "arbitrary"), vmem_limit_bytes=VMEM_LIMIT),
        name="dilated",
    )(slopes, proj3, proj3, proj3, proj3, proj3, proj3)


def _ssd_kernel(x_ref, z_ref, b_ref, c_ref, sm_ref, cw_ref, cb_ref, dtb_ref, alog_ref, dskip_ref, nw_ref,
                o_ref, tx_ref, tb_ref, tc_ref, st_ref):
    L = SSD_CHUNK
    N = SSD_STATE

    @pl.when(pl.program_id(1) == 0)
    def _():
        tx_ref[...] = jnp.zeros_like(tx_ref)
        tb_ref[...] = jnp.zeros_like(tb_ref)
        tc_ref[...] = jnp.zeros_like(tc_ref)
        st_ref[...] = jnp.zeros_like(st_ref)

    cw = cw_ref[...]
    cb = cb_ref[...]
    xr, br, cr = x_ref[...], b_ref[...], c_ref[...]
    xs = _causal_conv_silu(xr, tx_ref[...], cw[:, :W_MIX], cb[:, :W_MIX])
    Bm = _causal_conv_silu(br, tb_ref[...], cw[:, W_MIX:W_MIX + 2 * N], cb[:, W_MIX:W_MIX + 2 * N])
    Cm = _causal_conv_silu(cr, tc_ref[...], cw[:, W_MIX + 2 * N:], cb[:, W_MIX + 2 * N:])
    tx_ref[...] = xr[L - TAIL:, :]
    tb_ref[...] = br[L - TAIL:, :]
    tc_ref[...] = cr[L - TAIL:, :]

    dt = _softplus(sm_ref[...] + dtb_ref[...])
    a = dt * (-jnp.exp(alog_ref[...]))
    row = lax.broadcasted_iota(jnp.int32, (L, L), 0)
    col = lax.broadcasted_iota(jnp.int32, (L, L), 1)
    causal = col <= row
    a_cum = _fdot(causal.astype(F32), a)
    e_row = lax.broadcasted_iota(jnp.int32, (LANES, W_MIX), 0)
    e_col = lax.broadcasted_iota(jnp.int32, (LANES, W_MIX), 1)
    expand = (e_col // HEAD_DIM == e_row).astype(F32)
    dt_w = _fdot(dt, expand)
    acum_w = _fdot(a_cum, expand)
    atot_w = acum_w[L - 1:L, :]
    xdt = xs * dt_w
    a_cum_t = a_cum.T

    lane = lax.broadcasted_iota(jnp.int32, (1, LANES), 1)
    lane_w = lax.broadcasted_iota(jnp.int32, (1, W_MIX), 1)
    scores = [_bdot_nt(Cm[:, g * N:(g + 1) * N], Bm[:, g * N:(g + 1) * N]) for g in range(SSD_GROUPS)]
    y_pairs = []
    for p in range(N_PAIRS):
        res = []
        for hh in range(2):
            h = 2 * p + hh
            diff = a_cum[:, h:h + 1] - a_cum_t[h:h + 1, :]
            lmat = jnp.exp(jnp.where(causal, diff, NEG))
            res.append(_bdot(scores[h // (N_HEADS // SSD_GROUPS)] * lmat, xdt[:, p * LANES:(p + 1) * LANES]))
        y_pairs.append(jnp.where(lane < HEAD_DIM, res[0], res[1]))
    y_diag = jnp.concatenate(y_pairs, axis=1)

    state = st_ref[...]
    in_g0 = lane_w < W_MIX // SSD_GROUPS
    y_off = jnp.where(in_g0, _bdot(Cm[:, :N], state), _bdot(Cm[:, N:], state)) * jnp.exp(acum_w)
    xdec = xdt * jnp.exp(atot_w - acum_w)
    contrib = jnp.where(in_g0, _bdot_tn(Bm[:, :N], xdec), _bdot_tn(Bm[:, N:], xdec))
    st_ref[...] = state * jnp.exp(atot_w) + contrib

    y = y_diag + y_off + dskip_ref[...] * xs
    y = y * _silu(z_ref[...])
    y = y * lax.rsqrt(jnp.mean(y * y, axis=1, keepdims=True) + RMS_EPS) * nw_ref[...]
    o_ref[...] = y


def _pad_lanes(v, offset):
    return jnp.zeros((1, LANES), F32).at[0, offset:offset + v.shape[0]].set(v.astype(F32))


def _ssd(proj3, conv_w, conv_b, dt_bias, a_log, d_skip, norm_w):
    B, S, _ = proj3.shape
    L = SSD_CHUNK
    N2 = 2 * SSD_STATE
    full = lambda shape: pl.BlockSpec(shape, lambda b, c: (0, 0))
    return pl.pallas_call(
        _ssd_kernel,
        out_shape=jax.ShapeDtypeStruct((B, S, W_MIX), F32),
        grid=(B, S // L),
        in_specs=[pl.BlockSpec((None, L, W_MIX), lambda b, c: (b, c, C_SSD_X // W_MIX)),
                  pl.BlockSpec((None, L, W_MIX), lambda b, c: (b, c, C_SSD_Z // W_MIX)),
                  pl.BlockSpec((None, L, N2), lambda b, c: (b, c, C_SSD_B // N2)),
                  pl.BlockSpec((None, L, N2), lambda b, c: (b, c, C_SSD_C // N2)),
                  pl.BlockSpec((None, L, LANES), lambda b, c: (b, c, C_SMALL_A // LANES)),
                  full((CONV_WIDTH, SSD_XBC)), full((1, SSD_XBC)), full((1, LANES)), full((1, LANES)),
                  full((1, W_MIX)), full((1, W_MIX))],
        out_specs=pl.BlockSpec((None, L, W_MIX), lambda b, c: (b, c, 0)),
        scratch_shapes=[pltpu.VMEM((TAIL, W_MIX), F32), pltpu.VMEM((TAIL, N2), F32),
                        pltpu.VMEM((TAIL, N2), F32), pltpu.VMEM((SSD_STATE, W_MIX), F32)],
        compiler_params=pltpu.CompilerParams(
            dimension_semantics=("parallel", "arbitrary"), vmem_limit_bytes=VMEM_LIMIT),
        name="ssd",
    )(proj3, proj3, proj3, proj3, proj3, conv_w, conv_b.reshape(1, -1),
      _pad_lanes(dt_bias, SM_DT), _pad_lanes(a_log, SM_DT),
      jnp.repeat(d_skip.astype(F32), HEAD_DIM).reshape(1, -1), norm_w.reshape(1, -1))


INV_LEAF = 16


def _pair_blockdiag(x):
    lo = lax.broadcasted_iota(jnp.int32, (1, LANES), 1) < HEAD_DIM
    xb = x.astype(BF16)
    zero = jnp.zeros_like(xb)
    return jnp.concatenate([jnp.where(lo, xb, zero), jnp.where(lo, zero, xb)], axis=0)


def _pair_mm(a, b):
    return jnp.dot(a.astype(BF16), _pair_blockdiag(b), preferred_element_type=F32)


def _pair_unit_lower_inverse(lms, eye, on_diag_block):
    mm = lambda xs, ys: [_pair_mm(x, y) for x, y in zip(xs, ys)]
    d = [jnp.where(on_diag_block, lm, 0.0) for lm in lms]
    off = [lm - x for lm, x in zip(lms, d)]
    d2 = mm(d, d)
    d4 = mm(d2, d2)
    d8 = mm(d4, d4)
    left = mm([eye - x for x in d], [eye + x for x in d2])
    right = mm([eye + x for x in d4], [eye + x for x in d8])
    dinv = mm(left, right)
    m = mm(dinv, off)
    m2 = mm(m, m)
    return mm(mm([eye - x for x in m], [eye + x for x in m2]), dinv)


def _split3(x):
    hi = x.astype(BF16)
    r = x - hi.astype(F32)
    mid = r.astype(BF16)
    lo = (r - mid.astype(F32)).astype(BF16)
    return hi, mid, lo


def _xdot(a01, x):
    a = a01.astype(BF16)
    return sum(jnp.dot(a, t, preferred_element_type=F32) for t in _split3(x))


def _xdot_r(x, b01):
    b = b01.astype(BF16)
    return sum(jnp.dot(t, b, preferred_element_type=F32) for t in _split3(x))


def _gdn_kernel(qkv_ref, g_ref, sm_ref, cw_ref, cb_ref, dtb_ref, alog_ref, nw_ref, o_ref, tail_ref, st_ref, *, tl):
    C = DELTA_CHUNK

    @pl.when(pl.program_id(1) == 0)
    def _():
        tail_ref[...] = jnp.zeros_like(tail_ref)
        st_ref[...] = jnp.zeros_like(st_ref)

    raw = qkv_ref[...]
    qkv = _causal_conv_silu(raw, tail_ref[...], cw_ref[...], cb_ref[...])
    tail_ref[...] = raw[tl - TAIL:, :]
    q, k, v = qkv[:, :W_MIX], qkv[:, W_MIX:2 * W_MIX], qkv[:, 2 * W_MIX:]

    w_row = lax.broadcasted_iota(jnp.int32, (W_MIX, W_MIX), 0)
    w_col = lax.broadcasted_iota(jnp.int32, (W_MIX, W_MIX), 1)
    same_head = (w_row // HEAD_DIM == w_col // HEAD_DIM).astype(BF16)
    q = q * lax.rsqrt(_bdot(q * q, same_head) + RMS_EPS) * ATTN_SCALE
    k = k * lax.rsqrt(_bdot(k * k, same_head) + RMS_EPS)

    sm = sm_ref[...]
    beta = jax.nn.sigmoid(sm)
    gdec = -jnp.exp(alog_ref[...]) * _softplus(sm + dtb_ref[...])
    t_row = lax.broadcasted_iota(jnp.int32, (tl, tl), 0)
    t_col = lax.broadcasted_iota(jnp.int32, (tl, tl), 1)
    chunk_tri = (t_col <= t_row) & (t_row // C == t_col // C)
    gc = _xdot(chunk_tri, gdec)
    e_row = lax.broadcasted_iota(jnp.int32, (LANES, W_MIX), 0)
    e_col = lax.broadcasted_iota(jnp.int32, (LANES, W_MIX), 1)
    beta_w = _xdot_r(beta, e_col // HEAD_DIM + SM_BETA == e_row)
    gc_w = _xdot_r(gc, e_col // HEAD_DIM + SM_DECAY == e_row)

    kb = k * beta_w
    vb = v * beta_w
    egc = jnp.exp(gc_w)
    kbe = kb * egc
    qe = q * egc

    row = lax.broadcasted_iota(jnp.int32, (C, LANES), 0)
    col = lax.broadcasted_iota(jnp.int32, (C, LANES), 1) % HEAD_DIM
    eye = (row == col).astype(F32)
    incl = col <= row
    strict = col < row
    on_diag_block = (row // INV_LEAF) == (col // INV_LEAF)
    s_row = lax.broadcasted_iota(jnp.int32, (LANES, LANES), 0)
    s_col = lax.broadcasted_iota(jnp.int32, (LANES, LANES), 1)
    pair_diag = (s_row // HEAD_DIM) == (s_col // HEAD_DIM)

    tiles = [(slice(c * C, (c + 1) * C), slice(p * LANES, (p + 1) * LANES))
             for c in range(tl // C) for p in range(N_PAIRS)]
    grams, decays = [], []
    for rows, cols in tiles:
        gcol = gc_w[rows, cols]
        grow = jnp.sum(gcol * eye, axis=0, keepdims=True)
        decays.append(jnp.exp(jnp.where(incl, gcol - grow, NEG)))
        grams.append(lax.dot_general(jnp.concatenate([kb[rows, cols], q[rows, cols]], axis=0).astype(BF16),
                                     _pair_blockdiag(k[rows, cols]), (((1,), (1,)), ((), ())),
                                     preferred_element_type=F32))
    lms = [jnp.where(strict, g[:C] * d, 0.0) for g, d in zip(grams, decays)]
    attns = [jnp.where(incl, g[C:] * d, 0.0) for g, d in zip(grams, decays)]
    t_invs = _pair_unit_lower_inverse(lms, eye, on_diag_block)
    uws = [jnp.dot(t.astype(BF16),
                   jnp.concatenate([_pair_blockdiag(vb[rows, cols]), _pair_blockdiag(kbe[rows, cols])], axis=1),
                   preferred_element_type=F32) for t, (rows, cols) in zip(t_invs, tiles)]
    pre = []
    for (rows, cols), uw, attn in zip(tiles, uws, attns):
        g_last = gc_w[rows.stop - 1:rows.stop, cols]
        k_dec = k[rows, cols] * jnp.exp(g_last - gc_w[rows, cols])
        pre.append((uw[:, :LANES], uw[:, LANES:], attn, qe[rows, cols], k_dec, jnp.exp(g_last)))

    for c in range(tl // C):
        o_pairs = []
        for p in range(N_PAIRS):
            u2, w2, attn, qe2, k_dec2, last_decay = pre[c * N_PAIRS + p]
            state = st_ref[p]
            sb = state.astype(BF16)
            v_new = u2 - jnp.dot(w2.astype(BF16), sb, preferred_element_type=F32)
            o_pairs.append(jnp.dot(qe2.astype(BF16), sb, preferred_element_type=F32) + _pair_mm(attn, v_new))
            st_ref[p] = state * last_decay + jnp.where(pair_diag, _bdot_tn(k_dec2, v_new), 0.0)
        o_ref[c * C:(c + 1) * C, :] = jnp.concatenate(o_pairs, axis=1)

    o = o_ref[...]
    o = o * lax.rsqrt(_bdot(o * o, same_head) * (1.0 / HEAD_DIM) + RMS_EPS) * nw_ref[...]
    o_ref[...] = o * _silu(g_ref[...])


def _gdn(proj3, conv_w, conv_b, dt_bias, a_log, norm_w):
    B, S, _ = proj3.shape
    tl = 256
    full = lambda shape: pl.BlockSpec(shape, lambda b, c: (0, 0))
    return pl.pallas_call(
        functools.partial(_gdn_kernel, tl=tl),
        out_shape=jax.ShapeDtypeStruct((B, S, W_MIX), F32),
        grid=(B, S // tl),
        in_specs=[pl.BlockSpec((None, tl, 3 * W_MIX), lambda b, c: (b, c, C_DN_QKV // (3 * W_MIX))),
                  pl.BlockSpec((None, tl, W_MIX), lambda b, c: (b, c, C_DN_G // W_MIX)),
                  pl.BlockSpec((None, tl, LANES), lambda b, c: (b, c, C_SMALL_B // LANES)),
                  full((CONV_WIDTH, 3 * W_MIX)), full((1, 3 * W_MIX)), full((1, LANES)), full((1, LANES)),
                  full((1, W_MIX))],
        out_specs=pl.BlockSpec((None, tl, W_MIX), lambda b, c: (b, c, 0)),
        scratch_shapes=[pltpu.VMEM((TAIL, 3 * W_MIX), F32), pltpu.VMEM((N_PAIRS, LANES, LANES), F32)],
        compiler_params=pltpu.CompilerParams(
            dimension_semantics=("parallel", "arbitrary"), vmem_limit_bytes=VMEM_LIMIT),
        name="gdn",
    )(proj3, proj3, proj3, conv_w, conv_b.reshape(1, -1),
      _pad_lanes(dt_bias, SM_DECAY), _pad_lanes(a_log, SM_DECAY), jnp.tile(norm_w.astype(F32), N_HEADS).reshape(1, -1))


def _outproj_kernel(x_ref, ya_ref, yb_ref, yc_ref, yd_ref, w_ref, lg_ref, lb_ref, o_ref):
    mix = (_bdot(ya_ref[...], w_ref[0]) + _bdot(yb_ref[...], w_ref[1])
           + _bdot(yc_ref[...], w_ref[2]) + _bdot(yd_ref[...], w_ref[3]))
    r = DEEPNORM_ALPHA * x_ref[...] + mix
    mu = jnp.mean(r, axis=1, keepdims=True)
    var = jnp.mean(jnp.square(r - mu), axis=1, keepdims=True)
    o_ref[...] = (r - mu) * lax.rsqrt(var + LN_EPS) * lg_ref[...] + lb_ref[...]


def _outproj(x2d, ya, yb, yc, yd, w_out, ln_g, ln_b):
    T = x2d.shape[0]
    tm = min(512, T)
    rowblk = lambda width: pl.BlockSpec((tm, width), lambda i: (i, 0))
    return pl.pallas_call(
        _outproj_kernel,
        out_shape=jax.ShapeDtypeStruct((T, D_MODEL), F32),
        grid=(T // tm,),
        in_specs=[rowblk(D_MODEL), rowblk(W_MIX), rowblk(W_MIX), rowblk(W_MIX), rowblk(W_MIX),
                  pl.BlockSpec((4, W_MIX, D_MODEL), lambda i: (0, 0, 0)),
                  pl.BlockSpec((1, D_MODEL), lambda i: (0, 0)), pl.BlockSpec((1, D_MODEL), lambda i: (0, 0))],
        out_specs=rowblk(D_MODEL),
        compiler_params=pltpu.CompilerParams(
            dimension_semantics=("parallel",), vmem_limit_bytes=VMEM_LIMIT),
        name="outproj",
    )(x2d, ya, yb, yc, yd, w_out, ln_g.reshape(1, -1), ln_b.reshape(1, -1))


def _alibi_slopes():
    n = 2 * N_HEADS
    s = (2.0 ** (-8.0 * (np.arange(n) + 1) / n)).astype(np.float32)
    return jnp.asarray(s[:N_HEADS]), jnp.asarray(s[N_HEADS:])


def _layer(x, w_in_t, w_out, ssm_conv_w, ssm_conv_b, ssm_dt_bias, ssm_A_log, ssm_D, ssm_norm_w,
           dn_conv_w, dn_conv_b, dn_dt_bias, dn_A_log, dn_norm_w, ln_g, ln_b):
    B, S, _ = x.shape
    T = B * S
    slopes_dil, slopes_moba = _alibi_slopes()
    x2d = x.reshape(T, D_MODEL)
    proj2d = _inproj(x2d, w_in_t)
    proj3 = proj2d.reshape(B, S, PROJ_COLS)

    ya = _moba(proj3, slopes_moba)
    yb = _ssd(proj3, ssm_conv_w, ssm_conv_b, ssm_dt_bias, ssm_A_log, ssm_D, ssm_norm_w)
    yd = _gdn(proj3, dn_conv_w, dn_conv_b, dn_dt_bias, dn_A_log, dn_norm_w)

    yc = _dilated(proj3, slopes_dil)

    out = _outproj(x2d, ya.reshape(T, W_MIX), yb.reshape(T, W_MIX), yc.reshape(T, W_MIX), yd.reshape(T, W_MIX),
                   w_out.reshape(4, W_MIX, D_MODEL).astype(BF16), ln_g, ln_b)
    return out.reshape(B, S, D_MODEL)


def kernel(x, w_in, w_out, ssm_conv_w, ssm_conv_b, ssm_dt_bias, ssm_A_log, ssm_D, ssm_norm_w,
           dn_conv_w, dn_conv_b, dn_dt_bias, dn_A_log, dn_norm_w, ln_g, ln_b):
    assert x.shape[1] % 2048 == 0 and x.shape[2] == D_MODEL
    w_t = jnp.transpose(w_in, (2, 0, 1))
    for l in range(DEPTH):
        x = _layer(x, _repack_w_in(w_t, l), w_out[l], ssm_conv_w[l], ssm_conv_b[l], ssm_dt_bias[l], ssm_A_log[l],
                   ssm_D[l], ssm_norm_w[l], dn_conv_w[l], dn_conv_b[l], dn_dt_bias[l], dn_A_log[l],
                   dn_norm_w[l], ln_g[l], ln_b[l])
    return x
```

```python
import functools
import math

import numpy as np
import jax
import jax.numpy as jnp
from jax import lax
from jax.experimental import pallas as pl
from jax.experimental.pallas import tpu as pltpu

F32 = jnp.float32
BF16 = jnp.bfloat16
HIGHEST = lax.Precision.HIGHEST

D_MODEL = 1024
DEPTH = 2
HEAD_DIM = 64
N_HEADS = 6
W_MIX = N_HEADS * HEAD_DIM
N_PAIRS = W_MIX // 128
D_MIX = 4 * W_MIX
MOBA_BLOCK = 256
MOBA_TOPK = 3
MOBA_NFEAT = 3
MOBA_UNROLL = 2
MOBA_LROWS = 16
SSD_STATE = 128
SSD_GROUPS = 2
SSD_CHUNK = 128
SSD_TILE = 512
SSD_XBC = W_MIX + 2 * SSD_GROUPS * SSD_STATE
CONV_WIDTH = 4
DIL_DILATIONS = (1, 4, 16)
DIL_GROUPS = len(DIL_DILATIONS)
DIL_BATCH = 4
DIL_MERGE_ROWS = 256
DIL_SPAN = 128
DELTA_CHUNK = 64
LN_EPS = 1e-5
RMS_EPS = 1e-6
DEEPNORM_ALPHA = (2.0 * DEPTH) ** 0.25
ATTN_SCALE = HEAD_DIM ** -0.5
LOG2E = math.log2(math.e)
NEG = -1e30

LANES = 128
TAIL = 8
VMEM_LIMIT = 48 * 1024 * 1024

C_MOBA = 0
C_SSD_X = 1536
C_SSD_Z = 1920
C_DN_QKV = 2304
C_DN_G = 3456
C_DIL = 3840
C_SSD_B = 5376
C_SSD_C = 5632
C_SMALL_A = 5888
C_SMALL_B = 6016
PROJ_COLS = 6144
SM_DT, SM_BETA, SM_DECAY = 0, 116, 122

_R = [int(v) for v in np.cumsum([0, 384, 384, 384, 384, SSD_XBC, 384, 6, 384, 384, 384, 384, 1152, 384, 6, 6])]
R_SSD_XBC, R_SSD_Z, R_SSD_DT, R_DIL, R_DN_QKV, R_DN_G, R_END = _R[4], _R[5], _R[6], _R[7], _R[11], _R[12], _R[15]
_SEGMENTS = ((C_MOBA, 0, 4 * W_MIX), (C_SSD_X, R_SSD_XBC, W_MIX), (C_SSD_Z, R_SSD_Z, W_MIX),
             (C_DN_QKV, R_DN_QKV, 3 * W_MIX), (C_DN_G, R_DN_G, W_MIX), (C_DIL, R_DIL, 4 * W_MIX),
             (C_SSD_B, R_SSD_XBC + W_MIX, 2 * SSD_GROUPS * SSD_STATE), (C_SMALL_A, R_SSD_DT, LANES),
             (C_SMALL_B, R_END - LANES, LANES))
_BLOCK_SRC = np.zeros(PROJ_COLS // LANES, np.int32)
for _dst, _src, _n in _SEGMENTS:
    _BLOCK_SRC[_dst // LANES:(_dst + _n) // LANES] = _src + LANES * np.arange(_n // LANES)
assert R_SSD_DT + LANES <= R_END


def _repack_kernel(src_ref, w_ref, o_ref, *, layer):
    o_ref[...] = w_ref[:, layer, :].astype(BF16)


def _repack_w_in(w_t, layer):
    _, depth, dm = w_t.shape
    return pl.pallas_call(
        functools.partial(_repack_kernel, layer=layer),
        out_shape=jax.ShapeDtypeStruct((PROJ_COLS, dm), BF16),
        grid_spec=pltpu.PrefetchScalarGridSpec(
            num_scalar_prefetch=1, grid=(PROJ_COLS // LANES,),
            in_specs=[pl.BlockSpec((pl.Element(LANES), pl.Element(depth), pl.Element(dm)),
                                   lambda b, src: (src[b], 0, 0))],
            out_specs=pl.BlockSpec((LANES, dm), lambda b, src: (b, 0))),
        compiler_params=pltpu.CompilerParams(dimension_semantics=("parallel",), vmem_limit_bytes=VMEM_LIMIT),
        name="repack_w_in",
    )(jnp.asarray(_BLOCK_SRC), w_t)


def _silu(x):
    return x * jax.nn.sigmoid(x)


def _softplus(x):
    return jnp.maximum(x, 0.0) + jnp.log(1.0 + jnp.exp(-jnp.abs(x)))


def _bdot(a, b):
    return jnp.dot(a.astype(BF16), b.astype(BF16), preferred_element_type=F32)


def _bdot_nt(a, b):
    return lax.dot_general(a.astype(BF16), b.astype(BF16), (((1,), (1,)), ((), ())),
                           preferred_element_type=F32)


def _bdot_tn(a, b):
    return lax.dot_general(a.astype(BF16), b.astype(BF16), (((0,), (0,)), ((), ())),
                           preferred_element_type=F32)


def _fdot(a, b):
    return jnp.dot(a, b, precision=HIGHEST, preferred_element_type=F32)


def _causal_conv_silu(x_ref, tail, w, b):
    L = x_ref.shape[0]
    head = jnp.concatenate([tail, x_ref[:TAIL, :]], axis=0)
    y_head, y_rest = b, b
    for j in range(CONV_WIDTH):
        off = TAIL - (CONV_WIDTH - 1) + j
        y_head = y_head + w[j:j + 1, :] * head[off:off + TAIL, :]
        y_rest = y_rest + w[j:j + 1, :] * x_ref[pl.ds(off, L - TAIL), :]
    return _silu(jnp.concatenate([y_head, y_rest], axis=0))


def _inproj_kernel(x_ref, w_ref, o_ref, xb_ref):
    @pl.when(pl.program_id(1) == 0)
    def _():
        xb_ref[...] = x_ref[...].astype(BF16)

    o_ref[...] = lax.dot_general(xb_ref[...], w_ref[...], (((1,), (1,)), ((), ())), preferred_element_type=F32)


def _inproj(x2d, w_t):
    T = x2d.shape[0]
    tm = min(1024, T)
    tn = 1536
    return pl.pallas_call(
        _inproj_kernel,
        out_shape=jax.ShapeDtypeStruct((T, PROJ_COLS), F32),
        grid=(T // tm, PROJ_COLS // tn),
        in_specs=[pl.BlockSpec((tm, D_MODEL), lambda i, j: (i, 0)),
                  pl.BlockSpec((tn, D_MODEL), lambda i, j: (j, 0))],
        out_specs=pl.BlockSpec((tm, tn), lambda i, j: (i, j)),
        scratch_shapes=[pltpu.VMEM((tm, D_MODEL), BF16)],
        compiler_params=pltpu.CompilerParams(
            dimension_semantics=("parallel", "arbitrary"), vmem_limit_bytes=VMEM_LIMIT),
        name="inproj",
    )(x2d, w_t)


def _moba_kernel(slopes_ref, q_ref, k_ref, v_ref, g_ref, o_ref, kaug_ref, vt_ref, kmean_ref, s_ref, *, nblk):
    p = pl.program_id(1)
    i = pl.program_id(2)
    L = MOBA_BLOCK

    NF = MOBA_NFEAT

    @pl.when(i == 0)
    def _():
        lane = lax.broadcasted_iota(jnp.int32, (L, LANES), 1)
        feat_lane = lane % HEAD_DIM
        c_loc = lax.broadcasted_iota(jnp.int32, (L, LANES), 0).astype(F32)
        feat = jnp.where(feat_lane < NF, 1.0, jnp.where(feat_lane < 2 * NF, c_loc, 0.0))
        ones = jnp.ones((MOBA_LROWS, L), F32)
        for n in range(nblk):
            kblk = k_ref[n * L:(n + 1) * L, :]
            kmean_ref[n:n + 1, :] = jnp.mean(kblk, axis=0, keepdims=True)
            kaug_ref[0, n] = jnp.where(lane < HEAD_DIM, kblk, feat).astype(BF16)
            kaug_ref[1, n] = jnp.where(lane < HEAD_DIM, feat, kblk).astype(BF16)
            vt = v_ref[n * L:(n + 1) * L, :].T
            for h in range(2):
                vt_ref[h, n] = jnp.concatenate([ones, vt[h * HEAD_DIM:(h + 1) * HEAD_DIM, :]], axis=0).astype(BF16)

    qt = q_ref[...].T
    qrow = lax.broadcasted_iota(jnp.int32, (LANES, L), 0)
    r_loc = lax.broadcasted_iota(jnp.int32, (1, L), 1).astype(F32)
    brow = lax.broadcasted_iota(jnp.int32, (nblk, L), 0).astype(F32)
    frow = lax.broadcasted_iota(jnp.int32, (2 * TAIL, L), 0)
    kmean = kmean_ref[...]
    i_f = i.astype(F32)

    qparts, sels, slope_rows, zeros48 = [], [], [], jnp.zeros((HEAD_DIM - 2 * TAIL, L), BF16)
    for h in range(2):
        in_head = (qrow >= HEAD_DIM * h) & (qrow < HEAD_DIM * (h + 1))
        gate = jnp.dot(kmean, jnp.where(in_head, qt, 0.0), precision=HIGHEST, preferred_element_type=F32)
        gate = jnp.where(brow < i_f, gate, -jnp.inf)
        sel_h = []
        for kk in range(MOBA_TOPK):
            mx = jnp.max(gate, axis=0, keepdims=True)
            idx = jnp.min(jnp.where(gate == mx, brow, float(nblk)), axis=0, keepdims=True)
            sel_h.append(jnp.where(kk < i, idx, -1.0))
            gate = jnp.where(brow == idx, -jnp.inf, gate)
        sels.append(sel_h)
        qparts.append((qt[h * HEAD_DIM:(h + 1) * HEAD_DIM, :] * (ATTN_SCALE * LOG2E)).astype(BF16))
        slope_rows.append(jnp.full((1, L), slopes_ref[2 * p + h] * LOG2E, F32))

    def q_aug(h, t_row):
        terms = _split3(t_row) + _split3(slope_rows[h])
        feat = jnp.zeros((2 * TAIL, L), F32)
        for f, term in enumerate(terms):
            feat = jnp.where(frow == f, term.astype(F32), feat)
        feat = feat.astype(BF16)
        if h == 0:
            return jnp.concatenate([qparts[0], feat, zeros48], axis=0)
        return jnp.concatenate([feat, zeros48, qparts[1]], axis=0)

    U = MOBA_UNROLL

    def put_scores(slot, n0):
        for u in range(U):
            n = n0 + u
            n_f = n.astype(F32)
            dist0 = r_loc + L * (i_f - n_f)
            for h in range(2):
                picked = (sels[h][0] == n_f) | (sels[h][1] == n_f) | (sels[h][2] == n_f)
                qa = q_aug(h, jnp.where(picked, -slope_rows[h] * dist0, NEG))
                s_ref[slot, h, u] = jnp.dot(kaug_ref[h, jnp.minimum(n, nblk - 1)], qa,
                                            preferred_element_type=F32).astype(BF16)

    def new_max(slot, carry):
        return [functools.reduce(jnp.maximum, [carry[2 * h]] + [
            jnp.max(s_ref[slot, h, u], axis=0, keepdims=True).astype(F32) for u in range(U)]) for h in range(2)]

    def absorb(slot, n0, m_new, carry):
        ps = [jnp.concatenate([jnp.exp2(s_ref[slot, h, u] - m_new[h].astype(BF16)) for u in range(U)], axis=0)
              for h in range(2)]
        vts = [jnp.concatenate([vt_ref[h, jnp.minimum(n0 + u, nblk - 1)] for u in range(U)], axis=1) for h in range(2)]
        pv = [jnp.dot(vts[h], ps[h], preferred_element_type=F32) for h in range(2)]
        out = []
        for h in range(2):
            out += [m_new[h], jnp.exp2(carry[2 * h] - m_new[h]) * carry[2 * h + 1] + pv[h]]
        return tuple(out)

    put_scores(0, jnp.int32(0))

    c_idx = lax.broadcasted_iota(jnp.int32, (L, L), 0)
    r_idx = lax.broadcasted_iota(jnp.int32, (L, L), 1)
    own = [jnp.where(c_idx <= r_idx,
                     jnp.dot(kaug_ref[h, i], q_aug(h, -slope_rows[h] * r_loc), preferred_element_type=F32),
                     NEG).astype(BF16) for h in range(2)]
    m0 = [jnp.max(s, axis=0, keepdims=True) for s in own]
    p0 = [jnp.exp2(s - m) for s, m in zip(own, m0)]
    init = []
    for h in range(2):
        init += [m0[h].astype(F32), jnp.dot(vt_ref[h, i], p0[h], preferred_element_type=F32)]

    def body(j, carry):
        n0 = 2 * U * j
        m_a = new_max(0, carry)
        put_scores(1, n0 + U)
        carry = absorb(0, n0, m_a, carry)
        m_b = new_max(1, carry)
        put_scores(0, n0 + 2 * U)
        return absorb(1, n0 + U, m_b, carry)

    res = lax.fori_loop(0, (i + 2 * U - 1) // (2 * U), body, tuple(init))
    out_t = jnp.concatenate([res[2 * h + 1][MOBA_LROWS:] / res[2 * h + 1][0:1] for h in range(2)], axis=0)
    o_ref[...] = out_t.T * _silu(g_ref[...])


def _moba(proj3, slopes):
    B, S, _ = proj3.shape
    L = MOBA_BLOCK
    nblk = S // L
    cb = C_MOBA // LANES
    return pl.pallas_call(
        functools.partial(_moba_kernel, nblk=nblk),
        out_shape=jax.ShapeDtypeStruct((B, S, W_MIX), F32),
        grid=(B, N_PAIRS, nblk),
        in_specs=[pl.BlockSpec(memory_space=pltpu.SMEM),
                  pl.BlockSpec((None, L, LANES), lambda b, p, i: (b, i, cb + p)),
                  pl.BlockSpec((None, S, LANES), lambda b, p, i: (b, 0, cb + N_PAIRS + p)),
                  pl.BlockSpec((None, S, LANES), lambda b, p, i: (b, 0, cb + 2 * N_PAIRS + p)),
                  pl.BlockSpec((None, L, LANES), lambda b, p, i: (b, i, cb + 3 * N_PAIRS + p))],
        out_specs=pl.BlockSpec((None, L, LANES), lambda b, p, i: (b, i, p)),
        scratch_shapes=[pltpu.VMEM((2, nblk, L, LANES), BF16),
                        pltpu.VMEM((2, nblk, MOBA_LROWS + HEAD_DIM, L), BF16),
                        pltpu.VMEM((nblk, LANES), F32),
                        pltpu.VMEM((2, 2, MOBA_UNROLL, L, L), BF16)],
        compiler_params=pltpu.CompilerParams(
            dimension_semantics=("parallel", "parallel", "arbitrary"), vmem_limit_bytes=VMEM_LIMIT),
        name="moba",
    )(slopes, proj3, proj3, proj3, proj3)


def _dil_kernel(slopes_ref, q_ref, kc_ref, kp_ref, vc_ref, vp_ref, g_ref, o_ref, acc_ref, m_ref, l_ref, *, tq):
    p = pl.program_id(1)
    first = pl.program_id(2) == 0
    SB = DIL_SPAN
    lane = lax.broadcasted_iota(jnp.int32, (1, LANES), 1)
    lo = lane < HEAD_DIM
    row = lax.broadcasted_iota(jnp.int32, (SB, 2 * SB), 0)
    col = lax.broadcasted_iota(jnp.int32, (SB, 2 * SB), 1)
    dist = (row + SB - col).astype(F32)
    band = (col >= row) & (col <= row + SB)
    no_halo = jnp.logical_and(first, col < SB)

    def rows(start, d):
        return pl.ds(start, SB) if d == 1 else pl.ds(start, SB, stride=d)

    for gi, d in enumerate(DIL_DILATIONS):
        span = d * SB
        bias, bias_first = [], []
        for h in range(2):
            slope = slopes_ref[2 * p + h] * (float(d) * LOG2E)
            bias.append(jnp.where(band, -slope * dist, NEG))
            bias_first.append(jnp.where(no_halo, NEG, bias[h]))
        problems = [(r, j) for j in range(tq // span) for r in range(d)]
        for c0 in range(0, len(problems), DIL_BATCH):
            chunk = problems[c0:c0 + DIL_BATCH]
            kv, qs = [], []
            for r, j in chunk:
                cur = rows(r + span * j, d)
                if j == 0:
                    prev = rows(tq - span + r, d)
                    kp, vp = kp_ref[prev, :], vp_ref[prev, :]
                else:
                    prev = rows(r + span * (j - 1), d)
                    kp, vp = kc_ref[prev, :], vc_ref[prev, :]
                v2 = jnp.concatenate([vp, vc_ref[cur, :]], axis=0)
                kv.append((jnp.concatenate([kp, kc_ref[cur, :]], axis=0).astype(BF16),
                           (jnp.where(lo, v2, 1.0).astype(BF16), jnp.where(lo, 1.0, v2).astype(BF16))))
                qs.append(q_ref[cur, :] * (ATTN_SCALE * LOG2E))
            work = [(c, h) for c in range(len(chunk)) for h in range(2)]
            ss = [(_bdot_nt(jnp.where(lo if h == 0 else jnp.logical_not(lo), qs[c], 0.0), kv[c][0])
                   + (bias_first[h] if chunk[c][1] == 0 else bias[h])).astype(BF16) for c, h in work]
            ms = [jnp.max(s, axis=1, keepdims=True) for s in ss]
            ps = [jnp.exp2(s - m) for s, m in zip(ss, ms)]
            pvs = [jnp.dot(e, kv[c][1][h], preferred_element_type=F32) for e, (c, h) in zip(ps, work)]
            for c, (r, j) in enumerate(chunk):
                cur = rows(r + span * j, d)
                acc_ref[gi, cur, :] = jnp.where(lo, pvs[2 * c], pvs[2 * c + 1])
                m_ref[gi, cur, :] = jnp.where(lo, ms[2 * c].astype(F32), ms[2 * c + 1].astype(F32))
                l_ref[gi, cur, :] = pltpu.roll(jnp.where(lo, pvs[2 * c + 1], pvs[2 * c]), HEAD_DIM, axis=1)

    for c0 in range(0, tq, DIL_MERGE_ROWS):
        rs = slice(c0, c0 + DIL_MERGE_ROWS)
        ms = [m_ref[gi, rs, :] for gi in range(DIL_GROUPS)]
        top = functools.reduce(jnp.maximum, ms)
        ws = [jnp.exp2(m - top) for m in ms]
        num = sum(w * acc_ref[gi, rs, :] for gi, w in enumerate(ws))
        den = sum(w * l_ref[gi, rs, :] for gi, w in enumerate(ws))
        o_ref[rs, :] = num / den * _silu(g_ref[rs, :])


def _dilated(proj3, slopes):
    B, S, _ = proj3.shape
    tq = DIL_DILATIONS[-1] * DIL_SPAN
    base = C_DIL // LANES
    cur = lambda off: pl.BlockSpec((None, tq, LANES), lambda b, p, t: (b, t, base + off + p))
    prev = lambda off: pl.BlockSpec((None, tq, LANES), lambda b, p, t: (b, jnp.maximum(t - 1, 0), base + off + p))
    scratch = pltpu.VMEM((DIL_GROUPS, tq, LANES), F32)
    return pl.pallas_call(
        functools.partial(_dil_kernel, tq=tq),
        out_shape=jax.ShapeDtypeStruct((B, S, W_MIX), F32),
        grid=(B, N_PAIRS, S // tq),
        in_specs=[pl.BlockSpec(memory_space=pltpu.SMEM),
                  cur(0), cur(N_PAIRS), prev(N_PAIRS), cur(2 * N_PAIRS), prev(2 * N_PAIRS), cur(3 * N_PAIRS)],
        out_specs=pl.BlockSpec((None, tq, LANES), lambda b, p, t: (b, t, p)),
        scratch_shapes=[scratch, scratch, scratch],
        compiler_params=pltpu.CompilerParams(
            dimension_semantics=("parallel", "parallel", "arbitrary"), vmem_limit_bytes=VMEM_LIMIT),
        name="dilated",
    )(slopes, proj3, proj3, proj3, proj3, proj3, proj3)


def _ssd_kernel(x_ref, z_ref, b_ref, c_ref, sm_ref, cw_ref, cb_ref, dtb_ref, alog_ref, dskip_ref, nw_ref,
                tri_ref, expand_ref, o_ref, tx_ref, tb_ref, tc_ref, st_ref, *, tl):
    L = SSD_CHUNK
    N = SSD_STATE
    nchunk = tl // L

    @pl.when(pl.program_id(1) == 0)
    def _():
        tx_ref[...] = jnp.zeros_like(tx_ref)
        tb_ref[...] = jnp.zeros_like(tb_ref)
        tc_ref[...] = jnp.zeros_like(tc_ref)
        st_ref[...] = jnp.zeros_like(st_ref)

    cw = cw_ref[...]
    cb = cb_ref[...]
    xs = _causal_conv_silu(x_ref, tx_ref[...], cw[:, :W_MIX], cb[:, :W_MIX])
    Bm = _causal_conv_silu(b_ref, tb_ref[...], cw[:, W_MIX:W_MIX + 2 * N], cb[:, W_MIX:W_MIX + 2 * N])
    Cm = _causal_conv_silu(c_ref, tc_ref[...], cw[:, W_MIX + 2 * N:], cb[:, W_MIX + 2 * N:])
    tx_ref[...] = x_ref[tl - TAIL:, :]
    tb_ref[...] = b_ref[tl - TAIL:, :]
    tc_ref[...] = c_ref[tl - TAIL:, :]

    dt = _softplus(sm_ref[...] + dtb_ref[...])
    a = dt * (-jnp.exp(alog_ref[...]))
    a_cum = _xdot(tri_ref[...], a)
    expand = expand_ref[...]
    dt_w = _xdot_r(dt, expand)
    acum_w = _xdot_r(a_cum, expand)
    xdt = xs * dt_w
    a_cum_t = a_cum.T
    e_acum = jnp.exp(acum_w)

    lane = lax.broadcasted_iota(jnp.int32, (1, LANES), 1)
    lane_w = lax.broadcasted_iota(jnp.int32, (1, W_MIX), 1)
    in_g0 = lane_w < W_MIX // SSD_GROUPS
    row = lax.broadcasted_iota(jnp.int32, (L, L), 0)
    col = lax.broadcasted_iota(jnp.int32, (L, L), 1)
    causal = col <= row
    chunks = [slice(c * L, (c + 1) * L) for c in range(nchunk)]
    heads_per_group = N_HEADS // SSD_GROUPS

    scores = [[_bdot_nt(Cm[rs, g * N:(g + 1) * N], Bm[rs, g * N:(g + 1) * N]) for g in range(SSD_GROUPS)]
              for rs in chunks]
    atot = [acum_w[rs.stop - 1:rs.stop, :] for rs in chunks]
    xdec = [xdt[rs, :] * jnp.exp(atot[c] - acum_w[rs, :]) for c, rs in enumerate(chunks)]
    contrib = [jnp.where(in_g0, _bdot_tn(Bm[rs, :N], xdec[c]), _bdot_tn(Bm[rs, N:], xdec[c]))
               for c, rs in enumerate(chunks)]
    lmats = [[jnp.exp(jnp.where(causal, a_cum[rs, h:h + 1] - a_cum_t[h:h + 1, rs], NEG)) for h in range(N_HEADS)]
             for rs in chunks]
    diag = [[_bdot(scores[c][h // heads_per_group] * lmats[c][h], xdt[rs, (h // 2) * LANES:(h // 2 + 1) * LANES])
             for h in range(N_HEADS)] for c, rs in enumerate(chunks)]

    states = [st_ref[...]]
    for c in range(nchunk):
        states.append(states[c] * jnp.exp(atot[c]) + contrib[c])
    st_ref[...] = states[nchunk]
    sb = [s.astype(BF16) for s in states[:nchunk]]
    off = [jnp.where(in_g0, jnp.dot(Cm[rs, :N].astype(BF16), sb[c], preferred_element_type=F32),
                     jnp.dot(Cm[rs, N:].astype(BF16), sb[c], preferred_element_type=F32))
           for c, rs in enumerate(chunks)]

    for c, rs in enumerate(chunks):
        y_diag = jnp.concatenate([jnp.where(lane < HEAD_DIM, diag[c][2 * p], diag[c][2 * p + 1])
                                  for p in range(N_PAIRS)], axis=1)
        y = y_diag + off[c] * e_acum[rs, :] + dskip_ref[...] * xs[rs, :]
        y = y * _silu(z_ref[rs, :])
        o_ref[rs, :] = y * lax.rsqrt(jnp.mean(y * y, axis=1, keepdims=True) + RMS_EPS) * nw_ref[...]


def _pad_lanes(v, offset):
    return jnp.zeros((1, LANES), F32).at[0, offset:offset + v.shape[0]].set(v.astype(F32))


def _ssd(proj3, conv_w, conv_b, dt_bias, a_log, d_skip, norm_w):
    B, S, _ = proj3.shape
    L = SSD_TILE
    N2 = 2 * SSD_STATE
    full = lambda shape: pl.BlockSpec(shape, lambda b, c: (0, 0))
    t = np.arange(L)
    tri = (t[None, :] <= t[:, None]) & (t[:, None] // SSD_CHUNK == t[None, :] // SSD_CHUNK)
    expand = np.arange(W_MIX)[None, :] // HEAD_DIM == np.arange(LANES)[:, None]
    return pl.pallas_call(
        functools.partial(_ssd_kernel, tl=L),
        out_shape=jax.ShapeDtypeStruct((B, S, W_MIX), F32),
        grid=(B, S // L),
        in_specs=[pl.BlockSpec((None, L, W_MIX), lambda b, c: (b, c, C_SSD_X // W_MIX)),
                  pl.BlockSpec((None, L, W_MIX), lambda b, c: (b, c, C_SSD_Z // W_MIX)),
                  pl.BlockSpec((None, L, N2), lambda b, c: (b, c, C_SSD_B // N2)),
                  pl.BlockSpec((None, L, N2), lambda b, c: (b, c, C_SSD_C // N2)),
                  pl.BlockSpec((None, L, LANES), lambda b, c: (b, c, C_SMALL_A // LANES)),
                  full((CONV_WIDTH, SSD_XBC)), full((1, SSD_XBC)), full((1, LANES)), full((1, LANES)),
                  full((1, W_MIX)), full((1, W_MIX)), full((L, L)), full((LANES, W_MIX))],
        out_specs=pl.BlockSpec((None, L, W_MIX), lambda b, c: (b, c, 0)),
        scratch_shapes=[pltpu.VMEM((TAIL, W_MIX), F32), pltpu.VMEM((TAIL, N2), F32),
                        pltpu.VMEM((TAIL, N2), F32), pltpu.VMEM((SSD_STATE, W_MIX), F32)],
        compiler_params=pltpu.CompilerParams(
            dimension_semantics=("parallel", "arbitrary"), vmem_limit_bytes=VMEM_LIMIT),
        name="ssd",
    )(proj3, proj3, proj3, proj3, proj3, conv_w, conv_b.reshape(1, -1),
      _pad_lanes(dt_bias, SM_DT), _pad_lanes(a_log, SM_DT),
      jnp.repeat(d_skip.astype(F32), HEAD_DIM).reshape(1, -1), norm_w.reshape(1, -1),
      jnp.asarray(tri, BF16), jnp.asarray(expand, BF16))


INV_LEAF = 16


def _pair_blockdiag(x):
    lo = lax.broadcasted_iota(jnp.int32, (1, LANES), 1) < HEAD_DIM
    xb = x.astype(BF16)
    zero = jnp.zeros_like(xb)
    return jnp.concatenate([jnp.where(lo, xb, zero), jnp.where(lo, zero, xb)], axis=0)


def _pair_mm(a, b):
    return jnp.dot(a.astype(BF16), _pair_blockdiag(b), preferred_element_type=F32)


def _pair_unit_lower_inverse(lms, eye, on_diag_block):
    mm = lambda xs, ys: [_pair_mm(x, y) for x, y in zip(xs, ys)]
    d = [jnp.where(on_diag_block, lm, 0.0) for lm in lms]
    off = [lm - x for lm, x in zip(lms, d)]
    d2 = mm(d, d)
    d4 = mm(d2, d2)
    d8 = mm(d4, d4)
    left = mm([eye - x for x in d], [eye + x for x in d2])
    right = mm([eye + x for x in d4], [eye + x for x in d8])
    dinv = mm(left, right)
    m = mm(dinv, off)
    m2 = mm(m, m)
    return mm(mm([eye - x for x in m], [eye + x for x in m2]), dinv)


def _split3(x):
    hi = x.astype(BF16)
    r = x - hi.astype(F32)
    mid = r.astype(BF16)
    lo = (r - mid.astype(F32)).astype(BF16)
    return hi, mid, lo


def _xdot(a01, x):
    a = a01.astype(BF16)
    return sum(jnp.dot(a, t, preferred_element_type=F32) for t in _split3(x))


def _xdot_r(x, b01):
    b = b01.astype(BF16)
    return sum(jnp.dot(t, b, preferred_element_type=F32) for t in _split3(x))


def _gdn_kernel(qkv_ref, g_ref, sm_ref, cw_ref, cb_ref, dtb_ref, alog_ref, nw_ref, same_ref, tri_ref, eb_ref, ed_ref,
                o_ref, tail_ref, st_ref, *, tl):
    C = DELTA_CHUNK

    @pl.when(pl.program_id(1) == 0)
    def _():
        tail_ref[...] = jnp.zeros_like(tail_ref)
        st_ref[...] = jnp.zeros_like(st_ref)

    qkv = _causal_conv_silu(qkv_ref, tail_ref[...], cw_ref[...], cb_ref[...])
    tail_ref[...] = qkv_ref[tl - TAIL:, :]
    q, k, v = qkv[:, :W_MIX], qkv[:, W_MIX:2 * W_MIX], qkv[:, 2 * W_MIX:]

    same_head = same_ref[...]
    q = q * lax.rsqrt(_bdot(q * q, same_head) + RMS_EPS) * ATTN_SCALE
    k = k * lax.rsqrt(_bdot(k * k, same_head) + RMS_EPS)

    sm = sm_ref[...]
    beta = jax.nn.sigmoid(sm)
    gdec = -jnp.exp(alog_ref[...]) * _softplus(sm + dtb_ref[...])
    gc = _xdot(tri_ref[...], gdec)
    beta_w = _xdot_r(beta, eb_ref[...])
    gc_w = _xdot_r(gc, ed_ref[...])

    kb = k * beta_w
    vb = v * beta_w
    egc = jnp.exp(gc_w)
    kbe = kb * egc
    qe = q * egc

    row = lax.broadcasted_iota(jnp.int32, (C, LANES), 0)
    col = lax.broadcasted_iota(jnp.int32, (C, LANES), 1) % HEAD_DIM
    eye = (row == col).astype(F32)
    incl = col <= row
    strict = col < row
    on_diag_block = (row // INV_LEAF) == (col // INV_LEAF)
    s_row = lax.broadcasted_iota(jnp.int32, (LANES, LANES), 0)
    s_col = lax.broadcasted_iota(jnp.int32, (LANES, LANES), 1)
    pair_diag = (s_row // HEAD_DIM) == (s_col // HEAD_DIM)

    tiles = [(slice(c * C, (c + 1) * C), slice(p * LANES, (p + 1) * LANES))
             for c in range(tl // C) for p in range(N_PAIRS)]
    grams, decays = [], []
    for rows, cols in tiles:
        gcol = gc_w[rows, cols]
        grow = jnp.sum(gcol * eye, axis=0, keepdims=True)
        decays.append(jnp.exp(jnp.where(incl, gcol - grow, NEG)))
        grams.append(lax.dot_general(jnp.concatenate([kb[rows, cols], q[rows, cols]], axis=0).astype(BF16),
                                     _pair_blockdiag(k[rows, cols]), (((1,), (1,)), ((), ())),
                                     preferred_element_type=F32))
    lms = [jnp.where(strict, g[:C] * d, 0.0) for g, d in zip(grams, decays)]
    attns = [jnp.where(incl, g[C:] * d, 0.0) for g, d in zip(grams, decays)]
    t_invs = _pair_unit_lower_inverse(lms, eye, on_diag_block)
    uws = [jnp.dot(t.astype(BF16),
                   jnp.concatenate([_pair_blockdiag(vb[rows, cols]), _pair_blockdiag(kbe[rows, cols])], axis=1),
                   preferred_element_type=F32) for t, (rows, cols) in zip(t_invs, tiles)]
    pre = []
    for (rows, cols), uw, attn in zip(tiles, uws, attns):
        g_last = gc_w[rows.stop - 1:rows.stop, cols]
        k_dec = k[rows, cols] * jnp.exp(g_last - gc_w[rows, cols])
        pre.append((uw[:, :LANES], uw[:, LANES:], attn, qe[rows, cols], k_dec, jnp.exp(g_last)))

    for c in range(tl // C):
        o_pairs = []
        for p in range(N_PAIRS):
            u2, w2, attn, qe2, k_dec2, last_decay = pre[c * N_PAIRS + p]
            state = st_ref[p]
            sb = state.astype(BF16)
            v_new = u2 - jnp.dot(w2.astype(BF16), sb, preferred_element_type=F32)
            o_pairs.append(jnp.dot(qe2.astype(BF16), sb, preferred_element_type=F32) + _pair_mm(attn, v_new))
            st_ref[p] = state * last_decay + jnp.where(pair_diag, _bdot_tn(k_dec2, v_new), 0.0)
        o_ref[c * C:(c + 1) * C, :] = jnp.concatenate(o_pairs, axis=1)

    o = o_ref[...]
    o = o * lax.rsqrt(_bdot(o * o, same_head) * (1.0 / HEAD_DIM) + RMS_EPS) * nw_ref[...]
    o_ref[...] = o * _silu(g_ref[...])


def _gdn(proj3, conv_w, conv_b, dt_bias, a_log, norm_w):
    B, S, _ = proj3.shape
    tl = 256
    full = lambda shape: pl.BlockSpec(shape, lambda b, c: (0, 0))
    t = np.arange(tl)
    tri = (t[None, :] <= t[:, None]) & (t[:, None] // DELTA_CHUNK == t[None, :] // DELTA_CHUNK)
    head_of = np.arange(W_MIX) // HEAD_DIM
    same = head_of[:, None] == head_of[None, :]
    lanes = np.arange(LANES)[:, None]
    return pl.pallas_call(
        functools.partial(_gdn_kernel, tl=tl),
        out_shape=jax.ShapeDtypeStruct((B, S, W_MIX), F32),
        grid=(B, S // tl),
        in_specs=[pl.BlockSpec((None, tl, 3 * W_MIX), lambda b, c: (b, c, C_DN_QKV // (3 * W_MIX))),
                  pl.BlockSpec((None, tl, W_MIX), lambda b, c: (b, c, C_DN_G // W_MIX)),
                  pl.BlockSpec((None, tl, LANES), lambda b, c: (b, c, C_SMALL_B // LANES)),
                  full((CONV_WIDTH, 3 * W_MIX)), full((1, 3 * W_MIX)), full((1, LANES)), full((1, LANES)),
                  full((1, W_MIX)), full((W_MIX, W_MIX)), full((tl, tl)), full((LANES, W_MIX)), full((LANES, W_MIX))],
        out_specs=pl.BlockSpec((None, tl, W_MIX), lambda b, c: (b, c, 0)),
        scratch_shapes=[pltpu.VMEM((TAIL, 3 * W_MIX), F32), pltpu.VMEM((N_PAIRS, LANES, LANES), F32)],
        compiler_params=pltpu.CompilerParams(
            dimension_semantics=("parallel", "arbitrary"), vmem_limit_bytes=VMEM_LIMIT),
        name="gdn",
    )(proj3, proj3, proj3, conv_w, conv_b.reshape(1, -1),
      _pad_lanes(dt_bias, SM_DECAY), _pad_lanes(a_log, SM_DECAY), jnp.tile(norm_w.astype(F32), N_HEADS).reshape(1, -1),
      jnp.asarray(same, BF16), jnp.asarray(tri, BF16),
      jnp.asarray(head_of[None, :] + SM_BETA == lanes, BF16), jnp.asarray(head_of[None, :] + SM_DECAY == lanes, BF16))


def _outproj_kernel(x_ref, ya_ref, yb_ref, yc_ref, yd_ref, w_ref, lg_ref, lb_ref, o_ref):
    mix = (_bdot(ya_ref[...], w_ref[0]) + _bdot(yb_ref[...], w_ref[1])
           + _bdot(yc_ref[...], w_ref[2]) + _bdot(yd_ref[...], w_ref[3]))
    r = DEEPNORM_ALPHA * x_ref[...] + mix
    mu = jnp.mean(r, axis=1, keepdims=True)
    var = jnp.mean(jnp.square(r - mu), axis=1, keepdims=True)
    o_ref[...] = (r - mu) * lax.rsqrt(var + LN_EPS) * lg_ref[...] + lb_ref[...]


def _outproj(x2d, ya, yb, yc, yd, w_out, ln_g, ln_b):
    T = x2d.shape[0]
    tm = min(512, T)
    rowblk = lambda width: pl.BlockSpec((tm, width), lambda i: (i, 0))
    return pl.pallas_call(
        _outproj_kernel,
        out_shape=jax.ShapeDtypeStruct((T, D_MODEL), F32),
        grid=(T // tm,),
        in_specs=[rowblk(D_MODEL), rowblk(W_MIX), rowblk(W_MIX), rowblk(W_MIX), rowblk(W_MIX),
                  pl.BlockSpec((4, W_MIX, D_MODEL), lambda i: (0, 0, 0)),
                  pl.BlockSpec((1, D_MODEL), lambda i: (0, 0)), pl.BlockSpec((1, D_MODEL), lambda i: (0, 0))],
        out_specs=rowblk(D_MODEL),
        compiler_params=pltpu.CompilerParams(
            dimension_semantics=("parallel",), vmem_limit_bytes=VMEM_LIMIT),
        name="outproj",
    )(x2d, ya, yb, yc, yd, w_out, ln_g.reshape(1, -1), ln_b.reshape(1, -1))


def _alibi_slopes():
    n = 2 * N_HEADS
    s = (2.0 ** (-8.0 * (np.arange(n) + 1) / n)).astype(np.float32)
    return jnp.asarray(s[:N_HEADS]), jnp.asarray(s[N_HEADS:])


def _layer(x, w_in_t, w_out, ssm_conv_w, ssm_conv_b, ssm_dt_bias, ssm_A_log, ssm_D, ssm_norm_w,
           dn_conv_w, dn_conv_b, dn_dt_bias, dn_A_log, dn_norm_w, ln_g, ln_b):
    B, S, _ = x.shape
    T = B * S
    slopes_dil, slopes_moba = _alibi_slopes()
    x2d = x.reshape(T, D_MODEL)
    proj2d = _inproj(x2d, w_in_t)
    proj3 = proj2d.reshape(B, S, PROJ_COLS)

    ya = _moba(proj3, slopes_moba)
    yb = _ssd(proj3, ssm_conv_w, ssm_conv_b, ssm_dt_bias, ssm_A_log, ssm_D, ssm_norm_w)
    yd = _gdn(proj3, dn_conv_w, dn_conv_b, dn_dt_bias, dn_A_log, dn_norm_w)

    yc = _dilated(proj3, slopes_dil)

    out = _outproj(x2d, ya.reshape(T, W_MIX), yb.reshape(T, W_MIX), yc.reshape(T, W_MIX), yd.reshape(T, W_MIX),
                   w_out.reshape(4, W_MIX, D_MODEL).astype(BF16), ln_g, ln_b)
    return out.reshape(B, S, D_MODEL)


def kernel(x, w_in, w_out, ssm_conv_w, ssm_conv_b, ssm_dt_bias, ssm_A_log, ssm_D, ssm_norm_w,
           dn_conv_w, dn_conv_b, dn_dt_bias, dn_A_log, dn_norm_w, ln_g, ln_b):
    assert x.shape[1] % 2048 == 0 and x.shape[2] == D_MODEL
    w_t = jnp.transpose(w_in, (2, 0, 1))
    for l in range(DEPTH):
        x = _layer(x, _repack_w_in(w_t, l), w_out[l], ssm_conv_w[l], ssm_conv_b[l], ssm_dt_bias[l], ssm_A_log[l],
                   ssm_D[l], ssm_norm_w[l], dn_conv_w[l], dn_conv_b[l], dn_dt_bias[l], dn_A_log[l],
                   dn_norm_w[l], ln_g[l], ln_b[l])
    return x
```

```python
import functools
import math

import numpy as np
import jax
import jax.numpy as jnp
from jax import lax
from jax.experimental import pallas as pl
from jax.experimental.pallas import tpu as pltpu

F32 = jnp.float32
BF16 = jnp.bfloat16
HIGHEST = lax.Precision.HIGHEST

D_MODEL = 1024
DEPTH = 2
HEAD_DIM = 64
N_HEADS = 6
W_MIX = N_HEADS * HEAD_DIM
N_PAIRS = W_MIX // 128
D_MIX = 4 * W_MIX
MOBA_BLOCK = 256
MOBA_TOPK = 3
MOBA_NFEAT = 3
MOBA_UNROLL = 2
MOBA_LROWS = 16
SSD_STATE = 128
SSD_GROUPS = 2
SSD_CHUNK = 128
SSD_TILE = 512
SSD_XBC = W_MIX + 2 * SSD_GROUPS * SSD_STATE
CONV_WIDTH = 4
DIL_DILATIONS = (1, 4, 16)
DIL_GROUPS = len(DIL_DILATIONS)
DIL_BATCH = 16
DIL_MERGE_ROWS = 256
DIL_SPAN = 128
DELTA_CHUNK = 64
LN_EPS = 1e-5
RMS_EPS = 1e-6
DEEPNORM_ALPHA = (2.0 * DEPTH) ** 0.25
ATTN_SCALE = HEAD_DIM ** -0.5
LOG2E = math.log2(math.e)
NEG = -1e30

LANES = 128
TAIL = 8
VMEM_LIMIT = 48 * 1024 * 1024

C_MOBA = 0
C_SSD_X = 1536
C_SSD_Z = 1920
C_DN_QKV = 2304
C_DN_G = 3456
C_DIL = 3840
C_SSD_B = 5376
C_SSD_C = 5632
C_SMALL_A = 5888
C_SMALL_B = 6016
PROJ_COLS = 6144
SM_DT, SM_BETA, SM_DECAY = 0, 116, 122

_R = [int(v) for v in np.cumsum([0, 384, 384, 384, 384, SSD_XBC, 384, 6, 384, 384, 384, 384, 1152, 384, 6, 6])]
R_SSD_XBC, R_SSD_Z, R_SSD_DT, R_DIL, R_DN_QKV, R_DN_G, R_END = _R[4], _R[5], _R[6], _R[7], _R[11], _R[12], _R[15]
_SEGMENTS = ((C_MOBA, 0, 4 * W_MIX), (C_SSD_X, R_SSD_XBC, W_MIX), (C_SSD_Z, R_SSD_Z, W_MIX),
             (C_DN_QKV, R_DN_QKV, 3 * W_MIX), (C_DN_G, R_DN_G, W_MIX), (C_DIL, R_DIL, 4 * W_MIX),
             (C_SSD_B, R_SSD_XBC + W_MIX, 2 * SSD_GROUPS * SSD_STATE), (C_SMALL_A, R_SSD_DT, LANES),
             (C_SMALL_B, R_END - LANES, LANES))
_BLOCK_SRC = np.zeros(PROJ_COLS // LANES, np.int32)
for _dst, _src, _n in _SEGMENTS:
    _BLOCK_SRC[_dst // LANES:(_dst + _n) // LANES] = _src + LANES * np.arange(_n // LANES)
assert R_SSD_DT + LANES <= R_END


def _repack_kernel(src_ref, w_ref, o_ref, *, layer):
    o_ref[...] = w_ref[:, layer, :].astype(BF16)


def _repack_w_in(w_t, layer):
    _, depth, dm = w_t.shape
    return pl.pallas_call(
        functools.partial(_repack_kernel, layer=layer),
        out_shape=jax.ShapeDtypeStruct((PROJ_COLS, dm), BF16),
        grid_spec=pltpu.PrefetchScalarGridSpec(
            num_scalar_prefetch=1, grid=(PROJ_COLS // LANES,),
            in_specs=[pl.BlockSpec((pl.Element(LANES), pl.Element(depth), pl.Element(dm)),
                                   lambda b, src: (src[b], 0, 0))],
            out_specs=pl.BlockSpec((LANES, dm), lambda b, src: (b, 0))),
        compiler_params=pltpu.CompilerParams(dimension_semantics=("parallel",), vmem_limit_bytes=VMEM_LIMIT),
        name="repack_w_in",
    )(jnp.asarray(_BLOCK_SRC), w_t)


def _silu(x):
    return x * jax.nn.sigmoid(x)


def _softplus(x):
    return jnp.maximum(x, 0.0) + jnp.log(1.0 + jnp.exp(-jnp.abs(x)))


def _bdot(a, b):
    return jnp.dot(a.astype(BF16), b.astype(BF16), preferred_element_type=F32)


def _bdot_nt(a, b):
    return lax.dot_general(a.astype(BF16), b.astype(BF16), (((1,), (1,)), ((), ())),
                           preferred_element_type=F32)


def _bdot_tn(a, b):
    return lax.dot_general(a.astype(BF16), b.astype(BF16), (((0,), (0,)), ((), ())),
                           preferred_element_type=F32)


def _fdot(a, b):
    return jnp.dot(a, b, precision=HIGHEST, preferred_element_type=F32)


def _causal_conv_silu(x_ref, tail, w, b):
    L = x_ref.shape[0]
    head = jnp.concatenate([tail, x_ref[:TAIL, :]], axis=0)
    y_head, y_rest = b, b
    for j in range(CONV_WIDTH):
        off = TAIL - (CONV_WIDTH - 1) + j
        y_head = y_head + w[j:j + 1, :] * head[off:off + TAIL, :]
        y_rest = y_rest + w[j:j + 1, :] * x_ref[pl.ds(off, L - TAIL), :]
    return _silu(jnp.concatenate([y_head, y_rest], axis=0))


def _inproj_kernel(x_ref, w_ref, o_ref, xb_ref):
    @pl.when(pl.program_id(1) == 0)
    def _():
        xb_ref[...] = x_ref[...].astype(BF16)

    o_ref[...] = lax.dot_general(xb_ref[...], w_ref[...], (((1,), (1,)), ((), ())), preferred_element_type=F32)


def _inproj(x2d, w_t):
    T = x2d.shape[0]
    tm = min(1024, T)
    tn = 1536
    return pl.pallas_call(
        _inproj_kernel,
        out_shape=jax.ShapeDtypeStruct((T, PROJ_COLS), F32),
        grid=(T // tm, PROJ_COLS // tn),
        in_specs=[pl.BlockSpec((tm, D_MODEL), lambda i, j: (i, 0)),
                  pl.BlockSpec((tn, D_MODEL), lambda i, j: (j, 0))],
        out_specs=pl.BlockSpec((tm, tn), lambda i, j: (i, j)),
        scratch_shapes=[pltpu.VMEM((tm, D_MODEL), BF16)],
        compiler_params=pltpu.CompilerParams(
            dimension_semantics=("parallel", "arbitrary"), vmem_limit_bytes=VMEM_LIMIT),
        name="inproj",
    )(x2d, w_t)


def _moba_kernel(slopes_ref, q_ref, k_ref, v_ref, g_ref, o_ref, kaug_ref, vt_ref, kmean_ref, s_ref, *, nblk):
    p = pl.program_id(1)
    i = pl.program_id(2)
    L = MOBA_BLOCK

    NF = MOBA_NFEAT

    @pl.when(i == 0)
    def _():
        lane = lax.broadcasted_iota(jnp.int32, (L, LANES), 1)
        feat_lane = lane % HEAD_DIM
        c_loc = lax.broadcasted_iota(jnp.int32, (L, LANES), 0).astype(F32)
        feat = jnp.where(feat_lane < NF, 1.0, jnp.where(feat_lane < 2 * NF, c_loc, 0.0))
        ones = jnp.ones((MOBA_LROWS, L), F32)
        for n in range(nblk):
            kblk = k_ref[n * L:(n + 1) * L, :]
            kmean_ref[n:n + 1, :] = jnp.mean(kblk, axis=0, keepdims=True)
            kaug_ref[0, n] = jnp.where(lane < HEAD_DIM, kblk, feat).astype(BF16)
            kaug_ref[1, n] = jnp.where(lane < HEAD_DIM, feat, kblk).astype(BF16)
            vt = v_ref[n * L:(n + 1) * L, :].T
            for h in range(2):
                vt_ref[h, n] = jnp.concatenate([ones, vt[h * HEAD_DIM:(h + 1) * HEAD_DIM, :]], axis=0).astype(BF16)

    qt = q_ref[...].T
    qrow = lax.broadcasted_iota(jnp.int32, (LANES, L), 0)
    r_loc = lax.broadcasted_iota(jnp.int32, (1, L), 1).astype(F32)
    brow = lax.broadcasted_iota(jnp.int32, (nblk, L), 0).astype(F32)
    frow = lax.broadcasted_iota(jnp.int32, (2 * TAIL, L), 0)
    kmean = kmean_ref[...]
    i_f = i.astype(F32)

    qparts, slope_rows, zeros48 = [], [], jnp.zeros((HEAD_DIM - 2 * TAIL, L), BF16)
    for h in range(2):
        qparts.append((qt[h * HEAD_DIM:(h + 1) * HEAD_DIM, :] * (ATTN_SCALE * LOG2E)).astype(BF16))
        slope_rows.append(jnp.full((1, L), slopes_ref[2 * p + h] * LOG2E, F32))

    def q_aug(h, t_row):
        terms = _split3(t_row) + _split3(slope_rows[h])
        feat = jnp.zeros((2 * TAIL, L), F32)
        for f, term in enumerate(terms):
            feat = jnp.where(frow == f, term.astype(F32), feat)
        feat = feat.astype(BF16)
        if h == 0:
            return jnp.concatenate([qparts[0], feat, zeros48], axis=0)
        return jnp.concatenate([feat, zeros48, qparts[1]], axis=0)

    U = MOBA_UNROLL

    def put_scores(slot, n0):
        for u in range(U):
            n = n0 + u
            n_f = n.astype(F32)
            dist0 = r_loc + L * (i_f - n_f)
            for h in range(2):
                picked = (sels[h][0] == n_f) | (sels[h][1] == n_f) | (sels[h][2] == n_f)
                qa = q_aug(h, jnp.where(picked, -slope_rows[h] * dist0, NEG))
                s_ref[slot, h, u] = jnp.dot(kaug_ref[h, jnp.minimum(n, nblk - 1)], qa,
                                            preferred_element_type=F32).astype(BF16)

    def new_max(slot, carry):
        return [functools.reduce(jnp.maximum, [carry[2 * h]] + [
            jnp.max(s_ref[slot, h, u], axis=0, keepdims=True).astype(F32) for u in range(U)]) for h in range(2)]

    def absorb(slot, n0, m_new, carry):
        ps = [jnp.concatenate([jnp.exp2(s_ref[slot, h, u] - m_new[h].astype(BF16)) for u in range(U)], axis=0)
              for h in range(2)]
        vts = [jnp.concatenate([vt_ref[h, jnp.minimum(n0 + u, nblk - 1)] for u in range(U)], axis=1) for h in range(2)]
        pv = [jnp.dot(vts[h], ps[h], preferred_element_type=F32) for h in range(2)]
        out = []
        for h in range(2):
            out += [m_new[h], jnp.exp2(carry[2 * h] - m_new[h]) * carry[2 * h + 1] + pv[h]]
        return tuple(out)

    c_idx = lax.broadcasted_iota(jnp.int32, (L, L), 0)
    r_idx = lax.broadcasted_iota(jnp.int32, (L, L), 1)
    own = [jnp.where(c_idx <= r_idx,
                     jnp.dot(kaug_ref[h, i], q_aug(h, -slope_rows[h] * r_loc), preferred_element_type=F32),
                     NEG).astype(BF16) for h in range(2)]

    gates = [jnp.where(brow < i_f,
                       jnp.dot(kmean, jnp.where((qrow >= HEAD_DIM * h) & (qrow < HEAD_DIM * (h + 1)), qt, 0.0),
                               precision=HIGHEST, preferred_element_type=F32), -jnp.inf) for h in range(2)]
    sels = [[], []]
    for kk in range(MOBA_TOPK):
        for h in range(2):
            mx = jnp.max(gates[h], axis=0, keepdims=True)
            idx = jnp.min(jnp.where(gates[h] == mx, brow, float(nblk)), axis=0, keepdims=True)
            sels[h].append(jnp.where(kk < i, idx, -1.0))
            gates[h] = jnp.where(brow == idx, -jnp.inf, gates[h])

    m0 = [jnp.max(s, axis=0, keepdims=True) for s in own]
    p0 = [jnp.exp2(s - m) for s, m in zip(own, m0)]
    init = []
    for h in range(2):
        init += [m0[h].astype(F32), jnp.dot(vt_ref[h, i], p0[h], preferred_element_type=F32)]

    put_scores(0, jnp.int32(0))

    def body(j, carry):
        n0 = 2 * U * j
        m_a = new_max(0, carry)
        put_scores(1, n0 + U)
        carry = absorb(0, n0, m_a, carry)
        m_b = new_max(1, carry)
        put_scores(0, n0 + 2 * U)
        return absorb(1, n0 + U, m_b, carry)

    res = lax.fori_loop(0, (i + 2 * U - 1) // (2 * U), body, tuple(init))
    out_t = jnp.concatenate([res[2 * h + 1][MOBA_LROWS:] / res[2 * h + 1][0:1] for h in range(2)], axis=0)
    o_ref[...] = out_t.T * _silu(g_ref[...])


def _moba(proj3, slopes):
    B, S, _ = proj3.shape
    L = MOBA_BLOCK
    nblk = S // L
    cb = C_MOBA // LANES
    return pl.pallas_call(
        functools.partial(_moba_kernel, nblk=nblk),
        out_shape=jax.ShapeDtypeStruct((B, S, W_MIX), F32),
        grid=(B, N_PAIRS, nblk),
        in_specs=[pl.BlockSpec(memory_space=pltpu.SMEM),
                  pl.BlockSpec((None, L, LANES), lambda b, p, i: (b, i, cb + p)),
                  pl.BlockSpec((None, S, LANES), lambda b, p, i: (b, 0, cb + N_PAIRS + p)),
                  pl.BlockSpec((None, S, LANES), lambda b, p, i: (b, 0, cb + 2 * N_PAIRS + p)),
                  pl.BlockSpec((None, L, LANES), lambda b, p, i: (b, i, cb + 3 * N_PAIRS + p))],
        out_specs=pl.BlockSpec((None, L, LANES), lambda b, p, i: (b, i, p)),
        scratch_shapes=[pltpu.VMEM((2, nblk, L, LANES), BF16),
                        pltpu.VMEM((2, nblk, MOBA_LROWS + HEAD_DIM, L), BF16),
                        pltpu.VMEM((nblk, LANES), F32),
                        pltpu.VMEM((2, 2, MOBA_UNROLL, L, L), BF16)],
        compiler_params=pltpu.CompilerParams(
            dimension_semantics=("parallel", "parallel", "arbitrary"), vmem_limit_bytes=VMEM_LIMIT),
        name="moba",
    )(slopes, proj3, proj3, proj3, proj3)


def _dil_kernel(slopes_ref, q_ref, kc_ref, kp_ref, vc_ref, vp_ref, g_ref, o_ref, acc_ref, m_ref, l_ref, *, tq):
    p = pl.program_id(1)
    first = pl.program_id(2) == 0
    SB = DIL_SPAN
    lane = lax.broadcasted_iota(jnp.int32, (1, LANES), 1)
    lo = lane < HEAD_DIM
    row = lax.broadcasted_iota(jnp.int32, (SB, 2 * SB), 0)
    col = lax.broadcasted_iota(jnp.int32, (SB, 2 * SB), 1)
    dist = (row + SB - col).astype(F32)
    band = (col >= row) & (col <= row + SB)
    no_halo = jnp.logical_and(first, col < SB)

    def rows(start, d):
        return pl.ds(start, SB) if d == 1 else pl.ds(start, SB, stride=d)

    for gi, d in enumerate(DIL_DILATIONS):
        span = d * SB
        bias, bias_first = [], []
        for h in range(2):
            slope = slopes_ref[2 * p + h] * (float(d) * LOG2E)
            bias.append(jnp.where(band, -slope * dist, NEG))
            bias_first.append(jnp.where(no_halo, NEG, bias[h]))
        problems = [(r, j) for j in range(tq // span) for r in range(d)]
        for c0 in range(0, len(problems), DIL_BATCH):
            chunk = problems[c0:c0 + DIL_BATCH]
            kv, qs = [], []
            for r, j in chunk:
                cur = rows(r + span * j, d)
                if j == 0:
                    prev = rows(tq - span + r, d)
                    kp, vp = kp_ref[prev, :], vp_ref[prev, :]
                else:
                    prev = rows(r + span * (j - 1), d)
                    kp, vp = kc_ref[prev, :], vc_ref[prev, :]
                v2 = jnp.concatenate([vp, vc_ref[cur, :]], axis=0)
                kv.append((jnp.concatenate([kp, kc_ref[cur, :]], axis=0).astype(BF16),
                           (jnp.where(lo, v2, 1.0).astype(BF16), jnp.where(lo, 1.0, v2).astype(BF16))))
                qs.append(q_ref[cur, :] * (ATTN_SCALE * LOG2E))
            work = [(c, h) for c in range(len(chunk)) for h in range(2)]
            ss = [(_bdot_nt(jnp.where(lo if h == 0 else jnp.logical_not(lo), qs[c], 0.0), kv[c][0])
                   + (bias_first[h] if chunk[c][1] == 0 else bias[h])).astype(BF16) for c, h in work]
            ms = [jnp.max(s, axis=1, keepdims=True) for s in ss]
            ps = [jnp.exp2(s - m) for s, m in zip(ss, ms)]
            pvs = [jnp.dot(e, kv[c][1][h], preferred_element_type=F32) for e, (c, h) in zip(ps, work)]
            for c, (r, j) in enumerate(chunk):
                cur = rows(r + span * j, d)
                acc_ref[gi, cur, :] = jnp.where(lo, pvs[2 * c], pvs[2 * c + 1])
                m_ref[gi, cur, :] = jnp.where(lo, ms[2 * c].astype(F32), ms[2 * c + 1].astype(F32))
                l_ref[gi, cur, :] = pltpu.roll(jnp.where(lo, pvs[2 * c + 1], pvs[2 * c]), HEAD_DIM, axis=1)

    for c0 in range(0, tq, DIL_MERGE_ROWS):
        rs = slice(c0, c0 + DIL_MERGE_ROWS)
        ms = [m_ref[gi, rs, :] for gi in range(DIL_GROUPS)]
        top = functools.reduce(jnp.maximum, ms)
        ws = [jnp.exp2(m - top) for m in ms]
        num = sum(w * acc_ref[gi, rs, :] for gi, w in enumerate(ws))
        den = sum(w * l_ref[gi, rs, :] for gi, w in enumerate(ws))
        o_ref[rs, :] = num / den * _silu(g_ref[rs, :])


def _dilated(proj3, slopes):
    B, S, _ = proj3.shape
    tq = DIL_DILATIONS[-1] * DIL_SPAN
    base = C_DIL // LANES
    cur = lambda off: pl.BlockSpec((None, tq, LANES), lambda b, p, t: (b, t, base + off + p))
    prev = lambda off: pl.BlockSpec((None, tq, LANES), lambda b, p, t: (b, jnp.maximum(t - 1, 0), base + off + p))
    scratch = pltpu.VMEM((DIL_GROUPS, tq, LANES), F32)
    return pl.pallas_call(
        functools.partial(_dil_kernel, tq=tq),
        out_shape=jax.ShapeDtypeStruct((B, S, W_MIX), F32),
        grid=(B, N_PAIRS, S // tq),
        in_specs=[pl.BlockSpec(memory_space=pltpu.SMEM),
                  cur(0), cur(N_PAIRS), prev(N_PAIRS), cur(2 * N_PAIRS), prev(2 * N_PAIRS), cur(3 * N_PAIRS)],
        out_specs=pl.BlockSpec((None, tq, LANES), lambda b, p, t: (b, t, p)),
        scratch_shapes=[scratch, scratch, scratch],
        compiler_params=pltpu.CompilerParams(
            dimension_semantics=("parallel", "parallel", "arbitrary"), vmem_limit_bytes=VMEM_LIMIT),
        name="dilated",
    )(slopes, proj3, proj3, proj3, proj3, proj3, proj3)


def _ssd_kernel(x_ref, z_ref, b_ref, c_ref, sm_ref, cw_ref, cb_ref, dtb_ref, alog_ref, dskip_ref, nw_ref,
                tri_ref, expand_ref, o_ref, tx_ref, tb_ref, tc_ref, st_ref, *, tl):
    L = SSD_CHUNK
    N = SSD_STATE
    nchunk = tl // L

    @pl.when(pl.program_id(1) == 0)
    def _():
        tx_ref[...] = jnp.zeros_like(tx_ref)
        tb_ref[...] = jnp.zeros_like(tb_ref)
        tc_ref[...] = jnp.zeros_like(tc_ref)
        st_ref[...] = jnp.zeros_like(st_ref)

    cw = cw_ref[...]
    cb = cb_ref[...]
    xs = _causal_conv_silu(x_ref, tx_ref[...], cw[:, :W_MIX], cb[:, :W_MIX])
    Bm = _causal_conv_silu(b_ref, tb_ref[...], cw[:, W_MIX:W_MIX + 2 * N], cb[:, W_MIX:W_MIX + 2 * N])
    Cm = _causal_conv_silu(c_ref, tc_ref[...], cw[:, W_MIX + 2 * N:], cb[:, W_MIX + 2 * N:])
    tx_ref[...] = x_ref[tl - TAIL:, :]
    tb_ref[...] = b_ref[tl - TAIL:, :]
    tc_ref[...] = c_ref[tl - TAIL:, :]

    dt = _softplus(sm_ref[...] + dtb_ref[...])
    a = dt * (-jnp.exp(alog_ref[...]))
    a_cum = _xdot(tri_ref[...], a)
    expand = expand_ref[...]
    dt_w = _xdot_r(dt, expand)
    acum_w = _xdot_r(a_cum, expand)
    xdt = xs * dt_w
    a_cum_t = a_cum.T
    e_acum = jnp.exp(acum_w)

    lane = lax.broadcasted_iota(jnp.int32, (1, LANES), 1)
    lane_w = lax.broadcasted_iota(jnp.int32, (1, W_MIX), 1)
    in_g0 = lane_w < W_MIX // SSD_GROUPS
    row = lax.broadcasted_iota(jnp.int32, (L, L), 0)
    col = lax.broadcasted_iota(jnp.int32, (L, L), 1)
    causal = col <= row
    chunks = [slice(c * L, (c + 1) * L) for c in range(nchunk)]
    heads_per_group = N_HEADS // SSD_GROUPS

    scores = [[_bdot_nt(Cm[rs, g * N:(g + 1) * N], Bm[rs, g * N:(g + 1) * N]) for g in range(SSD_GROUPS)]
              for rs in chunks]
    atot = [acum_w[rs.stop - 1:rs.stop, :] for rs in chunks]
    xdec = [xdt[rs, :] * jnp.exp(atot[c] - acum_w[rs, :]) for c, rs in enumerate(chunks)]
    contrib = [jnp.where(in_g0, _bdot_tn(Bm[rs, :N], xdec[c]), _bdot_tn(Bm[rs, N:], xdec[c]))
               for c, rs in enumerate(chunks)]
    lmats = [[jnp.exp(jnp.where(causal, a_cum[rs, h:h + 1] - a_cum_t[h:h + 1, rs], NEG)) for h in range(N_HEADS)]
             for rs in chunks]
    diag = [[_bdot(scores[c][h // heads_per_group] * lmats[c][h], xdt[rs, (h // 2) * LANES:(h // 2 + 1) * LANES])
             for h in range(N_HEADS)] for c, rs in enumerate(chunks)]

    states = [st_ref[...]]
    for c in range(nchunk):
        states.append(states[c] * jnp.exp(atot[c]) + contrib[c])
    st_ref[...] = states[nchunk]
    sb = [s.astype(BF16) for s in states[:nchunk]]
    off = [jnp.where(in_g0, jnp.dot(Cm[rs, :N].astype(BF16), sb[c], preferred_element_type=F32),
                     jnp.dot(Cm[rs, N:].astype(BF16), sb[c], preferred_element_type=F32))
           for c, rs in enumerate(chunks)]

    for c, rs in enumerate(chunks):
        y_diag = jnp.concatenate([jnp.where(lane < HEAD_DIM, diag[c][2 * p], diag[c][2 * p + 1])
                                  for p in range(N_PAIRS)], axis=1)
        y = y_diag + off[c] * e_acum[rs, :] + dskip_ref[...] * xs[rs, :]
        y = y * _silu(z_ref[rs, :])
        o_ref[rs, :] = y * lax.rsqrt(jnp.mean(y * y, axis=1, keepdims=True) + RMS_EPS) * nw_ref[...]


def _pad_lanes(v, offset):
    return jnp.zeros((1, LANES), F32).at[0, offset:offset + v.shape[0]].set(v.astype(F32))


def _ssd(proj3, conv_w, conv_b, dt_bias, a_log, d_skip, norm_w):
    B, S, _ = proj3.shape
    L = SSD_TILE
    N2 = 2 * SSD_STATE
    full = lambda shape: pl.BlockSpec(shape, lambda b, c: (0, 0))
    t = np.arange(L)
    tri = (t[None, :] <= t[:, None]) & (t[:, None] // SSD_CHUNK == t[None, :] // SSD_CHUNK)
    expand = np.arange(W_MIX)[None, :] // HEAD_DIM == np.arange(LANES)[:, None]
    return pl.pallas_call(
        functools.partial(_ssd_kernel, tl=L),
        out_shape=jax.ShapeDtypeStruct((B, S, W_MIX), F32),
        grid=(B, S // L),
        in_specs=[pl.BlockSpec((None, L, W_MIX), lambda b, c: (b, c, C_SSD_X // W_MIX)),
                  pl.BlockSpec((None, L, W_MIX), lambda b, c: (b, c, C_SSD_Z // W_MIX)),
                  pl.BlockSpec((None, L, N2), lambda b, c: (b, c, C_SSD_B // N2)),
                  pl.BlockSpec((None, L, N2), lambda b, c: (b, c, C_SSD_C // N2)),
                  pl.BlockSpec((None, L, LANES), lambda b, c: (b, c, C_SMALL_A // LANES)),
                  full((CONV_WIDTH, SSD_XBC)), full((1, SSD_XBC)), full((1, LANES)), full((1, LANES)),
                  full((1, W_MIX)), full((1, W_MIX)), full((L, L)), full((LANES, W_MIX))],
        out_specs=pl.BlockSpec((None, L, W_MIX), lambda b, c: (b, c, 0)),
        scratch_shapes=[pltpu.VMEM((TAIL, W_MIX), F32), pltpu.VMEM((TAIL, N2), F32),
                        pltpu.VMEM((TAIL, N2), F32), pltpu.VMEM((SSD_STATE, W_MIX), F32)],
        compiler_params=pltpu.CompilerParams(
            dimension_semantics=("parallel", "arbitrary"), vmem_limit_bytes=VMEM_LIMIT),
        name="ssd",
    )(proj3, proj3, proj3, proj3, proj3, conv_w, conv_b.reshape(1, -1),
      _pad_lanes(dt_bias, SM_DT), _pad_lanes(a_log, SM_DT),
      jnp.repeat(d_skip.astype(F32), HEAD_DIM).reshape(1, -1), norm_w.reshape(1, -1),
      jnp.asarray(tri, BF16), jnp.asarray(expand, BF16))


INV_LEAF = 16


def _pair_blockdiag(x):
    lo = lax.broadcasted_iota(jnp.int32, (1, LANES), 1) < HEAD_DIM
    xb = x.astype(BF16)
    zero = jnp.zeros_like(xb)
    return jnp.concatenate([jnp.where(lo, xb, zero), jnp.where(lo, zero, xb)], axis=0)


def _pair_mm(a, b):
    return jnp.dot(a.astype(BF16), _pair_blockdiag(b), preferred_element_type=F32)


def _pair_unit_lower_inverse(lms, eye, on_diag_block):
    mm = lambda xs, ys: [_pair_mm(x, y) for x, y in zip(xs, ys)]
    d = [jnp.where(on_diag_block, lm, 0.0) for lm in lms]
    off = [lm - x for lm, x in zip(lms, d)]
    d2 = mm(d, d)
    d4 = mm(d2, d2)
    d8 = mm(d4, d4)
    left = mm([eye - x for x in d], [eye + x for x in d2])
    right = mm([eye + x for x in d4], [eye + x for x in d8])
    dinv = mm(left, right)
    m = mm(dinv, off)
    m2 = mm(m, m)
    return mm(mm([eye - x for x in m], [eye + x for x in m2]), dinv)


def _split3(x):
    hi = x.astype(BF16)
    r = x - hi.astype(F32)
    mid = r.astype(BF16)
    lo = (r - mid.astype(F32)).astype(BF16)
    return hi, mid, lo


def _xdot(a01, x):
    a = a01.astype(BF16)
    return sum(jnp.dot(a, t, preferred_element_type=F32) for t in _split3(x))


def _xdot_r(x, b01):
    b = b01.astype(BF16)
    return sum(jnp.dot(t, b, preferred_element_type=F32) for t in _split3(x))


def _gdn_kernel(qkv_ref, g_ref, sm_ref, cw_ref, cb_ref, dtb_ref, alog_ref, nw_ref, same_ref, tri_ref, eb_ref, ed_ref,
                o_ref, tail_ref, st_ref, *, tl):
    C = DELTA_CHUNK

    @pl.when(pl.program_id(1) == 0)
    def _():
        tail_ref[...] = jnp.zeros_like(tail_ref)
        st_ref[...] = jnp.zeros_like(st_ref)

    qkv = _causal_conv_silu(qkv_ref, tail_ref[...], cw_ref[...], cb_ref[...])
    tail_ref[...] = qkv_ref[tl - TAIL:, :]
    q, k, v = qkv[:, :W_MIX], qkv[:, W_MIX:2 * W_MIX], qkv[:, 2 * W_MIX:]

    same_head = same_ref[...]
    q = q * lax.rsqrt(_bdot(q * q, same_head) + RMS_EPS) * ATTN_SCALE
    k = k * lax.rsqrt(_bdot(k * k, same_head) + RMS_EPS)

    sm = sm_ref[...]
    beta = jax.nn.sigmoid(sm)
    gdec = -jnp.exp(alog_ref[...]) * _softplus(sm + dtb_ref[...])
    gc = _xdot(tri_ref[...], gdec)
    beta_w = _xdot_r(beta, eb_ref[...])
    gc_w = _xdot_r(gc, ed_ref[...])

    kb = k * beta_w
    vb = v * beta_w
    egc = jnp.exp(gc_w)
    kbe = kb * egc
    qe = q * egc

    row = lax.broadcasted_iota(jnp.int32, (C, LANES), 0)
    col = lax.broadcasted_iota(jnp.int32, (C, LANES), 1) % HEAD_DIM
    eye = (row == col).astype(F32)
    incl = col <= row
    strict = col < row
    on_diag_block = (row // INV_LEAF) == (col // INV_LEAF)
    s_row = lax.broadcasted_iota(jnp.int32, (LANES, LANES), 0)
    s_col = lax.broadcasted_iota(jnp.int32, (LANES, LANES), 1)
    pair_diag = (s_row // HEAD_DIM) == (s_col // HEAD_DIM)

    tiles = [(slice(c * C, (c + 1) * C), slice(p * LANES, (p + 1) * LANES))
             for c in range(tl // C) for p in range(N_PAIRS)]
    grams, decays = [], []
    for rows, cols in tiles:
        gcol = gc_w[rows, cols]
        grow = jnp.sum(gcol * eye, axis=0, keepdims=True)
        decays.append(jnp.exp(jnp.where(incl, gcol - grow, NEG)))
        grams.append(lax.dot_general(jnp.concatenate([kb[rows, cols], q[rows, cols]], axis=0).astype(BF16),
                                     _pair_blockdiag(k[rows, cols]), (((1,), (1,)), ((), ())),
                                     preferred_element_type=F32))
    lms = [jnp.where(strict, g[:C] * d, 0.0) for g, d in zip(grams, decays)]
    attns = [jnp.where(incl, g[C:] * d, 0.0) for g, d in zip(grams, decays)]
    t_invs = _pair_unit_lower_inverse(lms, eye, on_diag_block)
    uws = [jnp.dot(t.astype(BF16),
                   jnp.concatenate([_pair_blockdiag(vb[rows, cols]), _pair_blockdiag(kbe[rows, cols])], axis=1),
                   preferred_element_type=F32) for t, (rows, cols) in zip(t_invs, tiles)]
    u2s = [uw[:, :LANES] for uw in uws]
    w2s = [uw[:, LANES:] for uw in uws]
    g_lasts = [gc_w[rows.stop - 1:rows.stop, cols] for rows, cols in tiles]
    k_decs = [k[rows, cols] * jnp.exp(gl - gc_w[rows, cols]) for (rows, cols), gl in zip(tiles, g_lasts)]
    kd_uw = [_bdot_tn(kd, jnp.concatenate([u2, w2], axis=1)) for kd, u2, w2 in zip(k_decs, u2s, w2s)]
    kd_u = [jnp.where(pair_diag, x[:, :LANES], 0.0) for x in kd_uw]
    kd_w = [jnp.where(pair_diag, x[:, LANES:], 0.0).astype(BF16) for x in kd_uw]

    states = [[st_ref[p] for p in range(N_PAIRS)]]
    for c in range(tl // C):
        nxt = []
        for p in range(N_PAIRS):
            t = c * N_PAIRS + p
            s_in = states[c][p]
            nxt.append(s_in * jnp.exp(g_lasts[t]) + kd_u[t]
                       - jnp.dot(kd_w[t], s_in.astype(BF16), preferred_element_type=F32))
        states.append(nxt)
    for p in range(N_PAIRS):
        st_ref[p] = states[tl // C][p]

    sbs = [states[t // N_PAIRS][t % N_PAIRS].astype(BF16) for t in range(len(tiles))]
    v_news = [u2 - jnp.dot(w2.astype(BF16), sb, preferred_element_type=F32) for u2, w2, sb in zip(u2s, w2s, sbs)]
    o2s = [jnp.dot(qe[rows, cols].astype(BF16), sb, preferred_element_type=F32) + _pair_mm(attn, v_new)
           for (rows, cols), sb, attn, v_new in zip(tiles, sbs, attns, v_news)]
    for c in range(tl // C):
        o_ref[c * C:(c + 1) * C, :] = jnp.concatenate(o2s[c * N_PAIRS:(c + 1) * N_PAIRS], axis=1)

    o = o_ref[...]
    o = o * lax.rsqrt(_bdot(o * o, same_head) * (1.0 / HEAD_DIM) + RMS_EPS) * nw_ref[...]
    o_ref[...] = o * _silu(g_ref[...])


def _gdn(proj3, conv_w, conv_b, dt_bias, a_log, norm_w):
    B, S, _ = proj3.shape
    tl = 512
    full = lambda shape: pl.BlockSpec(shape, lambda b, c: (0, 0))
    t = np.arange(tl)
    tri = (t[None, :] <= t[:, None]) & (t[:, None] // DELTA_CHUNK == t[None, :] // DELTA_CHUNK)
    head_of = np.arange(W_MIX) // HEAD_DIM
    same = head_of[:, None] == head_of[None, :]
    lanes = np.arange(LANES)[:, None]
    return pl.pallas_call(
        functools.partial(_gdn_kernel, tl=tl),
        out_shape=jax.ShapeDtypeStruct((B, S, W_MIX), F32),
        grid=(B, S // tl),
        in_specs=[pl.BlockSpec((None, tl, 3 * W_MIX), lambda b, c: (b, c, C_DN_QKV // (3 * W_MIX))),
                  pl.BlockSpec((None, tl, W_MIX), lambda b, c: (b, c, C_DN_G // W_MIX)),
                  pl.BlockSpec((None, tl, LANES), lambda b, c: (b, c, C_SMALL_B // LANES)),
                  full((CONV_WIDTH, 3 * W_MIX)), full((1, 3 * W_MIX)), full((1, LANES)), full((1, LANES)),
                  full((1, W_MIX)), full((W_MIX, W_MIX)), full((tl, tl)), full((LANES, W_MIX)), full((LANES, W_MIX))],
        out_specs=pl.BlockSpec((None, tl, W_MIX), lambda b, c: (b, c, 0)),
        scratch_shapes=[pltpu.VMEM((TAIL, 3 * W_MIX), F32), pltpu.VMEM((N_PAIRS, LANES, LANES), F32)],
        compiler_params=pltpu.CompilerParams(
            dimension_semantics=("parallel", "arbitrary"), vmem_limit_bytes=VMEM_LIMIT),
        name="gdn",
    )(proj3, proj3, proj3, conv_w, conv_b.reshape(1, -1),
      _pad_lanes(dt_bias, SM_DECAY), _pad_lanes(a_log, SM_DECAY), jnp.tile(norm_w.astype(F32), N_HEADS).reshape(1, -1),
      jnp.asarray(same, BF16), jnp.asarray(tri, BF16),
      jnp.asarray(head_of[None, :] + SM_BETA == lanes, BF16), jnp.asarray(head_of[None, :] + SM_DECAY == lanes, BF16))


def _outproj_kernel(x_ref, ya_ref, yb_ref, yc_ref, yd_ref, w_ref, lg_ref, lb_ref, o_ref):
    mix = (_bdot(ya_ref[...], w_ref[0]) + _bdot(yb_ref[...], w_ref[1])
           + _bdot(yc_ref[...], w_ref[2]) + _bdot(yd_ref[...], w_ref[3]))
    r = DEEPNORM_ALPHA * x_ref[...] + mix
    mu = jnp.mean(r, axis=1, keepdims=True)
    var = jnp.mean(jnp.square(r - mu), axis=1, keepdims=True)
    o_ref[...] = (r - mu) * lax.rsqrt(var + LN_EPS) * lg_ref[...] + lb_ref[...]


def _outproj(x2d, ya, yb, yc, yd, w_out, ln_g, ln_b):
    T = x2d.shape[0]
    tm = min(512, T)
    rowblk = lambda width: pl.BlockSpec((tm, width), lambda i: (i, 0))
    return pl.pallas_call(
        _outproj_kernel,
        out_shape=jax.ShapeDtypeStruct((T, D_MODEL), F32),
        grid=(T // tm,),
        in_specs=[rowblk(D_MODEL), rowblk(W_MIX), rowblk(W_MIX), rowblk(W_MIX), rowblk(W_MIX),
                  pl.BlockSpec((4, W_MIX, D_MODEL), lambda i: (0, 0, 0)),
                  pl.BlockSpec((1, D_MODEL), lambda i: (0, 0)), pl.BlockSpec((1, D_MODEL), lambda i: (0, 0))],
        out_specs=rowblk(D_MODEL),
        compiler_params=pltpu.CompilerParams(
            dimension_semantics=("parallel",), vmem_limit_bytes=VMEM_LIMIT),
        name="outproj",
    )(x2d, ya, yb, yc, yd, w_out, ln_g.reshape(1, -1), ln_b.reshape(1, -1))


def _alibi_slopes():
    n = 2 * N_HEADS
    s = (2.0 ** (-8.0 * (np.arange(n) + 1) / n)).astype(np.float32)
    return jnp.asarray(s[:N_HEADS]), jnp.asarray(s[N_HEADS:])


def _layer(x, w_in_t, w_out, ssm_conv_w, ssm_conv_b, ssm_dt_bias, ssm_A_log, ssm_D, ssm_norm_w,
           dn_conv_w, dn_conv_b, dn_dt_bias, dn_A_log, dn_norm_w, ln_g, ln_b):
    B, S, _ = x.shape
    T = B * S
    slopes_dil, slopes_moba = _alibi_slopes()
    x2d = x.reshape(T, D_MODEL)
    proj2d = _inproj(x2d, w_in_t)
    proj3 = proj2d.reshape(B, S, PROJ_COLS)

    ya = _moba(proj3, slopes_moba)
    yb = _ssd(proj3, ssm_conv_w, ssm_conv_b, ssm_dt_bias, ssm_A_log, ssm_D, ssm_norm_w)
    yd = _gdn(proj3, dn_conv_w, dn_conv_b, dn_dt_bias, dn_A_log, dn_norm_w)

    yc = _dilated(proj3, slopes_dil)

    out = _outproj(x2d, ya.reshape(T, W_MIX), yb.reshape(T, W_MIX), yc.reshape(T, W_MIX), yd.reshape(T, W_MIX),
                   w_out.reshape(4, W_MIX, D_MODEL).astype(BF16), ln_g, ln_b)
    return out.reshape(B, S, D_MODEL)


def kernel(x, w_in, w_out, ssm_conv_w, ssm_conv_b, ssm_dt_bias, ssm_A_log, ssm_D, ssm_norm_w,
           dn_conv_w, dn_conv_b, dn_dt_bias, dn_A_log, dn_norm_w, ln_g, ln_b):
    assert x.shape[1] % 2048 == 0 and x.shape[2] == D_MODEL
    w_t = jnp.transpose(w_in, (2, 0, 1))
    for l in range(DEPTH):
        x = _layer(x, _repack_w_in(w_t, l), w_out[l], ssm_conv_w[l], ssm_conv_b[l], ssm_dt_bias[l], ssm_A_log[l],
                   ssm_D[l], ssm_norm_w[l], dn_conv_w[l], dn_conv_b[l], dn_dt_bias[l], dn_A_log[l],
                   dn_norm_w[l], ln_g[l], ln_b[l])
    return x
```

```python
import functools
import math

import numpy as np
import jax
import jax.numpy as jnp
from jax import lax
from jax.experimental import pallas as pl
from jax.experimental.pallas import tpu as pltpu

F32 = jnp.float32
BF16 = jnp.bfloat16
HIGHEST = lax.Precision.HIGHEST

D_MODEL = 1024
DEPTH = 2
HEAD_DIM = 64
N_HEADS = 6
W_MIX = N_HEADS * HEAD_DIM
N_PAIRS = W_MIX // 128
D_MIX = 4 * W_MIX
MOBA_BLOCK = 256
MOBA_TOPK = 3
MOBA_NFEAT = 3
MOBA_UNROLL = 2
MOBA_QBLOCKS = 2
MOBA_LROWS = 16
SSD_STATE = 128
SSD_GROUPS = 2
SSD_CHUNK = 128
SSD_TILE = 512
SSD_XBC = W_MIX + 2 * SSD_GROUPS * SSD_STATE
CONV_WIDTH = 4
DIL_DILATIONS = (1, 4, 16)
DIL_GROUPS = len(DIL_DILATIONS)
DIL_BATCH = 16
DIL_MERGE_ROWS = 256
DIL_SPAN = 128
DELTA_CHUNK = 64
LN_EPS = 1e-5
RMS_EPS = 1e-6
DEEPNORM_ALPHA = (2.0 * DEPTH) ** 0.25
ATTN_SCALE = HEAD_DIM ** -0.5
LOG2E = math.log2(math.e)
NEG = -1e30

LANES = 128
TAIL = 8
VMEM_LIMIT = 48 * 1024 * 1024

C_MOBA = 0
C_SSD_X = 1536
C_SSD_Z = 1920
C_DN_QKV = 2304
C_DN_G = 3456
C_DIL = 3840
C_SSD_B = 5376
C_SSD_C = 5632
C_SMALL_A = 5888
C_SMALL_B = 6016
PROJ_COLS = 6144
SM_DT, SM_BETA, SM_DECAY = 0, 116, 122

_R = [int(v) for v in np.cumsum([0, 384, 384, 384, 384, SSD_XBC, 384, 6, 384, 384, 384, 384, 1152, 384, 6, 6])]
R_SSD_XBC, R_SSD_Z, R_SSD_DT, R_DIL, R_DN_QKV, R_DN_G, R_END = _R[4], _R[5], _R[6], _R[7], _R[11], _R[12], _R[15]
_SEGMENTS = ((C_MOBA, 0, 4 * W_MIX), (C_SSD_X, R_SSD_XBC, W_MIX), (C_SSD_Z, R_SSD_Z, W_MIX),
             (C_DN_QKV, R_DN_QKV, 3 * W_MIX), (C_DN_G, R_DN_G, W_MIX), (C_DIL, R_DIL, 4 * W_MIX),
             (C_SSD_B, R_SSD_XBC + W_MIX, 2 * SSD_GROUPS * SSD_STATE), (C_SMALL_A, R_SSD_DT, LANES),
             (C_SMALL_B, R_END - LANES, LANES))
_BLOCK_SRC = np.zeros(PROJ_COLS // LANES, np.int32)
for _dst, _src, _n in _SEGMENTS:
    _BLOCK_SRC[_dst // LANES:(_dst + _n) // LANES] = _src + LANES * np.arange(_n // LANES)
assert R_SSD_DT + LANES <= R_END


def _repack_kernel(src_ref, w_ref, o_ref):
    for layer in range(o_ref.shape[0]):
        o_ref[layer] = w_ref[:, layer, :].astype(BF16)


def _repack_w_in(w_t):
    _, depth, dm = w_t.shape
    return pl.pallas_call(
        _repack_kernel,
        out_shape=jax.ShapeDtypeStruct((depth, PROJ_COLS, dm), BF16),
        grid_spec=pltpu.PrefetchScalarGridSpec(
            num_scalar_prefetch=1, grid=(PROJ_COLS // LANES,),
            in_specs=[pl.BlockSpec((pl.Element(LANES), pl.Element(depth), pl.Element(dm)),
                                   lambda b, src: (src[b], 0, 0))],
            out_specs=pl.BlockSpec((depth, LANES, dm), lambda b, src: (0, b, 0))),
        compiler_params=pltpu.CompilerParams(dimension_semantics=("parallel",), vmem_limit_bytes=VMEM_LIMIT),
        name="repack_w_in",
    )(jnp.asarray(_BLOCK_SRC), w_t)


def _silu(x):
    return x * jax.nn.sigmoid(x)


def _softplus(x):
    return jnp.maximum(x, 0.0) + jnp.log(1.0 + jnp.exp(-jnp.abs(x)))


def _bdot(a, b):
    return jnp.dot(a.astype(BF16), b.astype(BF16), preferred_element_type=F32)


def _bdot_nt(a, b):
    return lax.dot_general(a.astype(BF16), b.astype(BF16), (((1,), (1,)), ((), ())),
                           preferred_element_type=F32)


def _bdot_tn(a, b):
    return lax.dot_general(a.astype(BF16), b.astype(BF16), (((0,), (0,)), ((), ())),
                           preferred_element_type=F32)


def _fdot(a, b):
    return jnp.dot(a, b, precision=HIGHEST, preferred_element_type=F32)


def _causal_conv_silu(x_ref, tail, w, b):
    L = x_ref.shape[0]
    head = jnp.concatenate([tail, x_ref[:TAIL, :]], axis=0)
    y_head, y_rest = b, b
    for j in range(CONV_WIDTH):
        off = TAIL - (CONV_WIDTH - 1) + j
        y_head = y_head + w[j:j + 1, :] * head[off:off + TAIL, :]
        y_rest = y_rest + w[j:j + 1, :] * x_ref[pl.ds(off, L - TAIL), :]
    return _silu(jnp.concatenate([y_head, y_rest], axis=0))


def _inproj_kernel(x_ref, w_ref, o_ref, xb_ref):
    @pl.when(pl.program_id(1) == 0)
    def _():
        xb_ref[...] = x_ref[...].astype(BF16)

    o_ref[...] = lax.dot_general(xb_ref[...], w_ref[...], (((1,), (1,)), ((), ())), preferred_element_type=F32)


def _inproj(x2d, w_t, layer):
    T = x2d.shape[0]
    tm = min(1024, T)
    tn = 1536
    return pl.pallas_call(
        _inproj_kernel,
        out_shape=jax.ShapeDtypeStruct((T, PROJ_COLS), F32),
        grid=(T // tm, PROJ_COLS // tn),
        in_specs=[pl.BlockSpec((tm, D_MODEL), lambda i, j: (i, 0)),
                  pl.BlockSpec((None, tn, D_MODEL), lambda i, j: (layer, j, 0))],
        out_specs=pl.BlockSpec((tm, tn), lambda i, j: (i, j)),
        scratch_shapes=[pltpu.VMEM((tm, D_MODEL), BF16)],
        compiler_params=pltpu.CompilerParams(
            dimension_semantics=("parallel", "arbitrary"), vmem_limit_bytes=VMEM_LIMIT),
        name="inproj",
    )(x2d, w_t)


def _moba_kernel(slopes_ref, q_ref, k_ref, v_ref, g_ref, o_ref, kaug_ref, vt_ref, kmean_ref, s_ref, *, nblk):
    p = pl.program_id(1)
    L = MOBA_BLOCK
    NF = MOBA_NFEAT

    @pl.when(pl.program_id(2) == 0)
    def _():
        lane = lax.broadcasted_iota(jnp.int32, (L, LANES), 1)
        feat_lane = lane % HEAD_DIM
        c_loc = lax.broadcasted_iota(jnp.int32, (L, LANES), 0).astype(F32)
        feat = jnp.where(feat_lane < NF, 1.0, jnp.where(feat_lane < 2 * NF, c_loc, 0.0))
        ones = jnp.ones((MOBA_LROWS, L), F32)
        for n in range(nblk):
            kblk = k_ref[n * L:(n + 1) * L, :]
            kmean_ref[n:n + 1, :] = jnp.mean(kblk, axis=0, keepdims=True)
            kaug_ref[0, n] = jnp.where(lane < HEAD_DIM, kblk, feat).astype(BF16)
            kaug_ref[1, n] = jnp.where(lane < HEAD_DIM, feat, kblk).astype(BF16)
            vt = v_ref[n * L:(n + 1) * L, :].T
            for h in range(2):
                vt_ref[h, n] = jnp.concatenate([ones, vt[h * HEAD_DIM:(h + 1) * HEAD_DIM, :]], axis=0).astype(BF16)

    QB = MOBA_QBLOCKS
    streams = [(qb, h) for qb in range(QB) for h in range(2)]
    i0 = QB * pl.program_id(2)
    qt_all = q_ref[...].T
    qt = [qt_all[:, qb * L:(qb + 1) * L] for qb in range(QB)]
    qrow = lax.broadcasted_iota(jnp.int32, (LANES, L), 0)
    r_loc = lax.broadcasted_iota(jnp.int32, (1, L), 1).astype(F32)
    brow = lax.broadcasted_iota(jnp.int32, (nblk, L), 0).astype(F32)
    frow = lax.broadcasted_iota(jnp.int32, (2 * TAIL, L), 0)
    kmean = kmean_ref[...]
    i_f = [(i0 + qb).astype(F32) for qb in range(QB)]

    qparts = [(qt[qb][h * HEAD_DIM:(h + 1) * HEAD_DIM, :] * (ATTN_SCALE * LOG2E)).astype(BF16) for qb, h in streams]
    slope_rows = [jnp.full((1, L), slopes_ref[2 * p + h] * LOG2E, F32) for h in range(2)]
    zeros48 = jnp.zeros((HEAD_DIM - 2 * TAIL, L), BF16)

    def q_aug(st, t_row):
        h = streams[st][1]
        terms = _split3(t_row) + _split3(slope_rows[h])
        feat = jnp.zeros((2 * TAIL, L), F32)
        for f, term in enumerate(terms):
            feat = jnp.where(frow == f, term.astype(F32), feat)
        feat = feat.astype(BF16)
        if h == 0:
            return jnp.concatenate([qparts[st], feat, zeros48], axis=0)
        return jnp.concatenate([feat, zeros48, qparts[st]], axis=0)

    def past_row(st, n, in_range=None):
        qb, h = streams[st]
        n_f = n.astype(F32)
        picked = (sels[st][0] == n_f) | (sels[st][1] == n_f) | (sels[st][2] == n_f)
        if in_range is not None:
            picked = picked & in_range
        return jnp.where(picked, -slope_rows[h] * (r_loc + L * (i_f[qb] - n_f)), NEG)

    U = MOBA_UNROLL
    NS = len(streams)

    def put_scores(slot, n0):
        for u in range(U):
            n = n0 + u
            for st, (qb, h) in enumerate(streams):
                s_ref[slot, st, u] = jnp.dot(kaug_ref[h, jnp.minimum(n, nblk - 1)], q_aug(st, past_row(st, n, n < i0)),
                                             preferred_element_type=F32).astype(BF16)

    def new_max(slot, carry):
        return [functools.reduce(jnp.maximum, [carry[2 * st]] + [
            jnp.max(s_ref[slot, st, u], axis=0, keepdims=True).astype(F32) for u in range(U)]) for st in range(NS)]

    def absorb(slot, n0, m_new, carry):
        ps = [jnp.concatenate([jnp.exp2(s_ref[slot, st, u] - m_new[st].astype(BF16)) for u in range(U)], axis=0)
              for st in range(NS)]
        vts = [jnp.concatenate([vt_ref[h, jnp.minimum(n0 + u, nblk - 1)] for u in range(U)], axis=1) for h in range(2)]
        pv = [jnp.dot(vts[streams[st][1]], ps[st], preferred_element_type=F32) for st in range(NS)]
        out = []
        for st in range(NS):
            out += [m_new[st], jnp.exp2(carry[2 * st] - m_new[st]) * carry[2 * st + 1] + pv[st]]
        return tuple(out)

    c_idx = lax.broadcasted_iota(jnp.int32, (L, L), 0)
    r_idx = lax.broadcasted_iota(jnp.int32, (L, L), 1)
    own = [jnp.where(c_idx <= r_idx,
                     jnp.dot(kaug_ref[h, i0 + qb], q_aug(st, -slope_rows[h] * r_loc), preferred_element_type=F32),
                     NEG).astype(BF16) for st, (qb, h) in enumerate(streams)]

    gates = [jnp.where(brow < i_f[qb],
                       jnp.dot(kmean, jnp.where((qrow >= HEAD_DIM * h) & (qrow < HEAD_DIM * (h + 1)), qt[qb], 0.0),
                               precision=HIGHEST, preferred_element_type=F32), -jnp.inf) for qb, h in streams]
    sels = [[] for _ in streams]
    for kk in range(MOBA_TOPK):
        for st, (qb, h) in enumerate(streams):
            mx = jnp.max(gates[st], axis=0, keepdims=True)
            idx = jnp.min(jnp.where(gates[st] == mx, brow, float(nblk)), axis=0, keepdims=True)
            sels[st].append(jnp.where(kk < i0 + qb, idx, -1.0))
            gates[st] = jnp.where(brow == idx, -jnp.inf, gates[st])

    init = []
    for st, (qb, h) in enumerate(streams):
        tiles = [jnp.dot(kaug_ref[h, i0 + e], q_aug(st, past_row(st, i0 + e)),
                         preferred_element_type=F32).astype(BF16) for e in range(qb)] + [own[st]]
        m0 = functools.reduce(jnp.maximum, [jnp.max(s, axis=0, keepdims=True) for s in tiles])
        p0 = jnp.concatenate([jnp.exp2(s - m0) for s in tiles], axis=0)
        vt0 = jnp.concatenate([vt_ref[h, i0 + e] for e in range(qb + 1)], axis=1)
        init += [m0.astype(F32), jnp.dot(vt0, p0, preferred_element_type=F32)]

    put_scores(0, jnp.int32(0))

    def body(j, carry):
        n0 = 2 * U * j
        m_a = new_max(0, carry)
        put_scores(1, n0 + U)
        carry = absorb(0, n0, m_a, carry)
        m_b = new_max(1, carry)
        put_scores(0, n0 + 2 * U)
        return absorb(1, n0 + U, m_b, carry)

    res = lax.fori_loop(0, (i0 + 2 * U - 1) // (2 * U), body, tuple(init))
    for qb in range(QB):
        out_t = jnp.concatenate([res[2 * st + 1][MOBA_LROWS:] / res[2 * st + 1][0:1]
                                 for st in (2 * qb, 2 * qb + 1)], axis=0)
        o_ref[qb * L:(qb + 1) * L, :] = out_t.T * _silu(g_ref[qb * L:(qb + 1) * L, :])


def _moba(proj3, slopes):
    B, S, _ = proj3.shape
    L = MOBA_BLOCK
    nblk = S // L
    QB = MOBA_QBLOCKS
    cb = C_MOBA // LANES
    return pl.pallas_call(
        functools.partial(_moba_kernel, nblk=nblk),
        out_shape=jax.ShapeDtypeStruct((B, S, W_MIX), F32),
        grid=(B, N_PAIRS, nblk // QB),
        in_specs=[pl.BlockSpec(memory_space=pltpu.SMEM),
                  pl.BlockSpec((None, QB * L, LANES), lambda b, p, i: (b, i, cb + p)),
                  pl.BlockSpec((None, S, LANES), lambda b, p, i: (b, 0, cb + N_PAIRS + p)),
                  pl.BlockSpec((None, S, LANES), lambda b, p, i: (b, 0, cb + 2 * N_PAIRS + p)),
                  pl.BlockSpec((None, QB * L, LANES), lambda b, p, i: (b, i, cb + 3 * N_PAIRS + p))],
        out_specs=pl.BlockSpec((None, QB * L, LANES), lambda b, p, i: (b, i, p)),
        scratch_shapes=[pltpu.VMEM((2, nblk, L, LANES), BF16),
                        pltpu.VMEM((2, nblk, MOBA_LROWS + HEAD_DIM, L), BF16),
                        pltpu.VMEM((nblk, LANES), F32),
                        pltpu.VMEM((2, 2 * QB, MOBA_UNROLL, L, L), BF16)],
        compiler_params=pltpu.CompilerParams(
            dimension_semantics=("parallel", "parallel", "arbitrary"), vmem_limit_bytes=VMEM_LIMIT),
        name="moba",
    )(slopes, proj3, proj3, proj3, proj3)


def _dil_kernel(slopes_ref, q_ref, kc_ref, kp_ref, vc_ref, vp_ref, g_ref, o_ref, acc_ref, m_ref, l_ref, *, tq):
    p = pl.program_id(1)
    first = pl.program_id(2) == 0
    SB = DIL_SPAN
    lane = lax.broadcasted_iota(jnp.int32, (1, LANES), 1)
    lo = lane < HEAD_DIM
    row = lax.broadcasted_iota(jnp.int32, (SB, 2 * SB), 0)
    col = lax.broadcasted_iota(jnp.int32, (SB, 2 * SB), 1)
    dist = (row + SB - col).astype(F32)
    band = (col >= row) & (col <= row + SB)
    no_halo = jnp.logical_and(first, col < SB)

    def rows(start, d):
        return pl.ds(start, SB) if d == 1 else pl.ds(start, SB, stride=d)

    for gi, d in enumerate(DIL_DILATIONS):
        span = d * SB
        bias, bias_first = [], []
        for h in range(2):
            slope = slopes_ref[2 * p + h] * (float(d) * LOG2E)
            bias.append(jnp.where(band, -slope * dist, NEG))
            bias_first.append(jnp.where(no_halo, NEG, bias[h]))
        problems = [(r, j) for j in range(tq // span) for r in range(d)]
        for c0 in range(0, len(problems), DIL_BATCH):
            chunk = problems[c0:c0 + DIL_BATCH]
            kv, qs = [], []
            for r, j in chunk:
                cur = rows(r + span * j, d)
                if j == 0:
                    prev = rows(tq - span + r, d)
                    kp, vp = kp_ref[prev, :], vp_ref[prev, :]
                else:
                    prev = rows(r + span * (j - 1), d)
                    kp, vp = kc_ref[prev, :], vc_ref[prev, :]
                v2 = jnp.concatenate([vp, vc_ref[cur, :]], axis=0)
                kv.append((jnp.concatenate([kp, kc_ref[cur, :]], axis=0).astype(BF16),
                           (jnp.where(lo, v2, 1.0).astype(BF16), jnp.where(lo, 1.0, v2).astype(BF16))))
                qs.append(q_ref[cur, :] * (ATTN_SCALE * LOG2E))
            work = [(c, h) for c in range(len(chunk)) for h in range(2)]
            ss = [(_bdot_nt(jnp.where(lo if h == 0 else jnp.logical_not(lo), qs[c], 0.0), kv[c][0])
                   + (bias_first[h] if chunk[c][1] == 0 else bias[h])).astype(BF16) for c, h in work]
            ms = [jnp.max(s, axis=1, keepdims=True) for s in ss]
            ps = [jnp.exp2(s - m) for s, m in zip(ss, ms)]
            pvs = [jnp.dot(e, kv[c][1][h], preferred_element_type=F32) for e, (c, h) in zip(ps, work)]
            for c, (r, j) in enumerate(chunk):
                cur = rows(r + span * j, d)
                acc_ref[gi, cur, :] = jnp.where(lo, pvs[2 * c], pvs[2 * c + 1])
                m_ref[gi, cur, :] = jnp.where(lo, ms[2 * c].astype(F32), ms[2 * c + 1].astype(F32))
                l_ref[gi, cur, :] = pltpu.roll(jnp.where(lo, pvs[2 * c + 1], pvs[2 * c]), HEAD_DIM, axis=1)

    for c0 in range(0, tq, DIL_MERGE_ROWS):
        rs = slice(c0, c0 + DIL_MERGE_ROWS)
        ms = [m_ref[gi, rs, :] for gi in range(DIL_GROUPS)]
        top = functools.reduce(jnp.maximum, ms)
        ws = [jnp.exp2(m - top) for m in ms]
        num = sum(w * acc_ref[gi, rs, :] for gi, w in enumerate(ws))
        den = sum(w * l_ref[gi, rs, :] for gi, w in enumerate(ws))
        o_ref[rs, :] = num / den * _silu(g_ref[rs, :])


def _dilated(proj3, slopes):
    B, S, _ = proj3.shape
    tq = DIL_DILATIONS[-1] * DIL_SPAN
    base = C_DIL // LANES
    cur = lambda off: pl.BlockSpec((None, tq, LANES), lambda b, p, t: (b, t, base + off + p))
    prev = lambda off: pl.BlockSpec((None, tq, LANES), lambda b, p, t: (b, jnp.maximum(t - 1, 0), base + off + p))
    scratch = pltpu.VMEM((DIL_GROUPS, tq, LANES), F32)
    return pl.pallas_call(
        functools.partial(_dil_kernel, tq=tq),
        out_shape=jax.ShapeDtypeStruct((B, S, W_MIX), F32),
        grid=(B, N_PAIRS, S // tq),
        in_specs=[pl.BlockSpec(memory_space=pltpu.SMEM),
                  cur(0), cur(N_PAIRS), prev(N_PAIRS), cur(2 * N_PAIRS), prev(2 * N_PAIRS), cur(3 * N_PAIRS)],
        out_specs=pl.BlockSpec((None, tq, LANES), lambda b, p, t: (b, t, p)),
        scratch_shapes=[scratch, scratch, scratch],
        compiler_params=pltpu.CompilerParams(
            dimension_semantics=("parallel", "parallel", "arbitrary"), vmem_limit_bytes=VMEM_LIMIT),
        name="dilated",
    )(slopes, proj3, proj3, proj3, proj3, proj3, proj3)


def _ssd_kernel(x_ref, z_ref, b_ref, c_ref, sm_ref, cw_ref, cb_ref, dtb_ref, alog_ref, dskip_ref, nw_ref,
                tri_ref, expand_ref, o_ref, tx_ref, tb_ref, tc_ref, st_ref, *, tl):
    L = SSD_CHUNK
    N = SSD_STATE
    nchunk = tl // L

    @pl.when(pl.program_id(1) == 0)
    def _():
        tx_ref[...] = jnp.zeros_like(tx_ref)
        tb_ref[...] = jnp.zeros_like(tb_ref)
        tc_ref[...] = jnp.zeros_like(tc_ref)
        st_ref[...] = jnp.zeros_like(st_ref)

    cw = cw_ref[...]
    cb = cb_ref[...]
    xs = _causal_conv_silu(x_ref, tx_ref[...], cw[:, :W_MIX], cb[:, :W_MIX])
    Bm = _causal_conv_silu(b_ref, tb_ref[...], cw[:, W_MIX:W_MIX + 2 * N], cb[:, W_MIX:W_MIX + 2 * N])
    Cm = _causal_conv_silu(c_ref, tc_ref[...], cw[:, W_MIX + 2 * N:], cb[:, W_MIX + 2 * N:])
    tx_ref[...] = x_ref[tl - TAIL:, :]
    tb_ref[...] = b_ref[tl - TAIL:, :]
    tc_ref[...] = c_ref[tl - TAIL:, :]

    dt = _softplus(sm_ref[...] + dtb_ref[...])
    a = dt * (-jnp.exp(alog_ref[...]))
    a_cum = _xdot(tri_ref[...], a)
    expand = expand_ref[...]
    dt_w = _xdot_r(dt, expand)
    acum_w = _xdot_r(a_cum, expand)
    xdt = xs * dt_w
    a_cum_t = a_cum.T
    e_acum = jnp.exp(acum_w)

    lane = lax.broadcasted_iota(jnp.int32, (1, LANES), 1)
    lane_w = lax.broadcasted_iota(jnp.int32, (1, W_MIX), 1)
    in_g0 = lane_w < W_MIX // SSD_GROUPS
    row = lax.broadcasted_iota(jnp.int32, (L, L), 0)
    col = lax.broadcasted_iota(jnp.int32, (L, L), 1)
    causal = col <= row
    chunks = [slice(c * L, (c + 1) * L) for c in range(nchunk)]
    heads_per_group = N_HEADS // SSD_GROUPS

    scores = [[_bdot_nt(Cm[rs, g * N:(g + 1) * N], Bm[rs, g * N:(g + 1) * N]) for g in range(SSD_GROUPS)]
              for rs in chunks]
    atot = [acum_w[rs.stop - 1:rs.stop, :] for rs in chunks]
    xdec = [xdt[rs, :] * jnp.exp(atot[c] - acum_w[rs, :]) for c, rs in enumerate(chunks)]
    contrib = [jnp.where(in_g0, _bdot_tn(Bm[rs, :N], xdec[c]), _bdot_tn(Bm[rs, N:], xdec[c]))
               for c, rs in enumerate(chunks)]
    lmats = [[jnp.exp(jnp.where(causal, a_cum[rs, h:h + 1] - a_cum_t[h:h + 1, rs], NEG)) for h in range(N_HEADS)]
             for rs in chunks]
    diag = [[_bdot(scores[c][h // heads_per_group] * lmats[c][h], xdt[rs, (h // 2) * LANES:(h // 2 + 1) * LANES])
             for h in range(N_HEADS)] for c, rs in enumerate(chunks)]

    states = [st_ref[...]]
    for c in range(nchunk):
        states.append(states[c] * jnp.exp(atot[c]) + contrib[c])
    st_ref[...] = states[nchunk]
    sb = [s.astype(BF16) for s in states[:nchunk]]
    off = [jnp.where(in_g0, jnp.dot(Cm[rs, :N].astype(BF16), sb[c], preferred_element_type=F32),
                     jnp.dot(Cm[rs, N:].astype(BF16), sb[c], preferred_element_type=F32))
           for c, rs in enumerate(chunks)]

    for c, rs in enumerate(chunks):
        y_diag = jnp.concatenate([jnp.where(lane < HEAD_DIM, diag[c][2 * p], diag[c][2 * p + 1])
                                  for p in range(N_PAIRS)], axis=1)
        y = y_diag + off[c] * e_acum[rs, :] + dskip_ref[...] * xs[rs, :]
        y = y * _silu(z_ref[rs, :])
        o_ref[rs, :] = y * lax.rsqrt(jnp.mean(y * y, axis=1, keepdims=True) + RMS_EPS) * nw_ref[...]


def _pad_lanes(v, offset):
    return jnp.zeros((1, LANES), F32).at[0, offset:offset + v.shape[0]].set(v.astype(F32))


def _ssd(proj3, conv_w, conv_b, dt_bias, a_log, d_skip, norm_w):
    B, S, _ = proj3.shape
    L = SSD_TILE
    N2 = 2 * SSD_STATE
    full = lambda shape: pl.BlockSpec(shape, lambda b, c: (0, 0))
    t = np.arange(L)
    tri = (t[None, :] <= t[:, None]) & (t[:, None] // SSD_CHUNK == t[None, :] // SSD_CHUNK)
    expand = np.arange(W_MIX)[None, :] // HEAD_DIM == np.arange(LANES)[:, None]
    return pl.pallas_call(
        functools.partial(_ssd_kernel, tl=L),
        out_shape=jax.ShapeDtypeStruct((B, S, W_MIX), F32),
        grid=(B, S // L),
        in_specs=[pl.BlockSpec((None, L, W_MIX), lambda b, c: (b, c, C_SSD_X // W_MIX)),
                  pl.BlockSpec((None, L, W_MIX), lambda b, c: (b, c, C_SSD_Z // W_MIX)),
                  pl.BlockSpec((None, L, N2), lambda b, c: (b, c, C_SSD_B // N2)),
                  pl.BlockSpec((None, L, N2), lambda b, c: (b, c, C_SSD_C // N2)),
                  pl.BlockSpec((None, L, LANES), lambda b, c: (b, c, C_SMALL_A // LANES)),
                  full((CONV_WIDTH, SSD_XBC)), full((1, SSD_XBC)), full((1, LANES)), full((1, LANES)),
                  full((1, W_MIX)), full((1, W_MIX)), full((L, L)), full((LANES, W_MIX))],
        out_specs=pl.BlockSpec((None, L, W_MIX), lambda b, c: (b, c, 0)),
        scratch_shapes=[pltpu.VMEM((TAIL, W_MIX), F32), pltpu.VMEM((TAIL, N2), F32),
                        pltpu.VMEM((TAIL, N2), F32), pltpu.VMEM((SSD_STATE, W_MIX), F32)],
        compiler_params=pltpu.CompilerParams(
            dimension_semantics=("parallel", "arbitrary"), vmem_limit_bytes=VMEM_LIMIT),
        name="ssd",
    )(proj3, proj3, proj3, proj3, proj3, conv_w, conv_b.reshape(1, -1),
      _pad_lanes(dt_bias, SM_DT), _pad_lanes(a_log, SM_DT),
      jnp.repeat(d_skip.astype(F32), HEAD_DIM).reshape(1, -1), norm_w.reshape(1, -1),
      jnp.asarray(tri, BF16), jnp.asarray(expand, BF16))


INV_LEAF = 16


def _pair_blockdiag(x):
    lo = lax.broadcasted_iota(jnp.int32, (1, LANES), 1) < HEAD_DIM
    xb = x.astype(BF16)
    zero = jnp.zeros_like(xb)
    return jnp.concatenate([jnp.where(lo, xb, zero), jnp.where(lo, zero, xb)], axis=0)


def _pair_mm(a, b):
    return jnp.dot(a.astype(BF16), _pair_blockdiag(b), preferred_element_type=F32)


def _pair_unit_lower_inverse(lms, eye, on_diag_block):
    mm = lambda xs, ys: [_pair_mm(x, y) for x, y in zip(xs, ys)]
    d = [jnp.where(on_diag_block, lm, 0.0) for lm in lms]
    off = [lm - x for lm, x in zip(lms, d)]
    d2 = mm(d, d)
    d4 = mm(d2, d2)
    d8 = mm(d4, d4)
    left = mm([eye - x for x in d], [eye + x for x in d2])
    right = mm([eye + x for x in d4], [eye + x for x in d8])
    dinv = mm(left, right)
    m = mm(dinv, off)
    m2 = mm(m, m)
    return mm(mm([eye - x for x in m], [eye + x for x in m2]), dinv)


def _split3(x):
    hi = x.astype(BF16)
    r = x - hi.astype(F32)
    mid = r.astype(BF16)
    lo = (r - mid.astype(F32)).astype(BF16)
    return hi, mid, lo


def _xdot(a01, x):
    a = a01.astype(BF16)
    return sum(jnp.dot(a, t, preferred_element_type=F32) for t in _split3(x))


def _xdot_r(x, b01):
    b = b01.astype(BF16)
    return sum(jnp.dot(t, b, preferred_element_type=F32) for t in _split3(x))


def _gdn_kernel(qkv_ref, g_ref, sm_ref, cw_ref, cb_ref, dtb_ref, alog_ref, nw_ref, same_ref, tri_ref, eb_ref, ed_ref,
                o_ref, tail_ref, st_ref, *, tl):
    C = DELTA_CHUNK

    @pl.when(pl.program_id(1) == 0)
    def _():
        tail_ref[...] = jnp.zeros_like(tail_ref)
        st_ref[...] = jnp.zeros_like(st_ref)

    qkv = _causal_conv_silu(qkv_ref, tail_ref[...], cw_ref[...], cb_ref[...])
    tail_ref[...] = qkv_ref[tl - TAIL:, :]
    q, k, v = qkv[:, :W_MIX], qkv[:, W_MIX:2 * W_MIX], qkv[:, 2 * W_MIX:]

    same_head = same_ref[...]
    q = q * lax.rsqrt(_bdot(q * q, same_head) + RMS_EPS) * ATTN_SCALE
    k = k * lax.rsqrt(_bdot(k * k, same_head) + RMS_EPS)

    sm = sm_ref[...]
    beta = jax.nn.sigmoid(sm)
    gdec = -jnp.exp(alog_ref[...]) * _softplus(sm + dtb_ref[...])
    gc = _xdot(tri_ref[...], gdec)
    beta_w = _xdot_r(beta, eb_ref[...])
    gc_w = _xdot_r(gc, ed_ref[...])

    kb = k * beta_w
    vb = v * beta_w
    egc = jnp.exp(gc_w)
    kbe = kb * egc
    qe = q * egc

    row = lax.broadcasted_iota(jnp.int32, (C, LANES), 0)
    col = lax.broadcasted_iota(jnp.int32, (C, LANES), 1) % HEAD_DIM
    eye = (row == col).astype(F32)
    incl = col <= row
    strict = col < row
    on_diag_block = (row // INV_LEAF) == (col // INV_LEAF)
    s_row = lax.broadcasted_iota(jnp.int32, (LANES, LANES), 0)
    s_col = lax.broadcasted_iota(jnp.int32, (LANES, LANES), 1)
    pair_diag = (s_row // HEAD_DIM) == (s_col // HEAD_DIM)

    tiles = [(slice(c * C, (c + 1) * C), slice(p * LANES, (p + 1) * LANES))
             for c in range(tl // C) for p in range(N_PAIRS)]
    grams, decays = [], []
    for rows, cols in tiles:
        gcol = gc_w[rows, cols]
        grow = jnp.sum(gcol * eye, axis=0, keepdims=True)
        decays.append(jnp.exp(jnp.where(incl, gcol - grow, NEG)))
        grams.append(lax.dot_general(jnp.concatenate([kb[rows, cols], q[rows, cols]], axis=0).astype(BF16),
                                     _pair_blockdiag(k[rows, cols]), (((1,), (1,)), ((), ())),
                                     preferred_element_type=F32))
    lms = [jnp.where(strict, g[:C] * d, 0.0) for g, d in zip(grams, decays)]
    attns = [jnp.where(incl, g[C:] * d, 0.0) for g, d in zip(grams, decays)]
    t_invs = _pair_unit_lower_inverse(lms, eye, on_diag_block)
    uws = [jnp.dot(t.astype(BF16),
                   jnp.concatenate([_pair_blockdiag(vb[rows, cols]), _pair_blockdiag(kbe[rows, cols])], axis=1),
                   preferred_element_type=F32) for t, (rows, cols) in zip(t_invs, tiles)]
    u2s = [uw[:, :LANES] for uw in uws]
    w2s = [uw[:, LANES:] for uw in uws]
    g_lasts = [gc_w[rows.stop - 1:rows.stop, cols] for rows, cols in tiles]
    k_decs = [k[rows, cols] * jnp.exp(gl - gc_w[rows, cols]) for (rows, cols), gl in zip(tiles, g_lasts)]
    kd_uw = [_bdot_tn(kd, jnp.concatenate([u2, w2], axis=1)) for kd, u2, w2 in zip(k_decs, u2s, w2s)]
    kd_u = [jnp.where(pair_diag, x[:, :LANES], 0.0) for x in kd_uw]
    kd_w = [jnp.where(pair_diag, x[:, LANES:], 0.0).astype(BF16) for x in kd_uw]

    states = [[st_ref[p] for p in range(N_PAIRS)]]
    for c in range(tl // C):
        nxt = []
        for p in range(N_PAIRS):
            t = c * N_PAIRS + p
            s_in = states[c][p]
            nxt.append(s_in * jnp.exp(g_lasts[t]) + kd_u[t]
                       - jnp.dot(kd_w[t], s_in.astype(BF16), preferred_element_type=F32))
        states.append(nxt)
    for p in range(N_PAIRS):
        st_ref[p] = states[tl // C][p]

    sbs = [states[t // N_PAIRS][t % N_PAIRS].astype(BF16) for t in range(len(tiles))]
    v_news = [u2 - jnp.dot(w2.astype(BF16), sb, preferred_element_type=F32) for u2, w2, sb in zip(u2s, w2s, sbs)]
    o2s = [jnp.dot(qe[rows, cols].astype(BF16), sb, preferred_element_type=F32) + _pair_mm(attn, v_new)
           for (rows, cols), sb, attn, v_new in zip(tiles, sbs, attns, v_news)]
    for c in range(tl // C):
        o_ref[c * C:(c + 1) * C, :] = jnp.concatenate(o2s[c * N_PAIRS:(c + 1) * N_PAIRS], axis=1)

    o = o_ref[...]
    o = o * lax.rsqrt(_bdot(o * o, same_head) * (1.0 / HEAD_DIM) + RMS_EPS) * nw_ref[...]
    o_ref[...] = o * _silu(g_ref[...])


def _gdn(proj3, conv_w, conv_b, dt_bias, a_log, norm_w):
    B, S, _ = proj3.shape
    tl = 512
    full = lambda shape: pl.BlockSpec(shape, lambda b, c: (0, 0))
    t = np.arange(tl)
    tri = (t[None, :] <= t[:, None]) & (t[:, None] // DELTA_CHUNK == t[None, :] // DELTA_CHUNK)
    head_of = np.arange(W_MIX) // HEAD_DIM
    same = head_of[:, None] == head_of[None, :]
    lanes = np.arange(LANES)[:, None]
    return pl.pallas_call(
        functools.partial(_gdn_kernel, tl=tl),
        out_shape=jax.ShapeDtypeStruct((B, S, W_MIX), F32),
        grid=(B, S // tl),
        in_specs=[pl.BlockSpec((None, tl, 3 * W_MIX), lambda b, c: (b, c, C_DN_QKV // (3 * W_MIX))),
                  pl.BlockSpec((None, tl, W_MIX), lambda b, c: (b, c, C_DN_G // W_MIX)),
                  pl.BlockSpec((None, tl, LANES), lambda b, c: (b, c, C_SMALL_B // LANES)),
                  full((CONV_WIDTH, 3 * W_MIX)), full((1, 3 * W_MIX)), full((1, LANES)), full((1, LANES)),
                  full((1, W_MIX)), full((W_MIX, W_MIX)), full((tl, tl)), full((LANES, W_MIX)), full((LANES, W_MIX))],
        out_specs=pl.BlockSpec((None, tl, W_MIX), lambda b, c: (b, c, 0)),
        scratch_shapes=[pltpu.VMEM((TAIL, 3 * W_MIX), F32), pltpu.VMEM((N_PAIRS, LANES, LANES), F32)],
        compiler_params=pltpu.CompilerParams(
            dimension_semantics=("parallel", "arbitrary"), vmem_limit_bytes=VMEM_LIMIT),
        name="gdn",
    )(proj3, proj3, proj3, conv_w, conv_b.reshape(1, -1),
      _pad_lanes(dt_bias, SM_DECAY), _pad_lanes(a_log, SM_DECAY), jnp.tile(norm_w.astype(F32), N_HEADS).reshape(1, -1),
      jnp.asarray(same, BF16), jnp.asarray(tri, BF16),
      jnp.asarray(head_of[None, :] + SM_BETA == lanes, BF16), jnp.asarray(head_of[None, :] + SM_DECAY == lanes, BF16))


def _outproj_kernel(x_ref, ya_ref, yb_ref, yc_ref, yd_ref, w_ref, lg_ref, lb_ref, o_ref):
    mix = (_bdot(ya_ref[...], w_ref[0]) + _bdot(yb_ref[...], w_ref[1])
           + _bdot(yc_ref[...], w_ref[2]) + _bdot(yd_ref[...], w_ref[3]))
    r = DEEPNORM_ALPHA * x_ref[...] + mix
    mu = jnp.mean(r, axis=1, keepdims=True)
    var = jnp.mean(jnp.square(r - mu), axis=1, keepdims=True)
    o_ref[...] = (r - mu) * lax.rsqrt(var + LN_EPS) * lg_ref[...] + lb_ref[...]


def _outproj(x2d, ya, yb, yc, yd, w_out, ln_g, ln_b):
    T = x2d.shape[0]
    tm = min(512, T)
    rowblk = lambda width: pl.BlockSpec((tm, width), lambda i: (i, 0))
    return pl.pallas_call(
        _outproj_kernel,
        out_shape=jax.ShapeDtypeStruct((T, D_MODEL), F32),
        grid=(T // tm,),
        in_specs=[rowblk(D_MODEL), rowblk(W_MIX), rowblk(W_MIX), rowblk(W_MIX), rowblk(W_MIX),
                  pl.BlockSpec((4, W_MIX, D_MODEL), lambda i: (0, 0, 0)),
                  pl.BlockSpec((1, D_MODEL), lambda i: (0, 0)), pl.BlockSpec((1, D_MODEL), lambda i: (0, 0))],
        out_specs=rowblk(D_MODEL),
        compiler_params=pltpu.CompilerParams(
            dimension_semantics=("parallel",), vmem_limit_bytes=VMEM_LIMIT),
        name="outproj",
    )(x2d, ya, yb, yc, yd, w_out, ln_g.reshape(1, -1), ln_b.reshape(1, -1))


def _alibi_slopes():
    n = 2 * N_HEADS
    s = (2.0 ** (-8.0 * (np.arange(n) + 1) / n)).astype(np.float32)
    return jnp.asarray(s[:N_HEADS]), jnp.asarray(s[N_HEADS:])


def _layer(x, w_in_t, layer, w_out, ssm_conv_w, ssm_conv_b, ssm_dt_bias, ssm_A_log, ssm_D, ssm_norm_w,
           dn_conv_w, dn_conv_b, dn_dt_bias, dn_A_log, dn_norm_w, ln_g, ln_b):
    B, S, _ = x.shape
    T = B * S
    slopes_dil, slopes_moba = _alibi_slopes()
    x2d = x.reshape(T, D_MODEL)
    proj2d = _inproj(x2d, w_in_t, layer)
    proj3 = proj2d.reshape(B, S, PROJ_COLS)

    ya = _moba(proj3, slopes_moba)
    yb = _ssd(proj3, ssm_conv_w, ssm_conv_b, ssm_dt_bias, ssm_A_log, ssm_D, ssm_norm_w)
    yd = _gdn(proj3, dn_conv_w, dn_conv_b, dn_dt_bias, dn_A_log, dn_norm_w)

    yc = _dilated(proj3, slopes_dil)

    out = _outproj(x2d, ya.reshape(T, W_MIX), yb.reshape(T, W_MIX), yc.reshape(T, W_MIX), yd.reshape(T, W_MIX),
                   w_out.reshape(4, W_MIX, D_MODEL).astype(BF16), ln_g, ln_b)
    return out.reshape(B, S, D_MODEL)


def kernel(x, w_in, w_out, ssm_conv_w, ssm_conv_b, ssm_dt_bias, ssm_A_log, ssm_D, ssm_norm_w,
           dn_conv_w, dn_conv_b, dn_dt_bias, dn_A_log, dn_norm_w, ln_g, ln_b):
    assert x.shape[1] % 2048 == 0 and x.shape[2] == D_MODEL
    w_in_t = _repack_w_in(jnp.transpose(w_in, (2, 0, 1)))
    for l in range(DEPTH):
        x = _layer(x, w_in_t, l, w_out[l], ssm_conv_w[l], ssm_conv_b[l], ssm_dt_bias[l], ssm_A_log[l],
                   ssm_D[l], ssm_norm_w[l], dn_conv_w[l], dn_conv_b[l], dn_dt_bias[l], dn_A_log[l],
                   dn_norm_w[l], ln_g[l], ln_b[l])
    return x
```

```python
import functools
import math

import numpy as np
import jax
import jax.numpy as jnp
from jax import lax
from jax.experimental import pallas as pl
from jax.experimental.pallas import tpu as pltpu

F32 = jnp.float32
BF16 = jnp.bfloat16
HIGHEST = lax.Precision.HIGHEST

D_MODEL = 1024
DEPTH = 2
HEAD_DIM = 64
N_HEADS = 6
W_MIX = N_HEADS * HEAD_DIM
N_PAIRS = W_MIX // 128
D_MIX = 4 * W_MIX
MOBA_BLOCK = 256
MOBA_TOPK = 3
MOBA_NFEAT = 3
MOBA_UNROLL = 2
MOBA_QBLOCKS = 4
MOBA_LROWS = 16
SSD_STATE = 128
SSD_GROUPS = 2
SSD_CHUNK = 128
SSD_TILE = 512
SSD_XBC = W_MIX + 2 * SSD_GROUPS * SSD_STATE
CONV_WIDTH = 4
DIL_DILATIONS = (1, 4, 16)
DIL_GROUPS = len(DIL_DILATIONS)
DIL_BATCH = 16
DIL_MERGE_ROWS = 256
DIL_SPAN = 128
DELTA_CHUNK = 64
LN_EPS = 1e-5
RMS_EPS = 1e-6
DEEPNORM_ALPHA = (2.0 * DEPTH) ** 0.25
ATTN_SCALE = HEAD_DIM ** -0.5
LOG2E = math.log2(math.e)
NEG = -1e30

LANES = 128
TAIL = 8
VMEM_LIMIT = 48 * 1024 * 1024

C_MOBA = 0
C_SSD_X = 1536
C_SSD_Z = 1920
C_DN_QKV = 2304
C_DN_G = 3456
C_DIL = 3840
C_SSD_B = 5376
C_SSD_C = 5632
C_SMALL_A = 5888
C_SMALL_B = 6016
PROJ_COLS = 6144
SM_DT, SM_BETA, SM_DECAY = 0, 116, 122

_R = [int(v) for v in np.cumsum([0, 384, 384, 384, 384, SSD_XBC, 384, 6, 384, 384, 384, 384, 1152, 384, 6, 6])]
R_SSD_XBC, R_SSD_Z, R_SSD_DT, R_DIL, R_DN_QKV, R_DN_G, R_END = _R[4], _R[5], _R[6], _R[7], _R[11], _R[12], _R[15]
_SEGMENTS = ((C_MOBA, 0, 4 * W_MIX), (C_SSD_X, R_SSD_XBC, W_MIX), (C_SSD_Z, R_SSD_Z, W_MIX),
             (C_DN_QKV, R_DN_QKV, 3 * W_MIX), (C_DN_G, R_DN_G, W_MIX), (C_DIL, R_DIL, 4 * W_MIX),
             (C_SSD_B, R_SSD_XBC + W_MIX, 2 * SSD_GROUPS * SSD_STATE), (C_SMALL_A, R_SSD_DT, LANES),
             (C_SMALL_B, R_END - LANES, LANES))
_BLOCK_SRC = np.zeros(PROJ_COLS // LANES, np.int32)
for _dst, _src, _n in _SEGMENTS:
    _BLOCK_SRC[_dst // LANES:(_dst + _n) // LANES] = _src + LANES * np.arange(_n // LANES)
assert R_SSD_DT + LANES <= R_END


def _repack_kernel(src_ref, w_ref, o_ref):
    for layer in range(o_ref.shape[0]):
        o_ref[layer] = w_ref[:, layer, :].astype(BF16)


def _repack_w_in(w_t):
    _, depth, dm = w_t.shape
    return pl.pallas_call(
        _repack_kernel,
        out_shape=jax.ShapeDtypeStruct((depth, PROJ_COLS, dm), BF16),
        grid_spec=pltpu.PrefetchScalarGridSpec(
            num_scalar_prefetch=1, grid=(PROJ_COLS // LANES,),
            in_specs=[pl.BlockSpec((pl.Element(LANES), pl.Element(depth), pl.Element(dm)),
                                   lambda b, src: (src[b], 0, 0))],
            out_specs=pl.BlockSpec((depth, LANES, dm), lambda b, src: (0, b, 0))),
        compiler_params=pltpu.CompilerParams(dimension_semantics=("parallel",), vmem_limit_bytes=VMEM_LIMIT),
        name="repack_w_in",
    )(jnp.asarray(_BLOCK_SRC), w_t)


def _silu(x):
    return x * jax.nn.sigmoid(x)


def _softplus(x):
    return jnp.maximum(x, 0.0) + jnp.log(1.0 + jnp.exp(-jnp.abs(x)))


def _bdot(a, b):
    return jnp.dot(a.astype(BF16), b.astype(BF16), preferred_element_type=F32)


def _bdot_nt(a, b):
    return lax.dot_general(a.astype(BF16), b.astype(BF16), (((1,), (1,)), ((), ())),
                           preferred_element_type=F32)


def _bdot_tn(a, b):
    return lax.dot_general(a.astype(BF16), b.astype(BF16), (((0,), (0,)), ((), ())),
                           preferred_element_type=F32)


def _fdot(a, b):
    return jnp.dot(a, b, precision=HIGHEST, preferred_element_type=F32)


def _causal_conv_silu(x_ref, tail, w, b):
    L = x_ref.shape[0]
    head = jnp.concatenate([tail, x_ref[:TAIL, :]], axis=0)
    y_head, y_rest = b, b
    for j in range(CONV_WIDTH):
        off = TAIL - (CONV_WIDTH - 1) + j
        y_head = y_head + w[j:j + 1, :] * head[off:off + TAIL, :]
        y_rest = y_rest + w[j:j + 1, :] * x_ref[pl.ds(off, L - TAIL), :]
    return _silu(jnp.concatenate([y_head, y_rest], axis=0))


def _inproj_kernel(x_ref, w_ref, o_ref, xb_ref):
    @pl.when(pl.program_id(1) == 0)
    def _():
        xb_ref[...] = x_ref[...].astype(BF16)

    o_ref[...] = lax.dot_general(xb_ref[...], w_ref[...], (((1,), (1,)), ((), ())), preferred_element_type=F32)


def _inproj(x2d, w_t, layer):
    T = x2d.shape[0]
    tm = min(1024, T)
    tn = 2048
    return pl.pallas_call(
        _inproj_kernel,
        out_shape=jax.ShapeDtypeStruct((T, PROJ_COLS), F32),
        grid=(T // tm, PROJ_COLS // tn),
        in_specs=[pl.BlockSpec((tm, D_MODEL), lambda i, j: (i, 0)),
                  pl.BlockSpec((None, tn, D_MODEL), lambda i, j: (layer, j, 0))],
        out_specs=pl.BlockSpec((tm, tn), lambda i, j: (i, j)),
        scratch_shapes=[pltpu.VMEM((tm, D_MODEL), BF16)],
        compiler_params=pltpu.CompilerParams(
            dimension_semantics=("parallel", "arbitrary"), vmem_limit_bytes=VMEM_LIMIT),
        name="inproj",
    )(x2d, w_t)


def _moba_kernel(slopes_ref, q_ref, k_ref, v_ref, g_ref, o_ref, kaug_ref, vt_ref, kmean_ref, s_ref, *, nblk):
    p = pl.program_id(1)
    L = MOBA_BLOCK
    NF = MOBA_NFEAT

    @pl.when(pl.program_id(2) == 0)
    def _():
        lane = lax.broadcasted_iota(jnp.int32, (L, LANES), 1)
        feat_lane = lane % HEAD_DIM
        c_loc = lax.broadcasted_iota(jnp.int32, (L, LANES), 0).astype(F32)
        feat = jnp.where(feat_lane < NF, 1.0, jnp.where(feat_lane < 2 * NF, c_loc, 0.0))
        ones = jnp.ones((MOBA_LROWS, L), F32)
        for n in range(nblk):
            kblk = k_ref[n * L:(n + 1) * L, :]
            kmean_ref[n:n + 1, :] = jnp.mean(kblk, axis=0, keepdims=True)
            kaug_ref[0, n] = jnp.where(lane < HEAD_DIM, kblk, feat).astype(BF16)
            kaug_ref[1, n] = jnp.where(lane < HEAD_DIM, feat, kblk).astype(BF16)
            vt = v_ref[n * L:(n + 1) * L, :].T
            for h in range(2):
                vt_ref[h, n] = jnp.concatenate([ones, vt[h * HEAD_DIM:(h + 1) * HEAD_DIM, :]], axis=0).astype(BF16)

    QB = MOBA_QBLOCKS
    streams = [(qb, h) for qb in range(QB) for h in range(2)]
    i0 = QB * pl.program_id(2)
    qt_all = q_ref[...].T
    qt = [qt_all[:, qb * L:(qb + 1) * L] for qb in range(QB)]
    qrow = lax.broadcasted_iota(jnp.int32, (LANES, L), 0)
    r_loc = lax.broadcasted_iota(jnp.int32, (1, L), 1).astype(F32)
    brow = lax.broadcasted_iota(jnp.int32, (nblk, L), 0).astype(F32)
    frow = lax.broadcasted_iota(jnp.int32, (2 * TAIL, L), 0)
    kmean = kmean_ref[...]
    i_f = [(i0 + qb).astype(F32) for qb in range(QB)]

    qparts = [(qt[qb][h * HEAD_DIM:(h + 1) * HEAD_DIM, :] * (ATTN_SCALE * LOG2E)).astype(BF16) for qb, h in streams]
    slope_rows = [jnp.full((1, L), slopes_ref[2 * p + h] * LOG2E, F32) for h in range(2)]
    zeros48 = jnp.zeros((HEAD_DIM - 2 * TAIL, L), BF16)

    def q_aug(st, t_row):
        h = streams[st][1]
        terms = _split3(t_row) + _split3(slope_rows[h])
        feat = jnp.zeros((2 * TAIL, L), F32)
        for f, term in enumerate(terms):
            feat = jnp.where(frow == f, term.astype(F32), feat)
        feat = feat.astype(BF16)
        if h == 0:
            return jnp.concatenate([qparts[st], feat, zeros48], axis=0)
        return jnp.concatenate([feat, zeros48, qparts[st]], axis=0)

    def past_row(st, n, in_range=None):
        qb, h = streams[st]
        n_f = n.astype(F32)
        picked = (sels[st][0] == n_f) | (sels[st][1] == n_f) | (sels[st][2] == n_f)
        if in_range is not None:
            picked = picked & in_range
        return jnp.where(picked, -slope_rows[h] * (r_loc + L * (i_f[qb] - n_f)), NEG)

    U = MOBA_UNROLL
    NS = len(streams)

    def put_scores(slot, n0):
        for u in range(U):
            n = n0 + u
            for st, (qb, h) in enumerate(streams):
                s_ref[slot, st, u] = jnp.dot(kaug_ref[h, jnp.minimum(n, nblk - 1)], q_aug(st, past_row(st, n, n < i0)),
                                             preferred_element_type=F32).astype(BF16)

    def new_max(slot, carry):
        return [functools.reduce(jnp.maximum, [carry[2 * st]] + [
            jnp.max(s_ref[slot, st, u], axis=0, keepdims=True).astype(F32) for u in range(U)]) for st in range(NS)]

    def absorb(slot, n0, m_new, carry):
        ps = [jnp.concatenate([jnp.exp2(s_ref[slot, st, u] - m_new[st].astype(BF16)) for u in range(U)], axis=0)
              for st in range(NS)]
        vts = [jnp.concatenate([vt_ref[h, jnp.minimum(n0 + u, nblk - 1)] for u in range(U)], axis=1) for h in range(2)]
        pv = [jnp.dot(vts[streams[st][1]], ps[st], preferred_element_type=F32) for st in range(NS)]
        out = []
        for st in range(NS):
            out += [m_new[st], jnp.exp2(carry[2 * st] - m_new[st]) * carry[2 * st + 1] + pv[st]]
        return tuple(out)

    c_idx = lax.broadcasted_iota(jnp.int32, (L, L), 0)
    r_idx = lax.broadcasted_iota(jnp.int32, (L, L), 1)
    own = [jnp.where(c_idx <= r_idx,
                     jnp.dot(kaug_ref[h, i0 + qb], q_aug(st, -slope_rows[h] * r_loc), preferred_element_type=F32),
                     NEG).astype(BF16) for st, (qb, h) in enumerate(streams)]

    gates = [jnp.where(brow < i_f[qb],
                       jnp.dot(kmean, jnp.where((qrow >= HEAD_DIM * h) & (qrow < HEAD_DIM * (h + 1)), qt[qb], 0.0),
                               precision=HIGHEST, preferred_element_type=F32), -jnp.inf) for qb, h in streams]
    sels = [[] for _ in streams]
    for kk in range(MOBA_TOPK):
        for st, (qb, h) in enumerate(streams):
            mx = jnp.max(gates[st], axis=0, keepdims=True)
            idx = jnp.min(jnp.where(gates[st] == mx, brow, float(nblk)), axis=0, keepdims=True)
            sels[st].append(jnp.where(kk < i0 + qb, idx, -1.0))
            gates[st] = jnp.where(brow == idx, -jnp.inf, gates[st])

    init = []
    for st, (qb, h) in enumerate(streams):
        tiles = [jnp.dot(kaug_ref[h, i0 + e], q_aug(st, past_row(st, i0 + e)),
                         preferred_element_type=F32).astype(BF16) for e in range(qb)] + [own[st]]
        m0 = functools.reduce(jnp.maximum, [jnp.max(s, axis=0, keepdims=True) for s in tiles])
        p0 = jnp.concatenate([jnp.exp2(s - m0) for s in tiles], axis=0)
        vt0 = jnp.concatenate([vt_ref[h, i0 + e] for e in range(qb + 1)], axis=1)
        init += [m0.astype(F32), jnp.dot(vt0, p0, preferred_element_type=F32)]

    put_scores(0, jnp.int32(0))

    def body(j, carry):
        n0 = 2 * U * j
        m_a = new_max(0, carry)
        put_scores(1, n0 + U)
        carry = absorb(0, n0, m_a, carry)
        m_b = new_max(1, carry)
        put_scores(0, n0 + 2 * U)
        return absorb(1, n0 + U, m_b, carry)

    res = lax.fori_loop(0, (i0 + 2 * U - 1) // (2 * U), body, tuple(init))
    for qb in range(QB):
        out_t = jnp.concatenate([res[2 * st + 1][MOBA_LROWS:] / res[2 * st + 1][0:1]
                                 for st in (2 * qb, 2 * qb + 1)], axis=0)
        o_ref[qb * L:(qb + 1) * L, :] = out_t.T * _silu(g_ref[qb * L:(qb + 1) * L, :])


def _moba(proj3, slopes):
    B, S, _ = proj3.shape
    L = MOBA_BLOCK
    nblk = S // L
    QB = MOBA_QBLOCKS
    cb = C_MOBA // LANES
    return pl.pallas_call(
        functools.partial(_moba_kernel, nblk=nblk),
        out_shape=jax.ShapeDtypeStruct((B, S, W_MIX), F32),
        grid=(B, N_PAIRS, nblk // QB),
        in_specs=[pl.BlockSpec(memory_space=pltpu.SMEM),
                  pl.BlockSpec((None, QB * L, LANES), lambda b, p, i: (b, i, cb + p)),
                  pl.BlockSpec((None, S, LANES), lambda b, p, i: (b, 0, cb + N_PAIRS + p)),
                  pl.BlockSpec((None, S, LANES), lambda b, p, i: (b, 0, cb + 2 * N_PAIRS + p)),
                  pl.BlockSpec((None, QB * L, LANES), lambda b, p, i: (b, i, cb + 3 * N_PAIRS + p))],
        out_specs=pl.BlockSpec((None, QB * L, LANES), lambda b, p, i: (b, i, p)),
        scratch_shapes=[pltpu.VMEM((2, nblk, L, LANES), BF16),
                        pltpu.VMEM((2, nblk, MOBA_LROWS + HEAD_DIM, L), BF16),
                        pltpu.VMEM((nblk, LANES), F32),
                        pltpu.VMEM((2, 2 * QB, MOBA_UNROLL, L, L), BF16)],
        compiler_params=pltpu.CompilerParams(
            dimension_semantics=("parallel", "parallel", "arbitrary"), vmem_limit_bytes=VMEM_LIMIT),
        name="moba",
    )(slopes, proj3, proj3, proj3, proj3)


def _dil_kernel(slopes_ref, q_ref, kc_ref, kp_ref, vc_ref, vp_ref, g_ref, o_ref, acc_ref, m_ref, l_ref, *, tq):
    p = pl.program_id(1)
    first = pl.program_id(2) == 0
    SB = DIL_SPAN
    lane = lax.broadcasted_iota(jnp.int32, (1, LANES), 1)
    lo = lane < HEAD_DIM
    row = lax.broadcasted_iota(jnp.int32, (SB, 2 * SB), 0)
    col = lax.broadcasted_iota(jnp.int32, (SB, 2 * SB), 1)
    dist = (row + SB - col).astype(F32)
    band = (col >= row) & (col <= row + SB)
    no_halo = jnp.logical_and(first, col < SB)

    def rows(start, d):
        return pl.ds(start, SB) if d == 1 else pl.ds(start, SB, stride=d)

    for gi, d in enumerate(DIL_DILATIONS):
        span = d * SB
        bias, bias_first = [], []
        for h in range(2):
            slope = slopes_ref[2 * p + h] * (float(d) * LOG2E)
            bias.append(jnp.where(band, -slope * dist, NEG))
            bias_first.append(jnp.where(no_halo, NEG, bias[h]))
        problems = [(r, j) for j in range(tq // span) for r in range(d)]
        for c0 in range(0, len(problems), DIL_BATCH):
            chunk = problems[c0:c0 + DIL_BATCH]
            blocks = {}

            def block(r, j):
                if (r, j) not in blocks:
                    if j < 0:
                        sel = rows(tq - span + r, d)
                        kb, vb = kp_ref[sel, :], vp_ref[sel, :]
                    else:
                        sel = rows(r + span * j, d)
                        kb, vb = kc_ref[sel, :], vc_ref[sel, :]
                    blocks[(r, j)] = (kb.astype(BF16),
                                      (jnp.where(lo, vb, 1.0).astype(BF16), jnp.where(lo, 1.0, vb).astype(BF16)))
                return blocks[(r, j)]

            kv, qs = [], []
            for r, j in chunk:
                (kp, vp), (kc, vc) = block(r, j - 1), block(r, j)
                kv.append((jnp.concatenate([kp, kc], axis=0),
                           tuple(jnp.concatenate([vp[h], vc[h]], axis=0) for h in range(2))))
                qs.append(q_ref[rows(r + span * j, d), :] * (ATTN_SCALE * LOG2E))
            work = [(c, h) for c in range(len(chunk)) for h in range(2)]
            ss = [(_bdot_nt(jnp.where(lo if h == 0 else jnp.logical_not(lo), qs[c], 0.0), kv[c][0])
                   + (bias_first[h] if chunk[c][1] == 0 else bias[h])).astype(BF16) for c, h in work]
            ms = [jnp.max(s, axis=1, keepdims=True) for s in ss]
            ps = [jnp.exp2(s - m) for s, m in zip(ss, ms)]
            pvs = [jnp.dot(e, kv[c][1][h], preferred_element_type=F32) for e, (c, h) in zip(ps, work)]
            for c, (r, j) in enumerate(chunk):
                cur = rows(r + span * j, d)
                acc_ref[gi, cur, :] = jnp.where(lo, pvs[2 * c], pvs[2 * c + 1])
                m_ref[gi, cur, :] = jnp.where(lo, ms[2 * c].astype(F32), ms[2 * c + 1].astype(F32))
                l_ref[gi, cur, :] = pltpu.roll(jnp.where(lo, pvs[2 * c + 1], pvs[2 * c]), HEAD_DIM, axis=1)

    for c0 in range(0, tq, DIL_MERGE_ROWS):
        rs = slice(c0, c0 + DIL_MERGE_ROWS)
        ms = [m_ref[gi, rs, :] for gi in range(DIL_GROUPS)]
        top = functools.reduce(jnp.maximum, ms)
        ws = [jnp.exp2(m - top) for m in ms]
        num = sum(w * acc_ref[gi, rs, :] for gi, w in enumerate(ws))
        den = sum(w * l_ref[gi, rs, :] for gi, w in enumerate(ws))
        o_ref[rs, :] = num / den * _silu(g_ref[rs, :])


def _dilated(proj3, slopes):
    B, S, _ = proj3.shape
    tq = DIL_DILATIONS[-1] * DIL_SPAN
    base = C_DIL // LANES
    cur = lambda off: pl.BlockSpec((None, tq, LANES), lambda b, p, t: (b, t, base + off + p))
    prev = lambda off: pl.BlockSpec((None, tq, LANES), lambda b, p, t: (b, jnp.maximum(t - 1, 0), base + off + p))
    scratch = pltpu.VMEM((DIL_GROUPS, tq, LANES), F32)
    return pl.pallas_call(
        functools.partial(_dil_kernel, tq=tq),
        out_shape=jax.ShapeDtypeStruct((B, S, W_MIX), F32),
        grid=(B, N_PAIRS, S // tq),
        in_specs=[pl.BlockSpec(memory_space=pltpu.SMEM),
                  cur(0), cur(N_PAIRS), prev(N_PAIRS), cur(2 * N_PAIRS), prev(2 * N_PAIRS), cur(3 * N_PAIRS)],
        out_specs=pl.BlockSpec((None, tq, LANES), lambda b, p, t: (b, t, p)),
        scratch_shapes=[scratch, scratch, scratch],
        compiler_params=pltpu.CompilerParams(
            dimension_semantics=("parallel", "parallel", "arbitrary"), vmem_limit_bytes=VMEM_LIMIT),
        name="dilated",
    )(slopes, proj3, proj3, proj3, proj3, proj3, proj3)


def _ssd_kernel(x_ref, z_ref, b_ref, c_ref, sm_ref, cw_ref, cb_ref, dtb_ref, alog_ref, dskip_ref, nw_ref,
                tri_ref, expand_ref, o_ref, tx_ref, tb_ref, tc_ref, st_ref, *, tl):
    L = SSD_CHUNK
    N = SSD_STATE
    nchunk = tl // L

    @pl.when(pl.program_id(1) == 0)
    def _():
        tx_ref[...] = jnp.zeros_like(tx_ref)
        tb_ref[...] = jnp.zeros_like(tb_ref)
        tc_ref[...] = jnp.zeros_like(tc_ref)
        st_ref[...] = jnp.zeros_like(st_ref)

    cw = cw_ref[...]
    cb = cb_ref[...]
    xs = _causal_conv_silu(x_ref, tx_ref[...], cw[:, :W_MIX], cb[:, :W_MIX])
    Bm = _causal_conv_silu(b_ref, tb_ref[...], cw[:, W_MIX:W_MIX + 2 * N], cb[:, W_MIX:W_MIX + 2 * N])
    Cm = _causal_conv_silu(c_ref, tc_ref[...], cw[:, W_MIX + 2 * N:], cb[:, W_MIX + 2 * N:])
    tx_ref[...] = x_ref[tl - TAIL:, :]
    tb_ref[...] = b_ref[tl - TAIL:, :]
    tc_ref[...] = c_ref[tl - TAIL:, :]

    dt = _softplus(sm_ref[...] + dtb_ref[...])
    a = dt * (-jnp.exp(alog_ref[...]))
    a_cum = _xdot(tri_ref[...], a)
    expand = expand_ref[...]
    dt_w = _xdot_r(dt, expand)
    acum_w = _xdot_r(a_cum, expand)
    xdt = xs * dt_w
    a_cum_t = a_cum.T
    e_acum = jnp.exp(acum_w)

    lane = lax.broadcasted_iota(jnp.int32, (1, LANES), 1)
    lane_w = lax.broadcasted_iota(jnp.int32, (1, W_MIX), 1)
    in_g0 = lane_w < W_MIX // SSD_GROUPS
    row = lax.broadcasted_iota(jnp.int32, (L, L), 0)
    col = lax.broadcasted_iota(jnp.int32, (L, L), 1)
    causal = col <= row
    chunks = [slice(c * L, (c + 1) * L) for c in range(nchunk)]
    heads_per_group = N_HEADS // SSD_GROUPS

    scores = [[_bdot_nt(Cm[rs, g * N:(g + 1) * N], Bm[rs, g * N:(g + 1) * N]) for g in range(SSD_GROUPS)]
              for rs in chunks]
    atot = [acum_w[rs.stop - 1:rs.stop, :] for rs in chunks]
    xdec = [xdt[rs, :] * jnp.exp(atot[c] - acum_w[rs, :]) for c, rs in enumerate(chunks)]
    contrib = [jnp.where(in_g0, _bdot_tn(Bm[rs, :N], xdec[c]), _bdot_tn(Bm[rs, N:], xdec[c]))
               for c, rs in enumerate(chunks)]
    lmats = [[jnp.exp(jnp.where(causal, a_cum[rs, h:h + 1] - a_cum_t[h:h + 1, rs], NEG)) for h in range(N_HEADS)]
             for rs in chunks]
    diag = [[_bdot(scores[c][h // heads_per_group] * lmats[c][h], xdt[rs, (h // 2) * LANES:(h // 2 + 1) * LANES])
             for h in range(N_HEADS)] for c, rs in enumerate(chunks)]

    states = [st_ref[...]]
    for c in range(nchunk):
        states.append(states[c] * jnp.exp(atot[c]) + contrib[c])
    st_ref[...] = states[nchunk]
    sb = [s.astype(BF16) for s in states[:nchunk]]
    off = [jnp.where(in_g0, jnp.dot(Cm[rs, :N].astype(BF16), sb[c], preferred_element_type=F32),
                     jnp.dot(Cm[rs, N:].astype(BF16), sb[c], preferred_element_type=F32))
           for c, rs in enumerate(chunks)]

    for c, rs in enumerate(chunks):
        y_diag = jnp.concatenate([jnp.where(lane < HEAD_DIM, diag[c][2 * p], diag[c][2 * p + 1])
                                  for p in range(N_PAIRS)], axis=1)
        y = y_diag + off[c] * e_acum[rs, :] + dskip_ref[...] * xs[rs, :]
        y = y * _silu(z_ref[rs, :])
        o_ref[rs, :] = y * lax.rsqrt(jnp.mean(y * y, axis=1, keepdims=True) + RMS_EPS) * nw_ref[...]


def _pad_lanes(v, offset):
    return jnp.zeros((1, LANES), F32).at[0, offset:offset + v.shape[0]].set(v.astype(F32))


def _ssd(proj3, conv_w, conv_b, dt_bias, a_log, d_skip, norm_w):
    B, S, _ = proj3.shape
    L = SSD_TILE
    N2 = 2 * SSD_STATE
    full = lambda shape: pl.BlockSpec(shape, lambda b, c: (0, 0))
    t = np.arange(L)
    tri = (t[None, :] <= t[:, None]) & (t[:, None] // SSD_CHUNK == t[None, :] // SSD_CHUNK)
    expand = np.arange(W_MIX)[None, :] // HEAD_DIM == np.arange(LANES)[:, None]
    return pl.pallas_call(
        functools.partial(_ssd_kernel, tl=L),
        out_shape=jax.ShapeDtypeStruct((B, S, W_MIX), F32),
        grid=(B, S // L),
        in_specs=[pl.BlockSpec((None, L, W_MIX), lambda b, c: (b, c, C_SSD_X // W_MIX)),
                  pl.BlockSpec((None, L, W_MIX), lambda b, c: (b, c, C_SSD_Z // W_MIX)),
                  pl.BlockSpec((None, L, N2), lambda b, c: (b, c, C_SSD_B // N2)),
                  pl.BlockSpec((None, L, N2), lambda b, c: (b, c, C_SSD_C // N2)),
                  pl.BlockSpec((None, L, LANES), lambda b, c: (b, c, C_SMALL_A // LANES)),
                  full((CONV_WIDTH, SSD_XBC)), full((1, SSD_XBC)), full((1, LANES)), full((1, LANES)),
                  full((1, W_MIX)), full((1, W_MIX)), full((L, L)), full((LANES, W_MIX))],
        out_specs=pl.BlockSpec((None, L, W_MIX), lambda b, c: (b, c, 0)),
        scratch_shapes=[pltpu.VMEM((TAIL, W_MIX), F32), pltpu.VMEM((TAIL, N2), F32),
                        pltpu.VMEM((TAIL, N2), F32), pltpu.VMEM((SSD_STATE, W_MIX), F32)],
        compiler_params=pltpu.CompilerParams(
            dimension_semantics=("parallel", "arbitrary"), vmem_limit_bytes=VMEM_LIMIT),
        name="ssd",
    )(proj3, proj3, proj3, proj3, proj3, conv_w, conv_b.reshape(1, -1),
      _pad_lanes(dt_bias, SM_DT), _pad_lanes(a_log, SM_DT),
      jnp.repeat(d_skip.astype(F32), HEAD_DIM).reshape(1, -1), norm_w.reshape(1, -1),
      jnp.asarray(tri, BF16), jnp.asarray(expand, BF16))


INV_LEAF = 16


def _pair_blockdiag(x):
    lo = lax.broadcasted_iota(jnp.int32, (1, LANES), 1) < HEAD_DIM
    xb = x.astype(BF16)
    zero = jnp.zeros_like(xb)
    return jnp.concatenate([jnp.where(lo, xb, zero), jnp.where(lo, zero, xb)], axis=0)


def _pair_mm(a, b):
    return jnp.dot(a.astype(BF16), _pair_blockdiag(b), preferred_element_type=F32)


def _pair_unit_lower_inverse(lms, eye, on_diag_block):
    mm = lambda xs, ys: [_pair_mm(x, y) for x, y in zip(xs, ys)]
    d = [jnp.where(on_diag_block, lm, 0.0) for lm in lms]
    off = [lm - x for lm, x in zip(lms, d)]
    d2 = mm(d, d)
    d4 = mm(d2, d2)
    d8 = mm(d4, d4)
    left = mm([eye - x for x in d], [eye + x for x in d2])
    right = mm([eye + x for x in d4], [eye + x for x in d8])
    dinv = mm(left, right)
    m = mm(dinv, off)
    m2 = mm(m, m)
    return mm(mm([eye - x for x in m], [eye + x for x in m2]), dinv)


def _split3(x):
    hi = x.astype(BF16)
    r = x - hi.astype(F32)
    mid = r.astype(BF16)
    lo = (r - mid.astype(F32)).astype(BF16)
    return hi, mid, lo


def _xdot(a01, x):
    a = a01.astype(BF16)
    return sum(jnp.dot(a, t, preferred_element_type=F32) for t in _split3(x))


def _xdot_r(x, b01):
    b = b01.astype(BF16)
    return sum(jnp.dot(t, b, preferred_element_type=F32) for t in _split3(x))


def _gdn_kernel(qkv_ref, g_ref, sm_ref, cw_ref, cb_ref, dtb_ref, alog_ref, nw_ref, same_ref, tri_ref, eb_ref, ed_ref,
                o_ref, tail_ref, st_ref, *, tl):
    C = DELTA_CHUNK

    @pl.when(pl.program_id(1) == 0)
    def _():
        tail_ref[...] = jnp.zeros_like(tail_ref)
        st_ref[...] = jnp.zeros_like(st_ref)

    qkv = _causal_conv_silu(qkv_ref, tail_ref[...], cw_ref[...], cb_ref[...])
    tail_ref[...] = qkv_ref[tl - TAIL:, :]
    q, k, v = qkv[:, :W_MIX], qkv[:, W_MIX:2 * W_MIX], qkv[:, 2 * W_MIX:]

    same_head = same_ref[...]
    q = q * lax.rsqrt(_bdot(q * q, same_head) + RMS_EPS) * ATTN_SCALE
    k = k * lax.rsqrt(_bdot(k * k, same_head) + RMS_EPS)

    sm = sm_ref[...]
    beta = jax.nn.sigmoid(sm)
    gdec = -jnp.exp(alog_ref[...]) * _softplus(sm + dtb_ref[...])
    gc = _xdot(tri_ref[...], gdec)
    beta_w = _xdot_r(beta, eb_ref[...])
    gc_w = _xdot_r(gc, ed_ref[...])

    kb = k * beta_w
    vb = v * beta_w
    egc = jnp.exp(gc_w)
    kbe = kb * egc
    qe = q * egc

    row = lax.broadcasted_iota(jnp.int32, (C, LANES), 0)
    col = lax.broadcasted_iota(jnp.int32, (C, LANES), 1) % HEAD_DIM
    eye = (row == col).astype(F32)
    incl = col <= row
    strict = col < row
    on_diag_block = (row // INV_LEAF) == (col // INV_LEAF)
    s_row = lax.broadcasted_iota(jnp.int32, (LANES, LANES), 0)
    s_col = lax.broadcasted_iota(jnp.int32, (LANES, LANES), 1)
    pair_diag = (s_row // HEAD_DIM) == (s_col // HEAD_DIM)

    tiles = [(slice(c * C, (c + 1) * C), slice(p * LANES, (p + 1) * LANES))
             for c in range(tl // C) for p in range(N_PAIRS)]
    grams, decays = [], []
    for rows, cols in tiles:
        gcol = gc_w[rows, cols]
        grow = jnp.sum(gcol * eye, axis=0, keepdims=True)
        decays.append(jnp.exp(jnp.where(incl, gcol - grow, NEG)))
        grams.append(lax.dot_general(jnp.concatenate([kb[rows, cols], q[rows, cols]], axis=0).astype(BF16),
                                     _pair_blockdiag(k[rows, cols]), (((1,), (1,)), ((), ())),
                                     preferred_element_type=F32))
    lms = [jnp.where(strict, g[:C] * d, 0.0) for g, d in zip(grams, decays)]
    attns = [jnp.where(incl, g[C:] * d, 0.0) for g, d in zip(grams, decays)]
    t_invs = _pair_unit_lower_inverse(lms, eye, on_diag_block)
    uws = [jnp.dot(t.astype(BF16),
                   jnp.concatenate([_pair_blockdiag(vb[rows, cols]), _pair_blockdiag(kbe[rows, cols])], axis=1),
                   preferred_element_type=F32) for t, (rows, cols) in zip(t_invs, tiles)]
    u2s = [uw[:, :LANES] for uw in uws]
    w2s = [uw[:, LANES:] for uw in uws]
    g_lasts = [gc_w[rows.stop - 1:rows.stop, cols] for rows, cols in tiles]
    k_decs = [k[rows, cols] * jnp.exp(gl - gc_w[rows, cols]) for (rows, cols), gl in zip(tiles, g_lasts)]
    kd_uw = [_bdot_tn(kd, jnp.concatenate([u2, w2], axis=1)) for kd, u2, w2 in zip(k_decs, u2s, w2s)]
    kd_u = [jnp.where(pair_diag, x[:, :LANES], 0.0) for x in kd_uw]
    kd_w = [jnp.where(pair_diag, x[:, LANES:], 0.0).astype(BF16) for x in kd_uw]

    states = [[st_ref[p] for p in range(N_PAIRS)]]
    for c in range(tl // C):
        nxt = []
        for p in range(N_PAIRS):
            t = c * N_PAIRS + p
            s_in = states[c][p]
            nxt.append(s_in * jnp.exp(g_lasts[t]) + kd_u[t]
                       - jnp.dot(kd_w[t], s_in.astype(BF16), preferred_element_type=F32))
        states.append(nxt)
    for p in range(N_PAIRS):
        st_ref[p] = states[tl // C][p]

    sbs = [states[t // N_PAIRS][t % N_PAIRS].astype(BF16) for t in range(len(tiles))]
    v_news = [u2 - jnp.dot(w2.astype(BF16), sb, preferred_element_type=F32) for u2, w2, sb in zip(u2s, w2s, sbs)]
    o2s = [jnp.dot(qe[rows, cols].astype(BF16), sb, preferred_element_type=F32) + _pair_mm(attn, v_new)
           for (rows, cols), sb, attn, v_new in zip(tiles, sbs, attns, v_news)]
    for c in range(tl // C):
        o_ref[c * C:(c + 1) * C, :] = jnp.concatenate(o2s[c * N_PAIRS:(c + 1) * N_PAIRS], axis=1)

    o = o_ref[...]
    o = o * lax.rsqrt(_bdot(o * o, same_head) * (1.0 / HEAD_DIM) + RMS_EPS) * nw_ref[...]
    o_ref[...] = o * _silu(g_ref[...])


def _gdn(proj3, conv_w, conv_b, dt_bias, a_log, norm_w):
    B, S, _ = proj3.shape
    tl = 512
    full = lambda shape: pl.BlockSpec(shape, lambda b, c: (0, 0))
    t = np.arange(tl)
    tri = (t[None, :] <= t[:, None]) & (t[:, None] // DELTA_CHUNK == t[None, :] // DELTA_CHUNK)
    head_of = np.arange(W_MIX) // HEAD_DIM
    same = head_of[:, None] == head_of[None, :]
    lanes = np.arange(LANES)[:, None]
    return pl.pallas_call(
        functools.partial(_gdn_kernel, tl=tl),
        out_shape=jax.ShapeDtypeStruct((B, S, W_MIX), F32),
        grid=(B, S // tl),
        in_specs=[pl.BlockSpec((None, tl, 3 * W_MIX), lambda b, c: (b, c, C_DN_QKV // (3 * W_MIX))),
                  pl.BlockSpec((None, tl, W_MIX), lambda b, c: (b, c, C_DN_G // W_MIX)),
                  pl.BlockSpec((None, tl, LANES), lambda b, c: (b, c, C_SMALL_B // LANES)),
                  full((CONV_WIDTH, 3 * W_MIX)), full((1, 3 * W_MIX)), full((1, LANES)), full((1, LANES)),
                  full((1, W_MIX)), full((W_MIX, W_MIX)), full((tl, tl)), full((LANES, W_MIX)), full((LANES, W_MIX))],
        out_specs=pl.BlockSpec((None, tl, W_MIX), lambda b, c: (b, c, 0)),
        scratch_shapes=[pltpu.VMEM((TAIL, 3 * W_MIX), F32), pltpu.VMEM((N_PAIRS, LANES, LANES), F32)],
        compiler_params=pltpu.CompilerParams(
            dimension_semantics=("parallel", "arbitrary"), vmem_limit_bytes=VMEM_LIMIT),
        name="gdn",
    )(proj3, proj3, proj3, conv_w, conv_b.reshape(1, -1),
      _pad_lanes(dt_bias, SM_DECAY), _pad_lanes(a_log, SM_DECAY), jnp.tile(norm_w.astype(F32), N_HEADS).reshape(1, -1),
      jnp.asarray(same, BF16), jnp.asarray(tri, BF16),
      jnp.asarray(head_of[None, :] + SM_BETA == lanes, BF16), jnp.asarray(head_of[None, :] + SM_DECAY == lanes, BF16))


def _outproj_kernel(x_ref, ya_ref, yb_ref, yc_ref, yd_ref, w_ref, lg_ref, lb_ref, o_ref):
    mix = (_bdot(ya_ref[...], w_ref[0]) + _bdot(yb_ref[...], w_ref[1])
           + _bdot(yc_ref[...], w_ref[2]) + _bdot(yd_ref[...], w_ref[3]))
    r = DEEPNORM_ALPHA * x_ref[...] + mix
    mu = jnp.mean(r, axis=1, keepdims=True)
    var = jnp.mean(jnp.square(r - mu), axis=1, keepdims=True)
    o_ref[...] = (r - mu) * lax.rsqrt(var + LN_EPS) * lg_ref[...] + lb_ref[...]


def _outproj(x2d, ya, yb, yc, yd, w_out, ln_g, ln_b):
    T = x2d.shape[0]
    tm = min(1024, T)
    rowblk = lambda width: pl.BlockSpec((tm, width), lambda i: (i, 0))
    return pl.pallas_call(
        _outproj_kernel,
        out_shape=jax.ShapeDtypeStruct((T, D_MODEL), F32),
        grid=(T // tm,),
        in_specs=[rowblk(D_MODEL), rowblk(W_MIX), rowblk(W_MIX), rowblk(W_MIX), rowblk(W_MIX),
                  pl.BlockSpec((4, W_MIX, D_MODEL), lambda i: (0, 0, 0)),
                  pl.BlockSpec((1, D_MODEL), lambda i: (0, 0)), pl.BlockSpec((1, D_MODEL), lambda i: (0, 0))],
        out_specs=rowblk(D_MODEL),
        compiler_params=pltpu.CompilerParams(
            dimension_semantics=("parallel",), vmem_limit_bytes=VMEM_LIMIT),
        name="outproj",
    )(x2d, ya, yb, yc, yd, w_out, ln_g.reshape(1, -1), ln_b.reshape(1, -1))


def _alibi_slopes():
    n = 2 * N_HEADS
    s = (2.0 ** (-8.0 * (np.arange(n) + 1) / n)).astype(np.float32)
    return jnp.asarray(s[:N_HEADS]), jnp.asarray(s[N_HEADS:])


def _layer(x, w_in_t, layer, w_out, ssm_conv_w, ssm_conv_b, ssm_dt_bias, ssm_A_log, ssm_D, ssm_norm_w,
           dn_conv_w, dn_conv_b, dn_dt_bias, dn_A_log, dn_norm_w, ln_g, ln_b):
    B, S, _ = x.shape
    T = B * S
    slopes_dil, slopes_moba = _alibi_slopes()
    x2d = x.reshape(T, D_MODEL)
    proj2d = _inproj(x2d, w_in_t, layer)
    proj3 = proj2d.reshape(B, S, PROJ_COLS)

    ya = _moba(proj3, slopes_moba)
    yb = _ssd(proj3, ssm_conv_w, ssm_conv_b, ssm_dt_bias, ssm_A_log, ssm_D, ssm_norm_w)
    yd = _gdn(proj3, dn_conv_w, dn_conv_b, dn_dt_bias, dn_A_log, dn_norm_w)

    yc = _dilated(proj3, slopes_dil)

    out = _outproj(x2d, ya.reshape(T, W_MIX), yb.reshape(T, W_MIX), yc.reshape(T, W_MIX), yd.reshape(T, W_MIX),
                   w_out.reshape(4, W_MIX, D_MODEL).astype(BF16), ln_g, ln_b)
    return out.reshape(B, S, D_MODEL)


def kernel(x, w_in, w_out, ssm_conv_w, ssm_conv_b, ssm_dt_bias, ssm_A_log, ssm_D, ssm_norm_w,
           dn_conv_w, dn_conv_b, dn_dt_bias, dn_A_log, dn_norm_w, ln_g, ln_b):
    assert x.shape[1] % 2048 == 0 and x.shape[2] == D_MODEL
    w_in_t = _repack_w_in(jnp.transpose(w_in, (2, 0, 1)))
    for l in range(DEPTH):
        x = _layer(x, w_in_t, l, w_out[l], ssm_conv_w[l], ssm_conv_b[l], ssm_dt_bias[l], ssm_A_log[l],
                   ssm_D[l], ssm_norm_w[l], dn_conv_w[l], dn_conv_b[l], dn_dt_bias[l], dn_A_log[l],
                   dn_norm_w[l], ln_g[l], ln_b[l])
    return x
```

```python
import functools
import math

import numpy as np
import jax
import jax.numpy as jnp
from jax import lax
from jax.experimental import pallas as pl
from jax.experimental.pallas import tpu as pltpu

F32 = jnp.float32
BF16 = jnp.bfloat16
HIGHEST = lax.Precision.HIGHEST

D_MODEL = 1024
DEPTH = 2
HEAD_DIM = 64
N_HEADS = 6
W_MIX = N_HEADS * HEAD_DIM
N_PAIRS = W_MIX // 128
D_MIX = 4 * W_MIX
MOBA_BLOCK = 256
MOBA_TOPK = 3
MOBA_NFEAT = 3
MOBA_UNROLL = 2
MOBA_QBLOCKS = 4
MOBA_LROWS = 16
SSD_STATE = 128
SSD_GROUPS = 2
SSD_CHUNK = 128
SSD_TILE = 512
SSD_XBC = W_MIX + 2 * SSD_GROUPS * SSD_STATE
CONV_WIDTH = 4
DIL_DILATIONS = (1, 4, 16)
DIL_GROUPS = len(DIL_DILATIONS)
DIL_BATCH = 16
DIL_MERGE_ROWS = 256
DIL_SPAN = 128
DELTA_CHUNK = 64
LN_EPS = 1e-5
RMS_EPS = 1e-6
DEEPNORM_ALPHA = (2.0 * DEPTH) ** 0.25
ATTN_SCALE = HEAD_DIM ** -0.5
LOG2E = math.log2(math.e)
NEG = -1e30

LANES = 128
TAIL = 8
VMEM_LIMIT = 48 * 1024 * 1024

INPROJ_ROWS = 1024
INPROJ_COLS = 2048
OUTPROJ_ROWS = 1024
GDN_TILE = 512

C_MOBA = 0
C_SSD_X = 1536
C_SSD_Z = 1920
C_DN_QKV = 2304
C_DN_G = 3456
C_DIL = 3840
C_SSD_B = 5376
C_SSD_C = 5632
C_SMALL_A = 5888
C_SMALL_B = 6016
PROJ_COLS = 6144
SM_DT, SM_BETA, SM_DECAY = 0, 116, 122

_R = [int(v) for v in np.cumsum([0, 384, 384, 384, 384, SSD_XBC, 384, 6, 384, 384, 384, 384, 1152, 384, 6, 6])]
R_SSD_XBC, R_SSD_Z, R_SSD_DT, R_DIL, R_DN_QKV, R_DN_G, R_END = _R[4], _R[5], _R[6], _R[7], _R[11], _R[12], _R[15]
_SEGMENTS = ((C_MOBA, 0, 4 * W_MIX), (C_SSD_X, R_SSD_XBC, W_MIX), (C_SSD_Z, R_SSD_Z, W_MIX),
             (C_DN_QKV, R_DN_QKV, 3 * W_MIX), (C_DN_G, R_DN_G, W_MIX), (C_DIL, R_DIL, 4 * W_MIX),
             (C_SSD_B, R_SSD_XBC + W_MIX, 2 * SSD_GROUPS * SSD_STATE), (C_SMALL_A, R_SSD_DT, LANES),
             (C_SMALL_B, R_END - LANES, LANES))
_BLOCK_SRC = np.zeros(PROJ_COLS // LANES, np.int32)
for _dst, _src, _n in _SEGMENTS:
    _BLOCK_SRC[_dst // LANES:(_dst + _n) // LANES] = _src + LANES * np.arange(_n // LANES)
assert R_SSD_DT + LANES <= R_END


def _repack_kernel(src_ref, w_ref, o_ref):
    for layer in range(o_ref.shape[0]):
        o_ref[layer] = w_ref[:, layer, :].astype(BF16)


def _repack_w_in(w_t):
    _, depth, dm = w_t.shape
    return pl.pallas_call(
        _repack_kernel,
        out_shape=jax.ShapeDtypeStruct((depth, PROJ_COLS, dm), BF16),
        grid_spec=pltpu.PrefetchScalarGridSpec(
            num_scalar_prefetch=1, grid=(PROJ_COLS // LANES,),
            in_specs=[pl.BlockSpec((pl.Element(LANES), pl.Element(depth), pl.Element(dm)),
                                   lambda b, src: (src[b], 0, 0))],
            out_specs=pl.BlockSpec((depth, LANES, dm), lambda b, src: (0, b, 0))),
        compiler_params=pltpu.CompilerParams(dimension_semantics=("parallel",), vmem_limit_bytes=VMEM_LIMIT),
        name="repack_w_in",
    )(jnp.asarray(_BLOCK_SRC), w_t)


def _silu(x):
    return x * jax.nn.sigmoid(x)


def _softplus(x):
    return jnp.maximum(x, 0.0) + jnp.log(1.0 + jnp.exp(-jnp.abs(x)))


def _bdot(a, b):
    return jnp.dot(a.astype(BF16), b.astype(BF16), preferred_element_type=F32)


def _bdot_nt(a, b):
    return lax.dot_general(a.astype(BF16), b.astype(BF16), (((1,), (1,)), ((), ())),
                           preferred_element_type=F32)


def _bdot_tn(a, b):
    return lax.dot_general(a.astype(BF16), b.astype(BF16), (((0,), (0,)), ((), ())),
                           preferred_element_type=F32)


def _causal_conv_silu(x_ref, tail, w, b):
    L = x_ref.shape[0]
    head = jnp.concatenate([tail, x_ref[:TAIL, :]], axis=0)
    y_head, y_rest = b, b
    for j in range(CONV_WIDTH):
        off = TAIL - (CONV_WIDTH - 1) + j
        y_head = y_head + w[j:j + 1, :] * head[off:off + TAIL, :]
        y_rest = y_rest + w[j:j + 1, :] * x_ref[pl.ds(off, L - TAIL), :]
    return _silu(jnp.concatenate([y_head, y_rest], axis=0))


def _inproj_kernel(x_ref, w_ref, o_ref, xb_ref):
    @pl.when(pl.program_id(1) == 0)
    def _():
        xb_ref[...] = x_ref[...].astype(BF16)

    o_ref[...] = lax.dot_general(xb_ref[...], w_ref[...], (((1,), (1,)), ((), ())), preferred_element_type=F32)


def _inproj(x2d, w_t, layer):
    T = x2d.shape[0]
    tm = min(INPROJ_ROWS, T)
    tn = INPROJ_COLS
    return pl.pallas_call(
        _inproj_kernel,
        out_shape=jax.ShapeDtypeStruct((T, PROJ_COLS), F32),
        grid=(T // tm, PROJ_COLS // tn),
        in_specs=[pl.BlockSpec((tm, D_MODEL), lambda i, j: (i, 0)),
                  pl.BlockSpec((None, tn, D_MODEL), lambda i, j: (layer, j, 0))],
        out_specs=pl.BlockSpec((tm, tn), lambda i, j: (i, j)),
        scratch_shapes=[pltpu.VMEM((tm, D_MODEL), BF16)],
        compiler_params=pltpu.CompilerParams(
            dimension_semantics=("parallel", "arbitrary"), vmem_limit_bytes=VMEM_LIMIT),
        name="inproj",
    )(x2d, w_t)


def _moba_kernel(slopes_ref, q_ref, k_ref, v_ref, g_ref, o_ref, kaug_ref, vt_ref, kmean_ref, s_ref, *, nblk):
    p = pl.program_id(1)
    L = MOBA_BLOCK
    NF = MOBA_NFEAT

    @pl.when(pl.program_id(2) == 0)
    def _():
        lane = lax.broadcasted_iota(jnp.int32, (L, LANES), 1)
        feat_lane = lane % HEAD_DIM
        c_loc = lax.broadcasted_iota(jnp.int32, (L, LANES), 0).astype(F32)
        feat = jnp.where(feat_lane < NF, 1.0, jnp.where(feat_lane < 2 * NF, c_loc, 0.0))
        ones = jnp.ones((MOBA_LROWS, L), F32)
        for n in range(nblk):
            kblk = k_ref[n * L:(n + 1) * L, :]
            kmean_ref[n:n + 1, :] = jnp.mean(kblk, axis=0, keepdims=True)
            kaug_ref[0, n] = jnp.where(lane < HEAD_DIM, kblk, feat).astype(BF16)
            kaug_ref[1, n] = jnp.where(lane < HEAD_DIM, feat, kblk).astype(BF16)
            vt = v_ref[n * L:(n + 1) * L, :].T
            for h in range(2):
                vt_ref[h, n] = jnp.concatenate([ones, vt[h * HEAD_DIM:(h + 1) * HEAD_DIM, :]], axis=0).astype(BF16)

    QB = MOBA_QBLOCKS
    streams = [(qb, h) for qb in range(QB) for h in range(2)]
    i0 = QB * pl.program_id(2)
    qt_all = q_ref[...].T
    qt = [qt_all[:, qb * L:(qb + 1) * L] for qb in range(QB)]
    qrow = lax.broadcasted_iota(jnp.int32, (LANES, L), 0)
    r_loc = lax.broadcasted_iota(jnp.int32, (1, L), 1).astype(F32)
    brow = lax.broadcasted_iota(jnp.int32, (nblk, L), 0).astype(F32)
    frow = lax.broadcasted_iota(jnp.int32, (2 * TAIL, L), 0)
    kmean = kmean_ref[...]
    i_f = [(i0 + qb).astype(F32) for qb in range(QB)]

    qparts = [(qt[qb][h * HEAD_DIM:(h + 1) * HEAD_DIM, :] * (ATTN_SCALE * LOG2E)).astype(BF16) for qb, h in streams]
    slope_rows = [jnp.full((1, L), slopes_ref[2 * p + h] * LOG2E, F32) for h in range(2)]
    zeros48 = jnp.zeros((HEAD_DIM - 2 * TAIL, L), BF16)

    def q_aug(st, t_row):
        h = streams[st][1]
        terms = _split3(t_row) + _split3(slope_rows[h])
        feat = jnp.zeros((2 * TAIL, L), F32)
        for f, term in enumerate(terms):
            feat = jnp.where(frow == f, term.astype(F32), feat)
        feat = feat.astype(BF16)
        if h == 0:
            return jnp.concatenate([qparts[st], feat, zeros48], axis=0)
        return jnp.concatenate([feat, zeros48, qparts[st]], axis=0)

    def past_row(st, n, in_range=None):
        qb, h = streams[st]
        n_f = n.astype(F32)
        picked = (sels[st][0] == n_f) | (sels[st][1] == n_f) | (sels[st][2] == n_f)
        if in_range is not None:
            picked = picked & in_range
        return jnp.where(picked, -slope_rows[h] * (r_loc + L * (i_f[qb] - n_f)), NEG)

    U = MOBA_UNROLL
    NS = len(streams)

    def put_scores(slot, n0):
        for u in range(U):
            n = n0 + u
            for st, (qb, h) in enumerate(streams):
                s_ref[slot, st, u] = jnp.dot(kaug_ref[h, jnp.minimum(n, nblk - 1)], q_aug(st, past_row(st, n, n < i0)),
                                             preferred_element_type=F32).astype(BF16)

    def new_max(slot, carry):
        return [functools.reduce(jnp.maximum, [carry[2 * st]] + [
            jnp.max(s_ref[slot, st, u], axis=0, keepdims=True).astype(F32) for u in range(U)]) for st in range(NS)]

    def absorb(slot, n0, m_new, carry):
        ps = [jnp.concatenate([jnp.exp2(s_ref[slot, st, u] - m_new[st].astype(BF16)) for u in range(U)], axis=0)
              for st in range(NS)]
        vts = [jnp.concatenate([vt_ref[h, jnp.minimum(n0 + u, nblk - 1)] for u in range(U)], axis=1) for h in range(2)]
        pv = [jnp.dot(vts[streams[st][1]], ps[st], preferred_element_type=F32) for st in range(NS)]
        out = []
        for st in range(NS):
            out += [m_new[st], jnp.exp2(carry[2 * st] - m_new[st]) * carry[2 * st + 1] + pv[st]]
        return tuple(out)

    c_idx = lax.broadcasted_iota(jnp.int32, (L, L), 0)
    r_idx = lax.broadcasted_iota(jnp.int32, (L, L), 1)
    own = [jnp.where(c_idx <= r_idx,
                     jnp.dot(kaug_ref[h, i0 + qb], q_aug(st, -slope_rows[h] * r_loc), preferred_element_type=F32),
                     NEG).astype(BF16) for st, (qb, h) in enumerate(streams)]

    gates = [jnp.where(brow < i_f[qb],
                       jnp.dot(kmean, jnp.where((qrow >= HEAD_DIM * h) & (qrow < HEAD_DIM * (h + 1)), qt[qb], 0.0),
                               precision=HIGHEST, preferred_element_type=F32), -jnp.inf) for qb, h in streams]
    sels = [[] for _ in streams]
    for kk in range(MOBA_TOPK):
        for st, (qb, h) in enumerate(streams):
            mx = jnp.max(gates[st], axis=0, keepdims=True)
            idx = jnp.min(jnp.where(gates[st] == mx, brow, float(nblk)), axis=0, keepdims=True)
            sels[st].append(jnp.where(kk < i0 + qb, idx, -1.0))
            gates[st] = jnp.where(brow == idx, -jnp.inf, gates[st])

    init = []
    for st, (qb, h) in enumerate(streams):
        tiles = [jnp.dot(kaug_ref[h, i0 + e], q_aug(st, past_row(st, i0 + e)),
                         preferred_element_type=F32).astype(BF16) for e in range(qb)] + [own[st]]
        m0 = functools.reduce(jnp.maximum, [jnp.max(s, axis=0, keepdims=True) for s in tiles])
        p0 = jnp.concatenate([jnp.exp2(s - m0) for s in tiles], axis=0)
        vt0 = jnp.concatenate([vt_ref[h, i0 + e] for e in range(qb + 1)], axis=1)
        init += [m0.astype(F32), jnp.dot(vt0, p0, preferred_element_type=F32)]

    put_scores(0, jnp.int32(0))

    def body(j, carry):
        n0 = 2 * U * j
        m_a = new_max(0, carry)
        put_scores(1, n0 + U)
        carry = absorb(0, n0, m_a, carry)
        m_b = new_max(1, carry)
        put_scores(0, n0 + 2 * U)
        return absorb(1, n0 + U, m_b, carry)

    res = lax.fori_loop(0, (i0 + 2 * U - 1) // (2 * U), body, tuple(init))
    for qb in range(QB):
        out_t = jnp.concatenate([res[2 * st + 1][MOBA_LROWS:] / res[2 * st + 1][0:1]
                                 for st in (2 * qb, 2 * qb + 1)], axis=0)
        o_ref[qb * L:(qb + 1) * L, :] = out_t.T * _silu(g_ref[qb * L:(qb + 1) * L, :])


def _moba(proj3, slopes):
    B, S, _ = proj3.shape
    L = MOBA_BLOCK
    nblk = S // L
    QB = MOBA_QBLOCKS
    cb = C_MOBA // LANES
    return pl.pallas_call(
        functools.partial(_moba_kernel, nblk=nblk),
        out_shape=jax.ShapeDtypeStruct((B, S, W_MIX), F32),
        grid=(B, N_PAIRS, nblk // QB),
        in_specs=[pl.BlockSpec(memory_space=pltpu.SMEM),
                  pl.BlockSpec((None, QB * L, LANES), lambda b, p, i: (b, i, cb + p)),
                  pl.BlockSpec((None, S, LANES), lambda b, p, i: (b, 0, cb + N_PAIRS + p)),
                  pl.BlockSpec((None, S, LANES), lambda b, p, i: (b, 0, cb + 2 * N_PAIRS + p)),
                  pl.BlockSpec((None, QB * L, LANES), lambda b, p, i: (b, i, cb + 3 * N_PAIRS + p))],
        out_specs=pl.BlockSpec((None, QB * L, LANES), lambda b, p, i: (b, i, p)),
        scratch_shapes=[pltpu.VMEM((2, nblk, L, LANES), BF16),
                        pltpu.VMEM((2, nblk, MOBA_LROWS + HEAD_DIM, L), BF16),
                        pltpu.VMEM((nblk, LANES), F32),
                        pltpu.VMEM((2, 2 * QB, MOBA_UNROLL, L, L), BF16)],
        compiler_params=pltpu.CompilerParams(
            dimension_semantics=("parallel", "parallel", "arbitrary"), vmem_limit_bytes=VMEM_LIMIT),
        name="moba",
    )(slopes, proj3, proj3, proj3, proj3)


def _dil_kernel(slopes_ref, q_ref, kc_ref, kp_ref, vc_ref, vp_ref, g_ref, o_ref, acc_ref, m_ref, l_ref, *, tq):
    p = pl.program_id(1)
    first = pl.program_id(2) == 0
    SB = DIL_SPAN
    lane = lax.broadcasted_iota(jnp.int32, (1, LANES), 1)
    lo = lane < HEAD_DIM
    row = lax.broadcasted_iota(jnp.int32, (SB, 2 * SB), 0)
    col = lax.broadcasted_iota(jnp.int32, (SB, 2 * SB), 1)
    dist = (row + SB - col).astype(F32)
    band = (col >= row) & (col <= row + SB)
    no_halo = jnp.logical_and(first, col < SB)

    def rows(start, d):
        return pl.ds(start, SB) if d == 1 else pl.ds(start, SB, stride=d)

    for gi, d in enumerate(DIL_DILATIONS):
        span = d * SB
        bias, bias_first = [], []
        for h in range(2):
            slope = slopes_ref[2 * p + h] * (float(d) * LOG2E)
            bias.append(jnp.where(band, -slope * dist, NEG))
            bias_first.append(jnp.where(no_halo, NEG, bias[h]))
        problems = [(r, j) for j in range(tq // span) for r in range(d)]
        for c0 in range(0, len(problems), DIL_BATCH):
            chunk = problems[c0:c0 + DIL_BATCH]
            blocks = {}

            def block(r, j):
                if (r, j) not in blocks:
                    if j < 0:
                        sel = rows(tq - span + r, d)
                        kb, vb = kp_ref[sel, :], vp_ref[sel, :]
                    else:
                        sel = rows(r + span * j, d)
                        kb, vb = kc_ref[sel, :], vc_ref[sel, :]
                    blocks[(r, j)] = (kb.astype(BF16),
                                      (jnp.where(lo, vb, 1.0).astype(BF16), jnp.where(lo, 1.0, vb).astype(BF16)))
                return blocks[(r, j)]

            kv, qs = [], []
            for r, j in chunk:
                (kp, vp), (kc, vc) = block(r, j - 1), block(r, j)
                kv.append((jnp.concatenate([kp, kc], axis=0),
                           tuple(jnp.concatenate([vp[h], vc[h]], axis=0) for h in range(2))))
                qs.append(q_ref[rows(r + span * j, d), :] * (ATTN_SCALE * LOG2E))
            work = [(c, h) for c in range(len(chunk)) for h in range(2)]
            ss = [(_bdot_nt(jnp.where(lo if h == 0 else jnp.logical_not(lo), qs[c], 0.0), kv[c][0])
                   + (bias_first[h] if chunk[c][1] == 0 else bias[h])).astype(BF16) for c, h in work]
            ms = [jnp.max(s, axis=1, keepdims=True) for s in ss]
            ps = [jnp.exp2(s - m) for s, m in zip(ss, ms)]
            pvs = [jnp.dot(e, kv[c][1][h], preferred_element_type=F32) for e, (c, h) in zip(ps, work)]
            for c, (r, j) in enumerate(chunk):
                cur = rows(r + span * j, d)
                acc_ref[gi, cur, :] = jnp.where(lo, pvs[2 * c], pvs[2 * c + 1])
                m_ref[gi, cur, :] = jnp.where(lo, ms[2 * c].astype(F32), ms[2 * c + 1].astype(F32))
                l_ref[gi, cur, :] = pltpu.roll(jnp.where(lo, pvs[2 * c + 1], pvs[2 * c]), HEAD_DIM, axis=1)

    for c0 in range(0, tq, DIL_MERGE_ROWS):
        rs = slice(c0, c0 + DIL_MERGE_ROWS)
        ms = [m_ref[gi, rs, :] for gi in range(DIL_GROUPS)]
        top = functools.reduce(jnp.maximum, ms)
        ws = [jnp.exp2(m - top) for m in ms]
        num = sum(w * acc_ref[gi, rs, :] for gi, w in enumerate(ws))
        den = sum(w * l_ref[gi, rs, :] for gi, w in enumerate(ws))
        o_ref[rs, :] = num / den * _silu(g_ref[rs, :])


def _dilated(proj3, slopes):
    B, S, _ = proj3.shape
    tq = DIL_DILATIONS[-1] * DIL_SPAN
    base = C_DIL // LANES
    cur = lambda off: pl.BlockSpec((None, tq, LANES), lambda b, p, t: (b, t, base + off + p))
    prev = lambda off: pl.BlockSpec((None, tq, LANES), lambda b, p, t: (b, jnp.maximum(t - 1, 0), base + off + p))
    scratch = pltpu.VMEM((DIL_GROUPS, tq, LANES), F32)
    return pl.pallas_call(
        functools.partial(_dil_kernel, tq=tq),
        out_shape=jax.ShapeDtypeStruct((B, S, W_MIX), F32),
        grid=(B, N_PAIRS, S // tq),
        in_specs=[pl.BlockSpec(memory_space=pltpu.SMEM),
                  cur(0), cur(N_PAIRS), prev(N_PAIRS), cur(2 * N_PAIRS), prev(2 * N_PAIRS), cur(3 * N_PAIRS)],
        out_specs=pl.BlockSpec((None, tq, LANES), lambda b, p, t: (b, t, p)),
        scratch_shapes=[scratch, scratch, scratch],
        compiler_params=pltpu.CompilerParams(
            dimension_semantics=("parallel", "parallel", "arbitrary"), vmem_limit_bytes=VMEM_LIMIT),
        name="dilated",
    )(slopes, proj3, proj3, proj3, proj3, proj3, proj3)


def _ssd_kernel(x_ref, z_ref, b_ref, c_ref, sm_ref, cw_ref, cb_ref, dtb_ref, alog_ref, dskip_ref, nw_ref,
                tri_ref, expand_ref, o_ref, tx_ref, tb_ref, tc_ref, st_ref, *, tl):
    L = SSD_CHUNK
    N = SSD_STATE
    nchunk = tl // L

    @pl.when(pl.program_id(1) == 0)
    def _():
        tx_ref[...] = jnp.zeros_like(tx_ref)
        tb_ref[...] = jnp.zeros_like(tb_ref)
        tc_ref[...] = jnp.zeros_like(tc_ref)
        st_ref[...] = jnp.zeros_like(st_ref)

    cw = cw_ref[...]
    cb = cb_ref[...]
    xs = _causal_conv_silu(x_ref, tx_ref[...], cw[:, :W_MIX], cb[:, :W_MIX])
    Bm = _causal_conv_silu(b_ref, tb_ref[...], cw[:, W_MIX:W_MIX + 2 * N], cb[:, W_MIX:W_MIX + 2 * N])
    Cm = _causal_conv_silu(c_ref, tc_ref[...], cw[:, W_MIX + 2 * N:], cb[:, W_MIX + 2 * N:])
    tx_ref[...] = x_ref[tl - TAIL:, :]
    tb_ref[...] = b_ref[tl - TAIL:, :]
    tc_ref[...] = c_ref[tl - TAIL:, :]

    dt = _softplus(sm_ref[...] + dtb_ref[...])
    a = dt * (-jnp.exp(alog_ref[...]))
    a_cum = _xdot(tri_ref[...], a)
    expand = expand_ref[...]
    dt_w = _xdot_r(dt, expand)
    acum_w = _xdot_r(a_cum, expand)
    xdt = xs * dt_w
    a_cum_t = a_cum.T
    e_acum = jnp.exp(acum_w)

    lane = lax.broadcasted_iota(jnp.int32, (1, LANES), 1)
    lane_w = lax.broadcasted_iota(jnp.int32, (1, W_MIX), 1)
    in_g0 = lane_w < W_MIX // SSD_GROUPS
    row = lax.broadcasted_iota(jnp.int32, (L, L), 0)
    col = lax.broadcasted_iota(jnp.int32, (L, L), 1)
    causal = col <= row
    chunks = [slice(c * L, (c + 1) * L) for c in range(nchunk)]
    heads_per_group = N_HEADS // SSD_GROUPS

    scores = [[_bdot_nt(Cm[rs, g * N:(g + 1) * N], Bm[rs, g * N:(g + 1) * N]) for g in range(SSD_GROUPS)]
              for rs in chunks]
    atot = [acum_w[rs.stop - 1:rs.stop, :] for rs in chunks]
    xdec = [xdt[rs, :] * jnp.exp(atot[c] - acum_w[rs, :]) for c, rs in enumerate(chunks)]
    contrib = [jnp.where(in_g0, _bdot_tn(Bm[rs, :N], xdec[c]), _bdot_tn(Bm[rs, N:], xdec[c]))
               for c, rs in enumerate(chunks)]
    lmats = [[jnp.exp(jnp.where(causal, a_cum[rs, h:h + 1] - a_cum_t[h:h + 1, rs], NEG)) for h in range(N_HEADS)]
             for rs in chunks]
    diag = [[_bdot(scores[c][h // heads_per_group] * lmats[c][h], xdt[rs, (h // 2) * LANES:(h // 2 + 1) * LANES])
             for h in range(N_HEADS)] for c, rs in enumerate(chunks)]

    states = [st_ref[...]]
    for c in range(nchunk):
        states.append(states[c] * jnp.exp(atot[c]) + contrib[c])
    st_ref[...] = states[nchunk]
    sb = [s.astype(BF16) for s in states[:nchunk]]
    off = [jnp.where(in_g0, jnp.dot(Cm[rs, :N].astype(BF16), sb[c], preferred_element_type=F32),
                     jnp.dot(Cm[rs, N:].astype(BF16), sb[c], preferred_element_type=F32))
           for c, rs in enumerate(chunks)]

    for c, rs in enumerate(chunks):
        y_diag = jnp.concatenate([jnp.where(lane < HEAD_DIM, diag[c][2 * p], diag[c][2 * p + 1])
                                  for p in range(N_PAIRS)], axis=1)
        y = y_diag + off[c] * e_acum[rs, :] + dskip_ref[...] * xs[rs, :]
        y = y * _silu(z_ref[rs, :])
        o_ref[rs, :] = y * lax.rsqrt(jnp.mean(y * y, axis=1, keepdims=True) + RMS_EPS) * nw_ref[...]


def _pad_lanes(v, offset):
    return jnp.zeros((1, LANES), F32).at[0, offset:offset + v.shape[0]].set(v.astype(F32))


def _ssd(proj3, conv_w, conv_b, dt_bias, a_log, d_skip, norm_w):
    B, S, _ = proj3.shape
    L = SSD_TILE
    N2 = 2 * SSD_STATE
    full = lambda shape: pl.BlockSpec(shape, lambda b, c: (0, 0))
    t = np.arange(L)
    tri = (t[None, :] <= t[:, None]) & (t[:, None] // SSD_CHUNK == t[None, :] // SSD_CHUNK)
    expand = np.arange(W_MIX)[None, :] // HEAD_DIM == np.arange(LANES)[:, None]
    return pl.pallas_call(
        functools.partial(_ssd_kernel, tl=L),
        out_shape=jax.ShapeDtypeStruct((B, S, W_MIX), F32),
        grid=(B, S // L),
        in_specs=[pl.BlockSpec((None, L, W_MIX), lambda b, c: (b, c, C_SSD_X // W_MIX)),
                  pl.BlockSpec((None, L, W_MIX), lambda b, c: (b, c, C_SSD_Z // W_MIX)),
                  pl.BlockSpec((None, L, N2), lambda b, c: (b, c, C_SSD_B // N2)),
                  pl.BlockSpec((None, L, N2), lambda b, c: (b, c, C_SSD_C // N2)),
                  pl.BlockSpec((None, L, LANES), lambda b, c: (b, c, C_SMALL_A // LANES)),
                  full((CONV_WIDTH, SSD_XBC)), full((1, SSD_XBC)), full((1, LANES)), full((1, LANES)),
                  full((1, W_MIX)), full((1, W_MIX)), full((L, L)), full((LANES, W_MIX))],
        out_specs=pl.BlockSpec((None, L, W_MIX), lambda b, c: (b, c, 0)),
        scratch_shapes=[pltpu.VMEM((TAIL, W_MIX), F32), pltpu.VMEM((TAIL, N2), F32),
                        pltpu.VMEM((TAIL, N2), F32), pltpu.VMEM((SSD_STATE, W_MIX), F32)],
        compiler_params=pltpu.CompilerParams(
            dimension_semantics=("parallel", "arbitrary"), vmem_limit_bytes=VMEM_LIMIT),
        name="ssd",
    )(proj3, proj3, proj3, proj3, proj3, conv_w, conv_b.reshape(1, -1),
      _pad_lanes(dt_bias, SM_DT), _pad_lanes(a_log, SM_DT),
      jnp.repeat(d_skip.astype(F32), HEAD_DIM).reshape(1, -1), norm_w.reshape(1, -1),
      jnp.asarray(tri, BF16), jnp.asarray(expand, BF16))


INV_LEAF = 16


def _pair_blockdiag(x):
    lo = lax.broadcasted_iota(jnp.int32, (1, LANES), 1) < HEAD_DIM
    xb = x.astype(BF16)
    zero = jnp.zeros_like(xb)
    return jnp.concatenate([jnp.where(lo, xb, zero), jnp.where(lo, zero, xb)], axis=0)


def _pair_mm(a, b):
    return jnp.dot(a.astype(BF16), _pair_blockdiag(b), preferred_element_type=F32)


def _pair_unit_lower_inverse(lms, eye, on_diag_block):
    mm = lambda xs, ys: [_pair_mm(x, y) for x, y in zip(xs, ys)]
    d = [jnp.where(on_diag_block, lm, 0.0) for lm in lms]
    off = [lm - x for lm, x in zip(lms, d)]
    d2 = mm(d, d)
    d4 = mm(d2, d2)
    d8 = mm(d4, d4)
    left = mm([eye - x for x in d], [eye + x for x in d2])
    right = mm([eye + x for x in d4], [eye + x for x in d8])
    dinv = mm(left, right)
    m = mm(dinv, off)
    m2 = mm(m, m)
    return mm(mm([eye - x for x in m], [eye + x for x in m2]), dinv)


def _split3(x):
    hi = x.astype(BF16)
    r = x - hi.astype(F32)
    mid = r.astype(BF16)
    lo = (r - mid.astype(F32)).astype(BF16)
    return hi, mid, lo


def _xdot(a01, x):
    a = a01.astype(BF16)
    return sum(jnp.dot(a, t, preferred_element_type=F32) for t in _split3(x))


def _xdot_r(x, b01):
    b = b01.astype(BF16)
    return sum(jnp.dot(t, b, preferred_element_type=F32) for t in _split3(x))


def _gdn_kernel(qkv_ref, g_ref, sm_ref, cw_ref, cb_ref, dtb_ref, alog_ref, nw_ref, same_ref, tri_ref, eb_ref, ed_ref,
                o_ref, tail_ref, st_ref, *, tl):
    C = DELTA_CHUNK

    @pl.when(pl.program_id(1) == 0)
    def _():
        tail_ref[...] = jnp.zeros_like(tail_ref)
        st_ref[...] = jnp.zeros_like(st_ref)

    qkv = _causal_conv_silu(qkv_ref, tail_ref[...], cw_ref[...], cb_ref[...])
    tail_ref[...] = qkv_ref[tl - TAIL:, :]
    q, k, v = qkv[:, :W_MIX], qkv[:, W_MIX:2 * W_MIX], qkv[:, 2 * W_MIX:]

    same_head = same_ref[...]
    q = q * lax.rsqrt(_bdot(q * q, same_head) + RMS_EPS) * ATTN_SCALE
    k = k * lax.rsqrt(_bdot(k * k, same_head) + RMS_EPS)

    sm = sm_ref[...]
    beta = jax.nn.sigmoid(sm)
    gdec = -jnp.exp(alog_ref[...]) * _softplus(sm + dtb_ref[...])
    gc = _xdot(tri_ref[...], gdec)
    beta_w = _xdot_r(beta, eb_ref[...])
    gc_w = _xdot_r(gc, ed_ref[...])

    kb = k * beta_w
    vb = v * beta_w
    egc = jnp.exp(gc_w)
    kbe = kb * egc
    qe = q * egc

    row = lax.broadcasted_iota(jnp.int32, (C, LANES), 0)
    col = lax.broadcasted_iota(jnp.int32, (C, LANES), 1) % HEAD_DIM
    eye = (row == col).astype(F32)
    incl = col <= row
    strict = col < row
    on_diag_block = (row // INV_LEAF) == (col // INV_LEAF)
    s_row = lax.broadcasted_iota(jnp.int32, (LANES, LANES), 0)
    s_col = lax.broadcasted_iota(jnp.int32, (LANES, LANES), 1)
    pair_diag = (s_row // HEAD_DIM) == (s_col // HEAD_DIM)

    tiles = [(slice(c * C, (c + 1) * C), slice(p * LANES, (p + 1) * LANES))
             for c in range(tl // C) for p in range(N_PAIRS)]
    grams, decays = [], []
    for rows, cols in tiles:
        gcol = gc_w[rows, cols]
        grow = jnp.sum(gcol * eye, axis=0, keepdims=True)
        decays.append(jnp.exp(jnp.where(incl, gcol - grow, NEG)))
        grams.append(lax.dot_general(jnp.concatenate([kb[rows, cols], q[rows, cols]], axis=0).astype(BF16),
                                     _pair_blockdiag(k[rows, cols]), (((1,), (1,)), ((), ())),
                                     preferred_element_type=F32))
    lms = [jnp.where(strict, g[:C] * d, 0.0) for g, d in zip(grams, decays)]
    attns = [jnp.where(incl, g[C:] * d, 0.0) for g, d in zip(grams, decays)]
    t_invs = _pair_unit_lower_inverse(lms, eye, on_diag_block)
    uws = [jnp.dot(t.astype(BF16),
                   jnp.concatenate([_pair_blockdiag(vb[rows, cols]), _pair_blockdiag(kbe[rows, cols])], axis=1),
                   preferred_element_type=F32) for t, (rows, cols) in zip(t_invs, tiles)]
    u2s = [uw[:, :LANES] for uw in uws]
    w2s = [uw[:, LANES:] for uw in uws]
    g_lasts = [gc_w[rows.stop - 1:rows.stop, cols] for rows, cols in tiles]
    k_decs = [k[rows, cols] * jnp.exp(gl - gc_w[rows, cols]) for (rows, cols), gl in zip(tiles, g_lasts)]
    kd_uw = [_bdot_tn(kd, jnp.concatenate([u2, w2], axis=1)) for kd, u2, w2 in zip(k_decs, u2s, w2s)]
    kd_u = [jnp.where(pair_diag, x[:, :LANES], 0.0) for x in kd_uw]
    kd_w = [jnp.where(pair_diag, x[:, LANES:], 0.0).astype(BF16) for x in kd_uw]

    state = [st_ref[p] for p in range(N_PAIRS)]
    for c in range(tl // C):
        ts = range(c * N_PAIRS, (c + 1) * N_PAIRS)
        sbs = [s.astype(BF16) for s in state]
        state = [s * jnp.exp(g_lasts[t]) + kd_u[t] - jnp.dot(kd_w[t], sb, preferred_element_type=F32)
                 for t, s, sb in zip(ts, state, sbs)]
        v_news = [u2s[t] - jnp.dot(w2s[t].astype(BF16), sb, preferred_element_type=F32) for t, sb in zip(ts, sbs)]
        o2s = [jnp.dot(qe[tiles[t][0], tiles[t][1]].astype(BF16), sb, preferred_element_type=F32)
               + _pair_mm(attns[t], v_new) for t, sb, v_new in zip(ts, sbs, v_news)]
        o_ref[c * C:(c + 1) * C, :] = jnp.concatenate(o2s, axis=1)
    for p in range(N_PAIRS):
        st_ref[p] = state[p]

    o = o_ref[...]
    o = o * lax.rsqrt(_bdot(o * o, same_head) * (1.0 / HEAD_DIM) + RMS_EPS) * nw_ref[...]
    o_ref[...] = o * _silu(g_ref[...])


def _gdn(proj3, conv_w, conv_b, dt_bias, a_log, norm_w):
    B, S, _ = proj3.shape
    tl = GDN_TILE
    full = lambda shape: pl.BlockSpec(shape, lambda b, c: (0, 0))
    t = np.arange(tl)
    tri = (t[None, :] <= t[:, None]) & (t[:, None] // DELTA_CHUNK == t[None, :] // DELTA_CHUNK)
    head_of = np.arange(W_MIX) // HEAD_DIM
    same = head_of[:, None] == head_of[None, :]
    lanes = np.arange(LANES)[:, None]
    return pl.pallas_call(
        functools.partial(_gdn_kernel, tl=tl),
        out_shape=jax.ShapeDtypeStruct((B, S, W_MIX), F32),
        grid=(B, S // tl),
        in_specs=[pl.BlockSpec((None, tl, 3 * W_MIX), lambda b, c: (b, c, C_DN_QKV // (3 * W_MIX))),
                  pl.BlockSpec((None, tl, W_MIX), lambda b, c: (b, c, C_DN_G // W_MIX)),
                  pl.BlockSpec((None, tl, LANES), lambda b, c: (b, c, C_SMALL_B // LANES)),
                  full((CONV_WIDTH, 3 * W_MIX)), full((1, 3 * W_MIX)), full((1, LANES)), full((1, LANES)),
                  full((1, W_MIX)), full((W_MIX, W_MIX)), full((tl, tl)), full((LANES, W_MIX)), full((LANES, W_MIX))],
        out_specs=pl.BlockSpec((None, tl, W_MIX), lambda b, c: (b, c, 0)),
        scratch_shapes=[pltpu.VMEM((TAIL, 3 * W_MIX), F32), pltpu.VMEM((N_PAIRS, LANES, LANES), F32)],
        compiler_params=pltpu.CompilerParams(
            dimension_semantics=("parallel", "arbitrary"), vmem_limit_bytes=VMEM_LIMIT),
        name="gdn",
    )(proj3, proj3, proj3, conv_w, conv_b.reshape(1, -1),
      _pad_lanes(dt_bias, SM_DECAY), _pad_lanes(a_log, SM_DECAY), jnp.tile(norm_w.astype(F32), N_HEADS).reshape(1, -1),
      jnp.asarray(same, BF16), jnp.asarray(tri, BF16),
      jnp.asarray(head_of[None, :] + SM_BETA == lanes, BF16), jnp.asarray(head_of[None, :] + SM_DECAY == lanes, BF16))


def _outproj_kernel(x_ref, ya_ref, yb_ref, yc_ref, yd_ref, w_ref, lg_ref, lb_ref, o_ref):
    mix = (_bdot(ya_ref[...], w_ref[0]) + _bdot(yb_ref[...], w_ref[1])
           + _bdot(yc_ref[...], w_ref[2]) + _bdot(yd_ref[...], w_ref[3]))
    r = DEEPNORM_ALPHA * x_ref[...] + mix
    mu = jnp.mean(r, axis=1, keepdims=True)
    var = jnp.mean(jnp.square(r - mu), axis=1, keepdims=True)
    o_ref[...] = (r - mu) * lax.rsqrt(var + LN_EPS) * lg_ref[...] + lb_ref[...]


def _outproj(x2d, ya, yb, yc, yd, w_out, ln_g, ln_b):
    T = x2d.shape[0]
    tm = min(OUTPROJ_ROWS, T)
    rowblk = lambda width: pl.BlockSpec((tm, width), lambda i: (i, 0))
    return pl.pallas_call(
        _outproj_kernel,
        out_shape=jax.ShapeDtypeStruct((T, D_MODEL), F32),
        grid=(T // tm,),
        in_specs=[rowblk(D_MODEL), rowblk(W_MIX), rowblk(W_MIX), rowblk(W_MIX), rowblk(W_MIX),
                  pl.BlockSpec((4, W_MIX, D_MODEL), lambda i: (0, 0, 0)),
                  pl.BlockSpec((1, D_MODEL), lambda i: (0, 0)), pl.BlockSpec((1, D_MODEL), lambda i: (0, 0))],
        out_specs=rowblk(D_MODEL),
        compiler_params=pltpu.CompilerParams(
            dimension_semantics=("parallel",), vmem_limit_bytes=VMEM_LIMIT),
        name="outproj",
    )(x2d, ya, yb, yc, yd, w_out, ln_g.reshape(1, -1), ln_b.reshape(1, -1))


def _alibi_slopes():
    n = 2 * N_HEADS
    s = (2.0 ** (-8.0 * (np.arange(n) + 1) / n)).astype(np.float32)
    return jnp.asarray(s[:N_HEADS]), jnp.asarray(s[N_HEADS:])


def _layer(x, w_in_t, layer, w_out, ssm_conv_w, ssm_conv_b, ssm_dt_bias, ssm_A_log, ssm_D, ssm_norm_w,
           dn_conv_w, dn_conv_b, dn_dt_bias, dn_A_log, dn_norm_w, ln_g, ln_b):
    B, S, _ = x.shape
    T = B * S
    slopes_dil, slopes_moba = _alibi_slopes()
    x2d = x.reshape(T, D_MODEL)
    proj2d = _inproj(x2d, w_in_t, layer)
    proj3 = proj2d.reshape(B, S, PROJ_COLS)

    ya = _moba(proj3, slopes_moba)
    yb = _ssd(proj3, ssm_conv_w, ssm_conv_b, ssm_dt_bias, ssm_A_log, ssm_D, ssm_norm_w)
    yd = _gdn(proj3, dn_conv_w, dn_conv_b, dn_dt_bias, dn_A_log, dn_norm_w)

    yc = _dilated(proj3, slopes_dil)

    out = _outproj(x2d, ya.reshape(T, W_MIX), yb.reshape(T, W_MIX), yc.reshape(T, W_MIX), yd.reshape(T, W_MIX),
                   w_out.reshape(4, W_MIX, D_MODEL).astype(BF16), ln_g, ln_b)
    return out.reshape(B, S, D_MODEL)


def kernel(x, w_in, w_out, ssm_conv_w, ssm_conv_b, ssm_dt_bias, ssm_A_log, ssm_D, ssm_norm_w,
           dn_conv_w, dn_conv_b, dn_dt_bias, dn_A_log, dn_norm_w, ln_g, ln_b):
    assert x.shape[1] % 2048 == 0 and x.shape[2] == D_MODEL
    w_in_t = _repack_w_in(jnp.transpose(w_in, (2, 0, 1)))
    for l in range(DEPTH):
        x = _layer(x, w_in_t, l, w_out[l], ssm_conv_w[l], ssm_conv_b[l], ssm_dt_bias[l], ssm_A_log[l],
                   ssm_D[l], ssm_norm_w[l], dn_conv_w[l], dn_conv_b[l], dn_dt_bias[l], dn_A_log[l],
                   dn_norm_w[l], ln_g[l], ln_b[l])
    return x
```

```python
import functools
import math

import numpy as np
import jax
import jax.numpy as jnp
from jax import lax
from jax.experimental import pallas as pl
from jax.experimental.pallas import tpu as pltpu

F32 = jnp.float32
BF16 = jnp.bfloat16
HIGHEST = lax.Precision.HIGHEST

D_MODEL = 1024
DEPTH = 2
HEAD_DIM = 64
N_HEADS = 6
W_MIX = N_HEADS * HEAD_DIM
N_PAIRS = W_MIX // 128
D_MIX = 4 * W_MIX
MOBA_BLOCK = 256
MOBA_TOPK = 3
MOBA_NFEAT = 3
MOBA_UNROLL = 1
MOBA_QBLOCKS = 4
MOBA_LROWS = 16
SSD_STATE = 128
SSD_GROUPS = 2
SSD_CHUNK = 128
SSD_TILE = 512
SSD_XBC = W_MIX + 2 * SSD_GROUPS * SSD_STATE
CONV_WIDTH = 4
DIL_DILATIONS = (1, 4, 16)
DIL_GROUPS = len(DIL_DILATIONS)
DIL_BATCH = 16
DIL_MERGE_ROWS = 256
DIL_SPAN = 128
DELTA_CHUNK = 64
LN_EPS = 1e-5
RMS_EPS = 1e-6
DEEPNORM_ALPHA = (2.0 * DEPTH) ** 0.25
ATTN_SCALE = HEAD_DIM ** -0.5
LOG2E = math.log2(math.e)
NEG = -1e30

LANES = 128
TAIL = 8
VMEM_LIMIT = 48 * 1024 * 1024

INPROJ_ROWS = 1024
INPROJ_COLS = 2048
OUTPROJ_ROWS = 1024
GDN_TILE = 512

C_MOBA = 0
C_SSD_X = 1536
C_SSD_Z = 1920
C_DN_QKV = 2304
C_DN_G = 3456
C_DIL = 3840
C_SSD_B = 5376
C_SSD_C = 5632
C_SMALL_A = 5888
C_SMALL_B = 6016
PROJ_COLS = 6144
SM_DT, SM_BETA, SM_DECAY = 0, 116, 122

_R = [int(v) for v in np.cumsum([0, 384, 384, 384, 384, SSD_XBC, 384, 6, 384, 384, 384, 384, 1152, 384, 6, 6])]
R_SSD_XBC, R_SSD_Z, R_SSD_DT, R_DIL, R_DN_QKV, R_DN_G, R_END = _R[4], _R[5], _R[6], _R[7], _R[11], _R[12], _R[15]
_SEGMENTS = ((C_MOBA, 0, 4 * W_MIX), (C_SSD_X, R_SSD_XBC, W_MIX), (C_SSD_Z, R_SSD_Z, W_MIX),
             (C_DN_QKV, R_DN_QKV, 3 * W_MIX), (C_DN_G, R_DN_G, W_MIX), (C_DIL, R_DIL, 4 * W_MIX),
             (C_SSD_B, R_SSD_XBC + W_MIX, 2 * SSD_GROUPS * SSD_STATE), (C_SMALL_A, R_SSD_DT, LANES),
             (C_SMALL_B, R_END - LANES, LANES))
_BLOCK_SRC = np.zeros(PROJ_COLS // LANES, np.int32)
for _dst, _src, _n in _SEGMENTS:
    _BLOCK_SRC[_dst // LANES:(_dst + _n) // LANES] = _src + LANES * np.arange(_n // LANES)
assert R_SSD_DT + LANES <= R_END


def _repack_kernel(src_ref, w_ref, o_ref):
    for layer in range(o_ref.shape[0]):
        o_ref[layer] = w_ref[:, layer, :].astype(BF16)


def _repack_w_in(w_t):
    _, depth, dm = w_t.shape
    return pl.pallas_call(
        _repack_kernel,
        out_shape=jax.ShapeDtypeStruct((depth, PROJ_COLS, dm), BF16),
        grid_spec=pltpu.PrefetchScalarGridSpec(
            num_scalar_prefetch=1, grid=(PROJ_COLS // LANES,),
            in_specs=[pl.BlockSpec((pl.Element(LANES), pl.Element(depth), pl.Element(dm)),
                                   lambda b, src: (src[b], 0, 0))],
            out_specs=pl.BlockSpec((depth, LANES, dm), lambda b, src: (0, b, 0))),
        compiler_params=pltpu.CompilerParams(dimension_semantics=("parallel",), vmem_limit_bytes=VMEM_LIMIT),
        name="repack_w_in",
    )(jnp.asarray(_BLOCK_SRC), w_t)


def _silu(x):
    return x * jax.nn.sigmoid(x)


def _softplus(x):
    return jnp.maximum(x, 0.0) + jnp.log(1.0 + jnp.exp(-jnp.abs(x)))


def _bdot(a, b):
    return jnp.dot(a.astype(BF16), b.astype(BF16), preferred_element_type=F32)


def _bdot_nt(a, b):
    return lax.dot_general(a.astype(BF16), b.astype(BF16), (((1,), (1,)), ((), ())),
                           preferred_element_type=F32)


def _bdot_tn(a, b):
    return lax.dot_general(a.astype(BF16), b.astype(BF16), (((0,), (0,)), ((), ())),
                           preferred_element_type=F32)


def _causal_conv_silu(x_ref, tail, w, b):
    L = x_ref.shape[0]
    head = jnp.concatenate([tail, x_ref[:TAIL, :]], axis=0)
    y_head, y_rest = b, b
    for j in range(CONV_WIDTH):
        off = TAIL - (CONV_WIDTH - 1) + j
        y_head = y_head + w[j:j + 1, :] * head[off:off + TAIL, :]
        y_rest = y_rest + w[j:j + 1, :] * x_ref[pl.ds(off, L - TAIL), :]
    return _silu(jnp.concatenate([y_head, y_rest], axis=0))


def _inproj_kernel(x_ref, w_ref, o_ref, xb_ref):
    @pl.when(pl.program_id(1) == 0)
    def _():
        xb_ref[...] = x_ref[...].astype(BF16)

    o_ref[...] = lax.dot_general(xb_ref[...], w_ref[...], (((1,), (1,)), ((), ())), preferred_element_type=F32)


def _inproj(x2d, w_t, layer):
    T = x2d.shape[0]
    tm = min(INPROJ_ROWS, T)
    tn = INPROJ_COLS
    return pl.pallas_call(
        _inproj_kernel,
        out_shape=jax.ShapeDtypeStruct((T, PROJ_COLS), F32),
        grid=(T // tm, PROJ_COLS // tn),
        in_specs=[pl.BlockSpec((tm, D_MODEL), lambda i, j: (i, 0)),
                  pl.BlockSpec((None, tn, D_MODEL), lambda i, j: (layer, j, 0))],
        out_specs=pl.BlockSpec((tm, tn), lambda i, j: (i, j)),
        scratch_shapes=[pltpu.VMEM((tm, D_MODEL), BF16)],
        compiler_params=pltpu.CompilerParams(
            dimension_semantics=("parallel", "arbitrary"), vmem_limit_bytes=VMEM_LIMIT),
        name="inproj",
    )(x2d, w_t)


def _moba_kernel(slopes_ref, q_ref, k_ref, v_ref, g_ref, o_ref, kaug_ref, vt_ref, kmean_ref, s_ref, *, nblk):
    p = pl.program_id(1)
    L = MOBA_BLOCK
    NF = MOBA_NFEAT

    @pl.when(pl.program_id(2) == 0)
    def _():
        lane = lax.broadcasted_iota(jnp.int32, (L, LANES), 1)
        feat_lane = lane % HEAD_DIM
        c_loc = lax.broadcasted_iota(jnp.int32, (L, LANES), 0).astype(F32)
        feat = jnp.where(feat_lane < NF, 1.0, jnp.where(feat_lane < 2 * NF, c_loc, 0.0))
        ones = jnp.ones((MOBA_LROWS, L), F32)
        for n in range(nblk):
            kblk = k_ref[n * L:(n + 1) * L, :]
            kmean_ref[n:n + 1, :] = jnp.mean(kblk, axis=0, keepdims=True)
            kaug_ref[0, n] = jnp.where(lane < HEAD_DIM, kblk, feat).astype(BF16)
            kaug_ref[1, n] = jnp.where(lane < HEAD_DIM, feat, kblk).astype(BF16)
            vt = v_ref[n * L:(n + 1) * L, :].T
            for h in range(2):
                vt_ref[h, n] = jnp.concatenate([ones, vt[h * HEAD_DIM:(h + 1) * HEAD_DIM, :]], axis=0).astype(BF16)

    QB = MOBA_QBLOCKS
    streams = [(qb, h) for qb in range(QB) for h in range(2)]
    i0 = QB * pl.program_id(2)
    qt_all = q_ref[...].T
    qt = [qt_all[:, qb * L:(qb + 1) * L] for qb in range(QB)]
    qrow = lax.broadcasted_iota(jnp.int32, (LANES, L), 0)
    r_loc = lax.broadcasted_iota(jnp.int32, (1, L), 1).astype(F32)
    brow = lax.broadcasted_iota(jnp.int32, (nblk, L), 0).astype(F32)
    frow = lax.broadcasted_iota(jnp.int32, (2 * TAIL, L), 0)
    kmean = kmean_ref[...]
    i_f = [(i0 + qb).astype(F32) for qb in range(QB)]

    qparts = [(qt[qb][h * HEAD_DIM:(h + 1) * HEAD_DIM, :] * (ATTN_SCALE * LOG2E)).astype(BF16) for qb, h in streams]
    slope_rows = [jnp.full((1, L), slopes_ref[2 * p + h] * LOG2E, F32) for h in range(2)]
    zeros48 = jnp.zeros((HEAD_DIM - 2 * TAIL, L), BF16)

    def q_aug(st, t_row):
        h = streams[st][1]
        terms = _split3(t_row) + _split3(slope_rows[h])
        feat = jnp.zeros((2 * TAIL, L), F32)
        for f, term in enumerate(terms):
            feat = jnp.where(frow == f, term.astype(F32), feat)
        feat = feat.astype(BF16)
        if h == 0:
            return jnp.concatenate([qparts[st], feat, zeros48], axis=0)
        return jnp.concatenate([feat, zeros48, qparts[st]], axis=0)

    def past_row(st, n, in_range=None):
        qb, h = streams[st]
        n_f = n.astype(F32)
        picked = (sels[st][0] == n_f) | (sels[st][1] == n_f) | (sels[st][2] == n_f)
        if in_range is not None:
            picked = picked & in_range
        return jnp.where(picked, -slope_rows[h] * (r_loc + L * (i_f[qb] - n_f)), NEG)

    U = MOBA_UNROLL
    NS = len(streams)

    def put_scores(slot, n0):
        for u in range(U):
            n = n0 + u
            for st, (qb, h) in enumerate(streams):
                s_ref[slot, st, u] = jnp.dot(kaug_ref[h, jnp.minimum(n, nblk - 1)], q_aug(st, past_row(st, n, n < i0)),
                                             preferred_element_type=F32).astype(BF16)

    def new_max(slot, carry):
        return [functools.reduce(jnp.maximum, [carry[2 * st]] + [
            jnp.max(s_ref[slot, st, u], axis=0, keepdims=True).astype(F32) for u in range(U)]) for st in range(NS)]

    def absorb(slot, n0, m_new, carry):
        ps = [jnp.concatenate([jnp.exp2(s_ref[slot, st, u] - m_new[st].astype(BF16)) for u in range(U)], axis=0)
              for st in range(NS)]
        vts = [jnp.concatenate([vt_ref[h, jnp.minimum(n0 + u, nblk - 1)] for u in range(U)], axis=1) for h in range(2)]
        pv = [jnp.dot(vts[streams[st][1]], ps[st], preferred_element_type=F32) for st in range(NS)]
        out = []
        for st in range(NS):
            out += [m_new[st], jnp.exp2(carry[2 * st] - m_new[st]) * carry[2 * st + 1] + pv[st]]
        return tuple(out)

    c_idx = lax.broadcasted_iota(jnp.int32, (L, L), 0)
    r_idx = lax.broadcasted_iota(jnp.int32, (L, L), 1)
    own = [jnp.where(c_idx <= r_idx,
                     jnp.dot(kaug_ref[h, i0 + qb], q_aug(st, -slope_rows[h] * r_loc), preferred_element_type=F32),
                     NEG).astype(BF16) for st, (qb, h) in enumerate(streams)]

    gates = [jnp.where(brow < i_f[qb],
                       jnp.dot(kmean, jnp.where((qrow >= HEAD_DIM * h) & (qrow < HEAD_DIM * (h + 1)), qt[qb], 0.0),
                               precision=HIGHEST, preferred_element_type=F32), -jnp.inf) for qb, h in streams]
    sels = [[] for _ in streams]
    for kk in range(MOBA_TOPK):
        for st, (qb, h) in enumerate(streams):
            mx = jnp.max(gates[st], axis=0, keepdims=True)
            idx = jnp.min(jnp.where(gates[st] == mx, brow, float(nblk)), axis=0, keepdims=True)
            sels[st].append(jnp.where(kk < i0 + qb, idx, -1.0))
            gates[st] = jnp.where(brow == idx, -jnp.inf, gates[st])

    init = []
    for st, (qb, h) in enumerate(streams):
        tiles = [jnp.dot(kaug_ref[h, i0 + e], q_aug(st, past_row(st, i0 + e)),
                         preferred_element_type=F32).astype(BF16) for e in range(qb)] + [own[st]]
        m0 = functools.reduce(jnp.maximum, [jnp.max(s, axis=0, keepdims=True) for s in tiles])
        p0 = jnp.concatenate([jnp.exp2(s - m0) for s in tiles], axis=0)
        vt0 = jnp.concatenate([vt_ref[h, i0 + e] for e in range(qb + 1)], axis=1)
        init += [m0.astype(F32), jnp.dot(vt0, p0, preferred_element_type=F32)]

    put_scores(0, jnp.int32(0))

    def body(j, carry):
        n0 = 2 * U * j
        m_a = new_max(0, carry)
        put_scores(1, n0 + U)
        carry = absorb(0, n0, m_a, carry)
        m_b = new_max(1, carry)
        put_scores(0, n0 + 2 * U)
        return absorb(1, n0 + U, m_b, carry)

    res = lax.fori_loop(0, (i0 + 2 * U - 1) // (2 * U), body, tuple(init))
    for qb in range(QB):
        out_t = jnp.concatenate([res[2 * st + 1][MOBA_LROWS:] / res[2 * st + 1][0:1]
                                 for st in (2 * qb, 2 * qb + 1)], axis=0)
        o_ref[qb * L:(qb + 1) * L, :] = out_t.T * _silu(g_ref[qb * L:(qb + 1) * L, :])


def _moba(proj3, slopes):
    B, S, _ = proj3.shape
    L = MOBA_BLOCK
    nblk = S // L
    QB = MOBA_QBLOCKS
    cb = C_MOBA // LANES
    return pl.pallas_call(
        functools.partial(_moba_kernel, nblk=nblk),
        out_shape=jax.ShapeDtypeStruct((B, S, W_MIX), F32),
        grid=(B, N_PAIRS, nblk // QB),
        in_specs=[pl.BlockSpec(memory_space=pltpu.SMEM),
                  pl.BlockSpec((None, QB * L, LANES), lambda b, p, i: (b, i, cb + p)),
                  pl.BlockSpec((None, S, LANES), lambda b, p, i: (b, 0, cb + N_PAIRS + p)),
                  pl.BlockSpec((None, S, LANES), lambda b, p, i: (b, 0, cb + 2 * N_PAIRS + p)),
                  pl.BlockSpec((None, QB * L, LANES), lambda b, p, i: (b, i, cb + 3 * N_PAIRS + p))],
        out_specs=pl.BlockSpec((None, QB * L, LANES), lambda b, p, i: (b, i, p)),
        scratch_shapes=[pltpu.VMEM((2, nblk, L, LANES), BF16),
                        pltpu.VMEM((2, nblk, MOBA_LROWS + HEAD_DIM, L), BF16),
                        pltpu.VMEM((nblk, LANES), F32),
                        pltpu.VMEM((2, 2 * QB, MOBA_UNROLL, L, L), BF16)],
        compiler_params=pltpu.CompilerParams(
            dimension_semantics=("parallel", "parallel", "arbitrary"), vmem_limit_bytes=VMEM_LIMIT),
        name="moba",
    )(slopes, proj3, proj3, proj3, proj3)


def _dil_kernel(slopes_ref, q_ref, kc_ref, kp_ref, vc_ref, vp_ref, g_ref, o_ref, acc_ref, m_ref, l_ref, *, tq):
    p = pl.program_id(1)
    first = pl.program_id(2) == 0
    SB = DIL_SPAN
    lane = lax.broadcasted_iota(jnp.int32, (1, LANES), 1)
    lo = lane < HEAD_DIM
    row = lax.broadcasted_iota(jnp.int32, (SB, 2 * SB), 0)
    col = lax.broadcasted_iota(jnp.int32, (SB, 2 * SB), 1)
    dist = (row + SB - col).astype(F32)
    band = (col >= row) & (col <= row + SB)
    no_halo = jnp.logical_and(first, col < SB)

    def rows(start, d):
        return pl.ds(start, SB) if d == 1 else pl.ds(start, SB, stride=d)

    for gi, d in enumerate(DIL_DILATIONS):
        span = d * SB
        bias, bias_first = [], []
        for h in range(2):
            slope = slopes_ref[2 * p + h] * (float(d) * LOG2E)
            bias.append(jnp.where(band, -slope * dist, NEG))
            bias_first.append(jnp.where(no_halo, NEG, bias[h]))
        problems = [(r, j) for j in range(tq // span) for r in range(d)]
        for c0 in range(0, len(problems), DIL_BATCH):
            chunk = problems[c0:c0 + DIL_BATCH]
            blocks = {}

            def block(r, j):
                if (r, j) not in blocks:
                    if j < 0:
                        sel = rows(tq - span + r, d)
                        kb, vb = kp_ref[sel, :], vp_ref[sel, :]
                    else:
                        sel = rows(r + span * j, d)
                        kb, vb = kc_ref[sel, :], vc_ref[sel, :]
                    blocks[(r, j)] = (kb.astype(BF16),
                                      (jnp.where(lo, vb, 1.0).astype(BF16), jnp.where(lo, 1.0, vb).astype(BF16)))
                return blocks[(r, j)]

            kv, qs = [], []
            for r, j in chunk:
                (kp, vp), (kc, vc) = block(r, j - 1), block(r, j)
                kv.append((jnp.concatenate([kp, kc], axis=0),
                           tuple(jnp.concatenate([vp[h], vc[h]], axis=0) for h in range(2))))
                qs.append(q_ref[rows(r + span * j, d), :] * (ATTN_SCALE * LOG2E))
            work = [(c, h) for c in range(len(chunk)) for h in range(2)]
            ss = [(_bdot_nt(jnp.where(lo if h == 0 else jnp.logical_not(lo), qs[c], 0.0), kv[c][0])
                   + (bias_first[h] if chunk[c][1] == 0 else bias[h])).astype(BF16) for c, h in work]
            ms = [jnp.max(s, axis=1, keepdims=True) for s in ss]
            ps = [jnp.exp2(s - m) for s, m in zip(ss, ms)]
            pvs = [jnp.dot(e, kv[c][1][h], preferred_element_type=F32) for e, (c, h) in zip(ps, work)]
            for c, (r, j) in enumerate(chunk):
                cur = rows(r + span * j, d)
                acc_ref[gi, cur, :] = jnp.where(lo, pvs[2 * c], pvs[2 * c + 1])
                m_ref[gi, cur, :] = jnp.where(lo, ms[2 * c].astype(F32), ms[2 * c + 1].astype(F32))
                l_ref[gi, cur, :] = pltpu.roll(jnp.where(lo, pvs[2 * c + 1], pvs[2 * c]), HEAD_DIM, axis=1)

    for c0 in range(0, tq, DIL_MERGE_ROWS):
        rs = slice(c0, c0 + DIL_MERGE_ROWS)
        ms = [m_ref[gi, rs, :] for gi in range(DIL_GROUPS)]
        top = functools.reduce(jnp.maximum, ms)
        ws = [jnp.exp2(m - top) for m in ms]
        num = sum(w * acc_ref[gi, rs, :] for gi, w in enumerate(ws))
        den = sum(w * l_ref[gi, rs, :] for gi, w in enumerate(ws))
        o_ref[rs, :] = num / den * _silu(g_ref[rs, :])


def _dilated(proj3, slopes):
    B, S, _ = proj3.shape
    tq = DIL_DILATIONS[-1] * DIL_SPAN
    base = C_DIL // LANES
    cur = lambda off: pl.BlockSpec((None, tq, LANES), lambda b, p, t: (b, t, base + off + p))
    prev = lambda off: pl.BlockSpec((None, tq, LANES), lambda b, p, t: (b, jnp.maximum(t - 1, 0), base + off + p))
    scratch = pltpu.VMEM((DIL_GROUPS, tq, LANES), F32)
    return pl.pallas_call(
        functools.partial(_dil_kernel, tq=tq),
        out_shape=jax.ShapeDtypeStruct((B, S, W_MIX), F32),
        grid=(B, N_PAIRS, S // tq),
        in_specs=[pl.BlockSpec(memory_space=pltpu.SMEM),
                  cur(0), cur(N_PAIRS), prev(N_PAIRS), cur(2 * N_PAIRS), prev(2 * N_PAIRS), cur(3 * N_PAIRS)],
        out_specs=pl.BlockSpec((None, tq, LANES), lambda b, p, t: (b, t, p)),
        scratch_shapes=[scratch, scratch, scratch],
        compiler_params=pltpu.CompilerParams(
            dimension_semantics=("parallel", "parallel", "arbitrary"), vmem_limit_bytes=VMEM_LIMIT),
        name="dilated",
    )(slopes, proj3, proj3, proj3, proj3, proj3, proj3)


def _ssd_kernel(x_ref, z_ref, b_ref, c_ref, sm_ref, cw_ref, cb_ref, dtb_ref, alog_ref, dskip_ref, nw_ref,
                tri_ref, expand_ref, o_ref, tx_ref, tb_ref, tc_ref, st_ref, *, tl):
    L = SSD_CHUNK
    N = SSD_STATE
    nchunk = tl // L

    @pl.when(pl.program_id(1) == 0)
    def _():
        tx_ref[...] = jnp.zeros_like(tx_ref)
        tb_ref[...] = jnp.zeros_like(tb_ref)
        tc_ref[...] = jnp.zeros_like(tc_ref)
        st_ref[...] = jnp.zeros_like(st_ref)

    cw = cw_ref[...]
    cb = cb_ref[...]
    xs = _causal_conv_silu(x_ref, tx_ref[...], cw[:, :W_MIX], cb[:, :W_MIX])
    Bm = _causal_conv_silu(b_ref, tb_ref[...], cw[:, W_MIX:W_MIX + 2 * N], cb[:, W_MIX:W_MIX + 2 * N])
    Cm = _causal_conv_silu(c_ref, tc_ref[...], cw[:, W_MIX + 2 * N:], cb[:, W_MIX + 2 * N:])
    tx_ref[...] = x_ref[tl - TAIL:, :]
    tb_ref[...] = b_ref[tl - TAIL:, :]
    tc_ref[...] = c_ref[tl - TAIL:, :]

    dt = _softplus(sm_ref[...] + dtb_ref[...])
    a = dt * (-jnp.exp(alog_ref[...]))
    a_cum = _xdot(tri_ref[...], a)
    expand = expand_ref[...]
    dt_w = _xdot_r(dt, expand)
    acum_w = _xdot_r(a_cum, expand)
    xdt = xs * dt_w
    a_cum_t = a_cum.T
    e_acum = jnp.exp(acum_w)

    lane = lax.broadcasted_iota(jnp.int32, (1, LANES), 1)
    lane_w = lax.broadcasted_iota(jnp.int32, (1, W_MIX), 1)
    in_g0 = lane_w < W_MIX // SSD_GROUPS
    row = lax.broadcasted_iota(jnp.int32, (L, L), 0)
    col = lax.broadcasted_iota(jnp.int32, (L, L), 1)
    causal = col <= row
    chunks = [slice(c * L, (c + 1) * L) for c in range(nchunk)]
    heads_per_group = N_HEADS // SSD_GROUPS

    scores = [[_bdot_nt(Cm[rs, g * N:(g + 1) * N], Bm[rs, g * N:(g + 1) * N]) for g in range(SSD_GROUPS)]
              for rs in chunks]
    atot = [acum_w[rs.stop - 1:rs.stop, :] for rs in chunks]
    xdec = [xdt[rs, :] * jnp.exp(atot[c] - acum_w[rs, :]) for c, rs in enumerate(chunks)]
    contrib = [jnp.where(in_g0, _bdot_tn(Bm[rs, :N], xdec[c]), _bdot_tn(Bm[rs, N:], xdec[c]))
               for c, rs in enumerate(chunks)]
    lmats = [[jnp.exp(jnp.where(causal, a_cum[rs, h:h + 1] - a_cum_t[h:h + 1, rs], NEG)) for h in range(N_HEADS)]
             for rs in chunks]
    diag = [[_bdot(scores[c][h // heads_per_group] * lmats[c][h], xdt[rs, (h // 2) * LANES:(h // 2 + 1) * LANES])
             for h in range(N_HEADS)] for c, rs in enumerate(chunks)]

    states = [st_ref[...]]
    for c in range(nchunk):
        states.append(states[c] * jnp.exp(atot[c]) + contrib[c])
    st_ref[...] = states[nchunk]
    sb = [s.astype(BF16) for s in states[:nchunk]]
    off = [jnp.where(in_g0, jnp.dot(Cm[rs, :N].astype(BF16), sb[c], preferred_element_type=F32),
                     jnp.dot(Cm[rs, N:].astype(BF16), sb[c], preferred_element_type=F32))
           for c, rs in enumerate(chunks)]

    for c, rs in enumerate(chunks):
        y_diag = jnp.concatenate([jnp.where(lane < HEAD_DIM, diag[c][2 * p], diag[c][2 * p + 1])
                                  for p in range(N_PAIRS)], axis=1)
        y = y_diag + off[c] * e_acum[rs, :] + dskip_ref[...] * xs[rs, :]
        y = y * _silu(z_ref[rs, :])
        o_ref[rs, :] = y * lax.rsqrt(jnp.mean(y * y, axis=1, keepdims=True) + RMS_EPS) * nw_ref[...]


def _pad_lanes(v, offset):
    return jnp.zeros((1, LANES), F32).at[0, offset:offset + v.shape[0]].set(v.astype(F32))


def _ssd(proj3, conv_w, conv_b, dt_bias, a_log, d_skip, norm_w):
    B, S, _ = proj3.shape
    L = SSD_TILE
    N2 = 2 * SSD_STATE
    full = lambda shape: pl.BlockSpec(shape, lambda b, c: (0, 0))
    t = np.arange(L)
    tri = (t[None, :] <= t[:, None]) & (t[:, None] // SSD_CHUNK == t[None, :] // SSD_CHUNK)
    expand = np.arange(W_MIX)[None, :] // HEAD_DIM == np.arange(LANES)[:, None]
    return pl.pallas_call(
        functools.partial(_ssd_kernel, tl=L),
        out_shape=jax.ShapeDtypeStruct((B, S, W_MIX), F32),
        grid=(B, S // L),
        in_specs=[pl.BlockSpec((None, L, W_MIX), lambda b, c: (b, c, C_SSD_X // W_MIX)),
                  pl.BlockSpec((None, L, W_MIX), lambda b, c: (b, c, C_SSD_Z // W_MIX)),
                  pl.BlockSpec((None, L, N2), lambda b, c: (b, c, C_SSD_B // N2)),
                  pl.BlockSpec((None, L, N2), lambda b, c: (b, c, C_SSD_C // N2)),
                  pl.BlockSpec((None, L, LANES), lambda b, c: (b, c, C_SMALL_A // LANES)),
                  full((CONV_WIDTH, SSD_XBC)), full((1, SSD_XBC)), full((1, LANES)), full((1, LANES)),
                  full((1, W_MIX)), full((1, W_MIX)), full((L, L)), full((LANES, W_MIX))],
        out_specs=pl.BlockSpec((None, L, W_MIX), lambda b, c: (b, c, 0)),
        scratch_shapes=[pltpu.VMEM((TAIL, W_MIX), F32), pltpu.VMEM((TAIL, N2), F32),
                        pltpu.VMEM((TAIL, N2), F32), pltpu.VMEM((SSD_STATE, W_MIX), F32)],
        compiler_params=pltpu.CompilerParams(
            dimension_semantics=("parallel", "arbitrary"), vmem_limit_bytes=VMEM_LIMIT),
        name="ssd",
    )(proj3, proj3, proj3, proj3, proj3, conv_w, conv_b.reshape(1, -1),
      _pad_lanes(dt_bias, SM_DT), _pad_lanes(a_log, SM_DT),
      jnp.repeat(d_skip.astype(F32), HEAD_DIM).reshape(1, -1), norm_w.reshape(1, -1),
      jnp.asarray(tri, BF16), jnp.asarray(expand, BF16))


INV_LEAF = 16


def _pair_blockdiag(x):
    lo = lax.broadcasted_iota(jnp.int32, (1, LANES), 1) < HEAD_DIM
    xb = x.astype(BF16)
    zero = jnp.zeros_like(xb)
    return jnp.concatenate([jnp.where(lo, xb, zero), jnp.where(lo, zero, xb)], axis=0)


def _pair_mm(a, b):
    return jnp.dot(a.astype(BF16), _pair_blockdiag(b), preferred_element_type=F32)


def _pair_unit_lower_inverse(lms, eye, on_diag_block):
    mm = lambda xs, ys: [_pair_mm(x, y) for x, y in zip(xs, ys)]
    d = [jnp.where(on_diag_block, lm, 0.0) for lm in lms]
    off = [lm - x for lm, x in zip(lms, d)]
    d2 = mm(d, d)
    d4 = mm(d2, d2)
    d8 = mm(d4, d4)
    left = mm([eye - x for x in d], [eye + x for x in d2])
    right = mm([eye + x for x in d4], [eye + x for x in d8])
    dinv = mm(left, right)
    m = mm(dinv, off)
    m2 = mm(m, m)
    return mm(mm([eye - x for x in m], [eye + x for x in m2]), dinv)


def _split3(x):
    hi = x.astype(BF16)
    r = x - hi.astype(F32)
    mid = r.astype(BF16)
    lo = (r - mid.astype(F32)).astype(BF16)
    return hi, mid, lo


def _xdot(a01, x):
    a = a01.astype(BF16)
    return sum(jnp.dot(a, t, preferred_element_type=F32) for t in _split3(x))


def _xdot_r(x, b01):
    b = b01.astype(BF16)
    return sum(jnp.dot(t, b, preferred_element_type=F32) for t in _split3(x))


def _gdn_kernel(qkv_ref, g_ref, sm_ref, cw_ref, cb_ref, dtb_ref, alog_ref, nw_ref, same_ref, tri_ref, eb_ref, ed_ref,
                o_ref, tail_ref, st_ref, *, tl):
    C = DELTA_CHUNK

    @pl.when(pl.program_id(1) == 0)
    def _():
        tail_ref[...] = jnp.zeros_like(tail_ref)
        st_ref[...] = jnp.zeros_like(st_ref)

    qkv = _causal_conv_silu(qkv_ref, tail_ref[...], cw_ref[...], cb_ref[...])
    tail_ref[...] = qkv_ref[tl - TAIL:, :]
    q, k, v = qkv[:, :W_MIX], qkv[:, W_MIX:2 * W_MIX], qkv[:, 2 * W_MIX:]

    same_head = same_ref[...]
    q = q * lax.rsqrt(_bdot(q * q, same_head) + RMS_EPS) * ATTN_SCALE
    k = k * lax.rsqrt(_bdot(k * k, same_head) + RMS_EPS)

    sm = sm_ref[...]
    beta = jax.nn.sigmoid(sm)
    gdec = -jnp.exp(alog_ref[...]) * _softplus(sm + dtb_ref[...])
    gc = _xdot(tri_ref[...], gdec)
    beta_w = _xdot_r(beta, eb_ref[...])
    gc_w = _xdot_r(gc, ed_ref[...])

    kb = k * beta_w
    vb = v * beta_w
    egc = jnp.exp(gc_w)
    kbe = kb * egc
    qe = q * egc

    row = lax.broadcasted_iota(jnp.int32, (C, LANES), 0)
    col = lax.broadcasted_iota(jnp.int32, (C, LANES), 1) % HEAD_DIM
    eye = (row == col).astype(F32)
    incl = col <= row
    strict = col < row
    on_diag_block = (row // INV_LEAF) == (col // INV_LEAF)
    s_row = lax.broadcasted_iota(jnp.int32, (LANES, LANES), 0)
    s_col = lax.broadcasted_iota(jnp.int32, (LANES, LANES), 1)
    pair_diag = (s_row // HEAD_DIM) == (s_col // HEAD_DIM)

    tiles = [(slice(c * C, (c + 1) * C), slice(p * LANES, (p + 1) * LANES))
             for c in range(tl // C) for p in range(N_PAIRS)]
    grams, decays = [], []
    for rows, cols in tiles:
        gcol = gc_w[rows, cols]
        grow = jnp.sum(gcol * eye, axis=0, keepdims=True)
        decays.append(jnp.exp(jnp.where(incl, gcol - grow, NEG)))
        grams.append(lax.dot_general(jnp.concatenate([kb[rows, cols], q[rows, cols]], axis=0).astype(BF16),
                                     _pair_blockdiag(k[rows, cols]), (((1,), (1,)), ((), ())),
                                     preferred_element_type=F32))
    lms = [jnp.where(strict, g[:C] * d, 0.0) for g, d in zip(grams, decays)]
    attns = [jnp.where(incl, g[C:] * d, 0.0) for g, d in zip(grams, decays)]
    t_invs = _pair_unit_lower_inverse(lms, eye, on_diag_block)
    uws = [jnp.dot(t.astype(BF16),
                   jnp.concatenate([_pair_blockdiag(vb[rows, cols]), _pair_blockdiag(kbe[rows, cols])], axis=1),
                   preferred_element_type=F32) for t, (rows, cols) in zip(t_invs, tiles)]
    u2s = [uw[:, :LANES] for uw in uws]
    w2s = [uw[:, LANES:] for uw in uws]
    g_lasts = [gc_w[rows.stop - 1:rows.stop, cols] for rows, cols in tiles]
    k_decs = [k[rows, cols] * jnp.exp(gl - gc_w[rows, cols]) for (rows, cols), gl in zip(tiles, g_lasts)]
    kd_uw = [_bdot_tn(kd, jnp.concatenate([u2, w2], axis=1)) for kd, u2, w2 in zip(k_decs, u2s, w2s)]
    kd_u = [jnp.where(pair_diag, x[:, :LANES], 0.0) for x in kd_uw]
    kd_w = [jnp.where(pair_diag, x[:, LANES:], 0.0).astype(BF16) for x in kd_uw]

    state = [st_ref[p] for p in range(N_PAIRS)]
    for c in range(tl // C):
        ts = range(c * N_PAIRS, (c + 1) * N_PAIRS)
        sbs = [s.astype(BF16) for s in state]
        state = [s * jnp.exp(g_lasts[t]) + kd_u[t] - jnp.dot(kd_w[t], sb, preferred_element_type=F32)
                 for t, s, sb in zip(ts, state, sbs)]
        v_news = [u2s[t] - jnp.dot(w2s[t].astype(BF16), sb, preferred_element_type=F32) for t, sb in zip(ts, sbs)]
        o2s = [jnp.dot(qe[tiles[t][0], tiles[t][1]].astype(BF16), sb, preferred_element_type=F32)
               + _pair_mm(attns[t], v_new) for t, sb, v_new in zip(ts, sbs, v_news)]
        o_ref[c * C:(c + 1) * C, :] = jnp.concatenate(o2s, axis=1)
    for p in range(N_PAIRS):
        st_ref[p] = state[p]

    o = o_ref[...]
    o = o * lax.rsqrt(_bdot(o * o, same_head) * (1.0 / HEAD_DIM) + RMS_EPS) * nw_ref[...]
    o_ref[...] = o * _silu(g_ref[...])


def _gdn(proj3, conv_w, conv_b, dt_bias, a_log, norm_w):
    B, S, _ = proj3.shape
    tl = GDN_TILE
    full = lambda shape: pl.BlockSpec(shape, lambda b, c: (0, 0))
    t = np.arange(tl)
    tri = (t[None, :] <= t[:, None]) & (t[:, None] // DELTA_CHUNK == t[None, :] // DELTA_CHUNK)
    head_of = np.arange(W_MIX) // HEAD_DIM
    same = head_of[:, None] == head_of[None, :]
    lanes = np.arange(LANES)[:, None]
    return pl.pallas_call(
        functools.partial(_gdn_kernel, tl=tl),
        out_shape=jax.ShapeDtypeStruct((B, S, W_MIX), F32),
        grid=(B, S // tl),
        in_specs=[pl.BlockSpec((None, tl, 3 * W_MIX), lambda b, c: (b, c, C_DN_QKV // (3 * W_MIX))),
                  pl.BlockSpec((None, tl, W_MIX), lambda b, c: (b, c, C_DN_G // W_MIX)),
                  pl.BlockSpec((None, tl, LANES), lambda b, c: (b, c, C_SMALL_B // LANES)),
                  full((CONV_WIDTH, 3 * W_MIX)), full((1, 3 * W_MIX)), full((1, LANES)), full((1, LANES)),
                  full((1, W_MIX)), full((W_MIX, W_MIX)), full((tl, tl)), full((LANES, W_MIX)), full((LANES, W_MIX))],
        out_specs=pl.BlockSpec((None, tl, W_MIX), lambda b, c: (b, c, 0)),
        scratch_shapes=[pltpu.VMEM((TAIL, 3 * W_MIX), F32), pltpu.VMEM((N_PAIRS, LANES, LANES), F32)],
        compiler_params=pltpu.CompilerParams(
            dimension_semantics=("parallel", "arbitrary"), vmem_limit_bytes=VMEM_LIMIT),
        name="gdn",
    )(proj3, proj3, proj3, conv_w, conv_b.reshape(1, -1),
      _pad_lanes(dt_bias, SM_DECAY), _pad_lanes(a_log, SM_DECAY), jnp.tile(norm_w.astype(F32), N_HEADS).reshape(1, -1),
      jnp.asarray(same, BF16), jnp.asarray(tri, BF16),
      jnp.asarray(head_of[None, :] + SM_BETA == lanes, BF16), jnp.asarray(head_of[None, :] + SM_DECAY == lanes, BF16))


def _outproj_kernel(x_ref, ya_ref, yb_ref, yc_ref, yd_ref, w_ref, lg_ref, lb_ref, o_ref):
    mix = (_bdot(ya_ref[...], w_ref[0]) + _bdot(yb_ref[...], w_ref[1])
           + _bdot(yc_ref[...], w_ref[2]) + _bdot(yd_ref[...], w_ref[3]))
    r = DEEPNORM_ALPHA * x_ref[...] + mix
    mu = jnp.mean(r, axis=1, keepdims=True)
    var = jnp.mean(jnp.square(r - mu), axis=1, keepdims=True)
    o_ref[...] = (r - mu) * lax.rsqrt(var + LN_EPS) * lg_ref[...] + lb_ref[...]


def _outproj(x2d, ya, yb, yc, yd, w_out, ln_g, ln_b):
    T = x2d.shape[0]
    tm = min(OUTPROJ_ROWS, T)
    rowblk = lambda width: pl.BlockSpec((tm, width), lambda i: (i, 0))
    return pl.pallas_call(
        _outproj_kernel,
        out_shape=jax.ShapeDtypeStruct((T, D_MODEL), F32),
        grid=(T // tm,),
        in_specs=[rowblk(D_MODEL), rowblk(W_MIX), rowblk(W_MIX), rowblk(W_MIX), rowblk(W_MIX),
                  pl.BlockSpec((4, W_MIX, D_MODEL), lambda i: (0, 0, 0)),
                  pl.BlockSpec((1, D_MODEL), lambda i: (0, 0)), pl.BlockSpec((1, D_MODEL), lambda i: (0, 0))],
        out_specs=rowblk(D_MODEL),
        compiler_params=pltpu.CompilerParams(
            dimension_semantics=("parallel",), vmem_limit_bytes=VMEM_LIMIT),
        name="outproj",
    )(x2d, ya, yb, yc, yd, w_out, ln_g.reshape(1, -1), ln_b.reshape(1, -1))


def _alibi_slopes():
    n = 2 * N_HEADS
    s = (2.0 ** (-8.0 * (np.arange(n) + 1) / n)).astype(np.float32)
    return jnp.asarray(s[:N_HEADS]), jnp.asarray(s[N_HEADS:])


def _layer(x, w_in_t, layer, w_out, ssm_conv_w, ssm_conv_b, ssm_dt_bias, ssm_A_log, ssm_D, ssm_norm_w,
           dn_conv_w, dn_conv_b, dn_dt_bias, dn_A_log, dn_norm_w, ln_g, ln_b):
    B, S, _ = x.shape
    T = B * S
    slopes_dil, slopes_moba = _alibi_slopes()
    x2d = x.reshape(T, D_MODEL)
    proj2d = _inproj(x2d, w_in_t, layer)
    proj3 = proj2d.reshape(B, S, PROJ_COLS)

    ya = _moba(proj3, slopes_moba)
    yb = _ssd(proj3, ssm_conv_w, ssm_conv_b, ssm_dt_bias, ssm_A_log, ssm_D, ssm_norm_w)
    yd = _gdn(proj3, dn_conv_w, dn_conv_b, dn_dt_bias, dn_A_log, dn_norm_w)

    yc = _dilated(proj3, slopes_dil)

    out = _outproj(x2d, ya.reshape(T, W_MIX), yb.reshape(T, W_MIX), yc.reshape(T, W_MIX), yd.reshape(T, W_MIX),
                   w_out.reshape(4, W_MIX, D_MODEL).astype(BF16), ln_g, ln_b)
    return out.reshape(B, S, D_MODEL)


def kernel(x, w_in, w_out, ssm_conv_w, ssm_conv_b, ssm_dt_bias, ssm_A_log, ssm_D, ssm_norm_w,
           dn_conv_w, dn_conv_b, dn_dt_bias, dn_A_log, dn_norm_w, ln_g, ln_b):
    assert x.shape[1] % 2048 == 0 and x.shape[2] == D_MODEL
    w_in_t = _repack_w_in(jnp.transpose(w_in, (2, 0, 1)))
    for l in range(DEPTH):
        x = _layer(x, w_in_t, l, w_out[l], ssm_conv_w[l], ssm_conv_b[l], ssm_dt_bias[l], ssm_A_log[l],
                   ssm_D[l], ssm_norm_w[l], dn_conv_w[l], dn_conv_b[l], dn_dt_bias[l], dn_A_log[l],
                   dn_norm_w[l], ln_g[l], ln_b[l])
    return x
```

```python
import functools
import math

import numpy as np
import jax
import jax.numpy as jnp
from jax import lax
from jax.experimental import pallas as pl
from jax.experimental.pallas import tpu as pltpu

F32 = jnp.float32
BF16 = jnp.bfloat16
HIGHEST = lax.Precision.HIGHEST

D_MODEL = 1024
DEPTH = 2
HEAD_DIM = 64
N_HEADS = 6
W_MIX = N_HEADS * HEAD_DIM
N_PAIRS = W_MIX // 128
D_MIX = 4 * W_MIX
MOBA_BLOCK = 256
MOBA_TOPK = 3
MOBA_NFEAT = 3
MOBA_UNROLL = 2
MOBA_QBLOCKS = 4
MOBA_LROWS = 16
SSD_STATE = 128
SSD_GROUPS = 2
SSD_CHUNK = 128
SSD_TILE = 512
SSD_XBC = W_MIX + 2 * SSD_GROUPS * SSD_STATE
CONV_WIDTH = 4
DIL_DILATIONS = (1, 4, 16)
DIL_GROUPS = len(DIL_DILATIONS)
DIL_BATCH = 16
DIL_MERGE_ROWS = 256
DIL_SPAN = 128
DELTA_CHUNK = 64
LN_EPS = 1e-5
RMS_EPS = 1e-6
DEEPNORM_ALPHA = (2.0 * DEPTH) ** 0.25
ATTN_SCALE = HEAD_DIM ** -0.5
LOG2E = math.log2(math.e)
NEG = -1e30

LANES = 128
TAIL = 8
VMEM_LIMIT = 48 * 1024 * 1024

INPROJ_ROWS = 1024
INPROJ_COLS = 2048
OUTPROJ_ROWS = 1024
GDN_TILE = 512

C_MOBA = 0
C_SSD_X = 1536
C_SSD_Z = 1920
C_DN_QKV = 2304
C_DN_G = 3456
C_DIL = 3840
C_SSD_B = 5376
C_SSD_C = 5632
C_SMALL_A = 5888
C_SMALL_B = 6016
PROJ_COLS = 6144
SM_DT, SM_BETA, SM_DECAY = 0, 116, 122

_R = [int(v) for v in np.cumsum([0, 384, 384, 384, 384, SSD_XBC, 384, 6, 384, 384, 384, 384, 1152, 384, 6, 6])]
R_SSD_XBC, R_SSD_Z, R_SSD_DT, R_DIL, R_DN_QKV, R_DN_G, R_END = _R[4], _R[5], _R[6], _R[7], _R[11], _R[12], _R[15]
_SEGMENTS = ((C_MOBA, 0, 4 * W_MIX), (C_SSD_X, R_SSD_XBC, W_MIX), (C_SSD_Z, R_SSD_Z, W_MIX),
             (C_DN_QKV, R_DN_QKV, 3 * W_MIX), (C_DN_G, R_DN_G, W_MIX), (C_DIL, R_DIL, 4 * W_MIX),
             (C_SSD_B, R_SSD_XBC + W_MIX, 2 * SSD_GROUPS * SSD_STATE), (C_SMALL_A, R_SSD_DT, LANES),
             (C_SMALL_B, R_END - LANES, LANES))
_BLOCK_SRC = np.zeros(PROJ_COLS // LANES, np.int32)
for _dst, _src, _n in _SEGMENTS:
    _BLOCK_SRC[_dst // LANES:(_dst + _n) // LANES] = _src + LANES * np.arange(_n // LANES)
assert R_SSD_DT + LANES <= R_END


def _repack_kernel(src_ref, w_ref, o_ref):
    for layer in range(o_ref.shape[0]):
        o_ref[layer] = w_ref[:, layer, :].astype(BF16)


def _repack_w_in(w_t):
    _, depth, dm = w_t.shape
    return pl.pallas_call(
        _repack_kernel,
        out_shape=jax.ShapeDtypeStruct((depth, PROJ_COLS, dm), BF16),
        grid_spec=pltpu.PrefetchScalarGridSpec(
            num_scalar_prefetch=1, grid=(PROJ_COLS // LANES,),
            in_specs=[pl.BlockSpec((pl.Element(LANES), pl.Element(depth), pl.Element(dm)),
                                   lambda b, src: (src[b], 0, 0))],
            out_specs=pl.BlockSpec((depth, LANES, dm), lambda b, src: (0, b, 0))),
        compiler_params=pltpu.CompilerParams(dimension_semantics=("parallel",), vmem_limit_bytes=VMEM_LIMIT),
        name="repack_w_in",
    )(jnp.asarray(_BLOCK_SRC), w_t)


def _silu(x):
    return x * jax.nn.sigmoid(x)


def _softplus(x):
    return jnp.maximum(x, 0.0) + jnp.log(1.0 + jnp.exp(-jnp.abs(x)))


def _bdot(a, b):
    return jnp.dot(a.astype(BF16), b.astype(BF16), preferred_element_type=F32)


def _bdot_nt(a, b):
    return lax.dot_general(a.astype(BF16), b.astype(BF16), (((1,), (1,)), ((), ())),
                           preferred_element_type=F32)


def _bdot_tn(a, b):
    return lax.dot_general(a.astype(BF16), b.astype(BF16), (((0,), (0,)), ((), ())),
                           preferred_element_type=F32)


def _causal_conv_silu(x_ref, tail, w, b):
    L = x_ref.shape[0]
    head = jnp.concatenate([tail, x_ref[:TAIL, :]], axis=0)
    y_head, y_rest = b, b
    for j in range(CONV_WIDTH):
        off = TAIL - (CONV_WIDTH - 1) + j
        y_head = y_head + w[j:j + 1, :] * head[off:off + TAIL, :]
        y_rest = y_rest + w[j:j + 1, :] * x_ref[pl.ds(off, L - TAIL), :]
    return _silu(jnp.concatenate([y_head, y_rest], axis=0))


def _inproj_kernel(x_ref, w_ref, o_ref, xb_ref):
    @pl.when(pl.program_id(1) == 0)
    def _():
        xb_ref[...] = x_ref[...].astype(BF16)

    o_ref[...] = lax.dot_general(xb_ref[...], w_ref[...], (((1,), (1,)), ((), ())), preferred_element_type=F32)


def _inproj(x2d, w_t, layer):
    T = x2d.shape[0]
    tm = min(INPROJ_ROWS, T)
    tn = INPROJ_COLS
    return pl.pallas_call(
        _inproj_kernel,
        out_shape=jax.ShapeDtypeStruct((T, PROJ_COLS), F32),
        grid=(T // tm, PROJ_COLS // tn),
        in_specs=[pl.BlockSpec((tm, D_MODEL), lambda i, j: (i, 0)),
                  pl.BlockSpec((None, tn, D_MODEL), lambda i, j: (layer, j, 0))],
        out_specs=pl.BlockSpec((tm, tn), lambda i, j: (i, j)),
        scratch_shapes=[pltpu.VMEM((tm, D_MODEL), BF16)],
        compiler_params=pltpu.CompilerParams(
            dimension_semantics=("parallel", "arbitrary"), vmem_limit_bytes=VMEM_LIMIT),
        name="inproj",
    )(x2d, w_t)


def _moba_kernel(slopes_ref, q_ref, k_ref, v_ref, g_ref, o_ref, kaug_ref, vt_ref, kmean_ref, s_ref, *, nblk):
    p = pl.program_id(1)
    L = MOBA_BLOCK
    NF = MOBA_NFEAT

    @pl.when(pl.program_id(2) == 0)
    def _():
        lane = lax.broadcasted_iota(jnp.int32, (L, LANES), 1)
        feat_lane = lane % HEAD_DIM
        c_loc = lax.broadcasted_iota(jnp.int32, (L, LANES), 0).astype(F32)
        feat = jnp.where(feat_lane < NF, 1.0, jnp.where(feat_lane < 2 * NF, c_loc, 0.0))
        ones = jnp.ones((MOBA_LROWS, L), F32)
        for n in range(nblk):
            kblk = k_ref[n * L:(n + 1) * L, :]
            kmean_ref[n:n + 1, :] = jnp.mean(kblk, axis=0, keepdims=True)
            kaug_ref[0, n] = jnp.where(lane < HEAD_DIM, kblk, feat).astype(BF16)
            kaug_ref[1, n] = jnp.where(lane < HEAD_DIM, feat, kblk).astype(BF16)
            vt = v_ref[n * L:(n + 1) * L, :].T
            for h in range(2):
                vt_ref[h, n] = jnp.concatenate([ones, vt[h * HEAD_DIM:(h + 1) * HEAD_DIM, :]], axis=0).astype(BF16)

    QB = MOBA_QBLOCKS
    streams = [(qb, h) for qb in range(QB) for h in range(2)]
    i0 = QB * pl.program_id(2)
    qt_all = q_ref[...].T
    qt = [qt_all[:, qb * L:(qb + 1) * L] for qb in range(QB)]
    qrow = lax.broadcasted_iota(jnp.int32, (LANES, L), 0)
    r_loc = lax.broadcasted_iota(jnp.int32, (1, L), 1).astype(F32)
    brow = lax.broadcasted_iota(jnp.int32, (nblk, L), 0).astype(F32)
    frow = lax.broadcasted_iota(jnp.int32, (2 * TAIL, L), 0)
    kmean = kmean_ref[...]
    i_f = [(i0 + qb).astype(F32) for qb in range(QB)]

    qparts = [(qt[qb][h * HEAD_DIM:(h + 1) * HEAD_DIM, :] * (ATTN_SCALE * LOG2E)).astype(BF16) for qb, h in streams]
    slope_rows = [jnp.full((1, L), slopes_ref[2 * p + h] * LOG2E, F32) for h in range(2)]
    zeros48 = jnp.zeros((HEAD_DIM - 2 * TAIL, L), BF16)

    def q_aug(st, t_row):
        h = streams[st][1]
        terms = _split3(t_row) + _split3(slope_rows[h])
        feat = jnp.zeros((2 * TAIL, L), F32)
        for f, term in enumerate(terms):
            feat = jnp.where(frow == f, term.astype(F32), feat)
        feat = feat.astype(BF16)
        if h == 0:
            return jnp.concatenate([qparts[st], feat, zeros48], axis=0)
        return jnp.concatenate([feat, zeros48, qparts[st]], axis=0)

    def past_row(st, n, in_range=None):
        qb, h = streams[st]
        n_f = n.astype(F32)
        picked = (sels[st][0] == n_f) | (sels[st][1] == n_f) | (sels[st][2] == n_f)
        if in_range is not None:
            picked = picked & in_range
        return jnp.where(picked, -slope_rows[h] * (r_loc + L * (i_f[qb] - n_f)), NEG)

    U = MOBA_UNROLL
    NS = len(streams)

    def put_scores(slot, n0):
        for u in range(U):
            n = n0 + u
            for st, (qb, h) in enumerate(streams):
                s_ref[slot, st, u] = jnp.dot(kaug_ref[h, jnp.minimum(n, nblk - 1)], q_aug(st, past_row(st, n, n < i0)),
                                             preferred_element_type=F32).astype(BF16)

    def new_max(slot, carry):
        return [functools.reduce(jnp.maximum, [carry[2 * st]] + [
            jnp.max(s_ref[slot, st, u], axis=0, keepdims=True).astype(F32) for u in range(U)]) for st in range(NS)]

    def absorb(slot, n0, m_new, carry):
        ps = [jnp.concatenate([jnp.exp2(s_ref[slot, st, u] - m_new[st].astype(BF16)) for u in range(U)], axis=0)
              for st in range(NS)]
        vts = [jnp.concatenate([vt_ref[h, jnp.minimum(n0 + u, nblk - 1)] for u in range(U)], axis=1) for h in range(2)]
        pv = [jnp.dot(vts[streams[st][1]], ps[st], preferred_element_type=F32) for st in range(NS)]
        out = []
        for st in range(NS):
            out += [m_new[st], jnp.exp2(carry[2 * st] - m_new[st]) * carry[2 * st + 1] + pv[st]]
        return tuple(out)

    c_idx = lax.broadcasted_iota(jnp.int32, (L, L), 0)
    r_idx = lax.broadcasted_iota(jnp.int32, (L, L), 1)
    own = [jnp.where(c_idx <= r_idx,
                     jnp.dot(kaug_ref[h, i0 + qb], q_aug(st, -slope_rows[h] * r_loc), preferred_element_type=F32),
                     NEG).astype(BF16) for st, (qb, h) in enumerate(streams)]

    gates = [jnp.where(brow < i_f[qb],
                       jnp.dot(kmean, jnp.where((qrow >= HEAD_DIM * h) & (qrow < HEAD_DIM * (h + 1)), qt[qb], 0.0),
                               precision=HIGHEST, preferred_element_type=F32), -jnp.inf) for qb, h in streams]
    sels = [[] for _ in streams]
    for kk in range(MOBA_TOPK):
        for st, (qb, h) in enumerate(streams):
            mx = jnp.max(gates[st], axis=0, keepdims=True)
            idx = jnp.min(jnp.where(gates[st] == mx, brow, float(nblk)), axis=0, keepdims=True)
            sels[st].append(jnp.where(kk < i0 + qb, idx, -1.0))
            gates[st] = jnp.where(brow == idx, -jnp.inf, gates[st])

    init = []
    for st, (qb, h) in enumerate(streams):
        tiles = [jnp.dot(kaug_ref[h, i0 + e], q_aug(st, past_row(st, i0 + e)),
                         preferred_element_type=F32).astype(BF16) for e in range(qb)] + [own[st]]
        m0 = functools.reduce(jnp.maximum, [jnp.max(s, axis=0, keepdims=True) for s in tiles])
        p0 = jnp.concatenate([jnp.exp2(s - m0) for s in tiles], axis=0)
        vt0 = jnp.concatenate([vt_ref[h, i0 + e] for e in range(qb + 1)], axis=1)
        init += [m0.astype(F32), jnp.dot(vt0, p0, preferred_element_type=F32)]

    put_scores(0, jnp.int32(0))

    def body(j, carry):
        n0 = 2 * U * j
        m_a = new_max(0, carry)
        put_scores(1, n0 + U)
        carry = absorb(0, n0, m_a, carry)
        m_b = new_max(1, carry)
        put_scores(0, n0 + 2 * U)
        return absorb(1, n0 + U, m_b, carry)

    res = lax.fori_loop(0, (i0 + 2 * U - 1) // (2 * U), body, tuple(init))
    for qb in range(QB):
        out_t = jnp.concatenate([res[2 * st + 1][MOBA_LROWS:] / res[2 * st + 1][0:1]
                                 for st in (2 * qb, 2 * qb + 1)], axis=0)
        o_ref[qb * L:(qb + 1) * L, :] = (out_t.T * _silu(g_ref[qb * L:(qb + 1) * L, :])).astype(BF16)


def _moba(proj3, slopes):
    B, S, _ = proj3.shape
    L = MOBA_BLOCK
    nblk = S // L
    QB = MOBA_QBLOCKS
    cb = C_MOBA // LANES
    return pl.pallas_call(
        functools.partial(_moba_kernel, nblk=nblk),
        out_shape=jax.ShapeDtypeStruct((B, S, W_MIX), BF16),
        grid=(B, N_PAIRS, nblk // QB),
        in_specs=[pl.BlockSpec(memory_space=pltpu.SMEM),
                  pl.BlockSpec((None, QB * L, LANES), lambda b, p, i: (b, i, cb + p)),
                  pl.BlockSpec((None, S, LANES), lambda b, p, i: (b, 0, cb + N_PAIRS + p)),
                  pl.BlockSpec((None, S, LANES), lambda b, p, i: (b, 0, cb + 2 * N_PAIRS + p)),
                  pl.BlockSpec((None, QB * L, LANES), lambda b, p, i: (b, i, cb + 3 * N_PAIRS + p))],
        out_specs=pl.BlockSpec((None, QB * L, LANES), lambda b, p, i: (b, i, p)),
        scratch_shapes=[pltpu.VMEM((2, nblk, L, LANES), BF16),
                        pltpu.VMEM((2, nblk, MOBA_LROWS + HEAD_DIM, L), BF16),
                        pltpu.VMEM((nblk, LANES), F32),
                        pltpu.VMEM((2, 2 * QB, MOBA_UNROLL, L, L), BF16)],
        compiler_params=pltpu.CompilerParams(
            dimension_semantics=("parallel", "parallel", "arbitrary"), vmem_limit_bytes=VMEM_LIMIT),
        name="moba",
    )(slopes, proj3, proj3, proj3, proj3)


def _dil_kernel(slopes_ref, q_ref, kc_ref, kp_ref, vc_ref, vp_ref, g_ref, o_ref, acc_ref, m_ref, l_ref, *, tq):
    p = pl.program_id(1)
    first = pl.program_id(2) == 0
    SB = DIL_SPAN
    lane = lax.broadcasted_iota(jnp.int32, (1, LANES), 1)
    lo = lane < HEAD_DIM
    row = lax.broadcasted_iota(jnp.int32, (SB, 2 * SB), 0)
    col = lax.broadcasted_iota(jnp.int32, (SB, 2 * SB), 1)
    dist = (row + SB - col).astype(F32)
    band = (col >= row) & (col <= row + SB)
    no_halo = jnp.logical_and(first, col < SB)

    def rows(start, d):
        return pl.ds(start, SB) if d == 1 else pl.ds(start, SB, stride=d)

    for gi, d in enumerate(DIL_DILATIONS):
        span = d * SB
        bias, bias_first = [], []
        for h in range(2):
            slope = slopes_ref[2 * p + h] * (float(d) * LOG2E)
            bias.append(jnp.where(band, -slope * dist, NEG))
            bias_first.append(jnp.where(no_halo, NEG, bias[h]))
        problems = [(r, j) for j in range(tq // span) for r in range(d)]
        for c0 in range(0, len(problems), DIL_BATCH):
            chunk = problems[c0:c0 + DIL_BATCH]
            blocks = {}

            def block(r, j):
                if (r, j) not in blocks:
                    if j < 0:
                        sel = rows(tq - span + r, d)
                        kb, vb = kp_ref[sel, :], vp_ref[sel, :]
                    else:
                        sel = rows(r + span * j, d)
                        kb, vb = kc_ref[sel, :], vc_ref[sel, :]
                    blocks[(r, j)] = (kb.astype(BF16),
                                      (jnp.where(lo, vb, 1.0).astype(BF16), jnp.where(lo, 1.0, vb).astype(BF16)))
                return blocks[(r, j)]

            kv, qs = [], []
            for r, j in chunk:
                (kp, vp), (kc, vc) = block(r, j - 1), block(r, j)
                kv.append((jnp.concatenate([kp, kc], axis=0),
                           tuple(jnp.concatenate([vp[h], vc[h]], axis=0) for h in range(2))))
                qs.append(q_ref[rows(r + span * j, d), :] * (ATTN_SCALE * LOG2E))
            work = [(c, h) for c in range(len(chunk)) for h in range(2)]
            ss = [(_bdot_nt(jnp.where(lo if h == 0 else jnp.logical_not(lo), qs[c], 0.0), kv[c][0])
                   + (bias_first[h] if chunk[c][1] == 0 else bias[h])).astype(BF16) for c, h in work]
            ms = [jnp.max(s, axis=1, keepdims=True) for s in ss]
            ps = [jnp.exp2(s - m) for s, m in zip(ss, ms)]
            pvs = [jnp.dot(e, kv[c][1][h], preferred_element_type=F32) for e, (c, h) in zip(ps, work)]
            for c, (r, j) in enumerate(chunk):
                cur = rows(r + span * j, d)
                acc_ref[gi, cur, :] = jnp.where(lo, pvs[2 * c], pvs[2 * c + 1])
                m_ref[gi, cur, :] = jnp.where(lo, ms[2 * c].astype(F32), ms[2 * c + 1].astype(F32))
                l_ref[gi, cur, :] = pltpu.roll(jnp.where(lo, pvs[2 * c + 1], pvs[2 * c]), HEAD_DIM, axis=1)

    for c0 in range(0, tq, DIL_MERGE_ROWS):
        rs = slice(c0, c0 + DIL_MERGE_ROWS)
        ms = [m_ref[gi, rs, :] for gi in range(DIL_GROUPS)]
        top = functools.reduce(jnp.maximum, ms)
        ws = [jnp.exp2(m - top) for m in ms]
        num = sum(w * acc_ref[gi, rs, :] for gi, w in enumerate(ws))
        den = sum(w * l_ref[gi, rs, :] for gi, w in enumerate(ws))
        o_ref[rs, :] = (num / den * _silu(g_ref[rs, :])).astype(BF16)


def _dilated(proj3, slopes):
    B, S, _ = proj3.shape
    tq = DIL_DILATIONS[-1] * DIL_SPAN
    base = C_DIL // LANES
    cur = lambda off: pl.BlockSpec((None, tq, LANES), lambda b, p, t: (b, t, base + off + p))
    prev = lambda off: pl.BlockSpec((None, tq, LANES), lambda b, p, t: (b, jnp.maximum(t - 1, 0), base + off + p))
    scratch = pltpu.VMEM((DIL_GROUPS, tq, LANES), F32)
    return pl.pallas_call(
        functools.partial(_dil_kernel, tq=tq),
        out_shape=jax.ShapeDtypeStruct((B, S, W_MIX), BF16),
        grid=(B, N_PAIRS, S // tq),
        in_specs=[pl.BlockSpec(memory_space=pltpu.SMEM),
                  cur(0), cur(N_PAIRS), prev(N_PAIRS), cur(2 * N_PAIRS), prev(2 * N_PAIRS), cur(3 * N_PAIRS)],
        out_specs=pl.BlockSpec((None, tq, LANES), lambda b, p, t: (b, t, p)),
        scratch_shapes=[scratch, scratch, scratch],
        compiler_params=pltpu.CompilerParams(
            dimension_semantics=("parallel", "parallel", "arbitrary"), vmem_limit_bytes=VMEM_LIMIT),
        name="dilated",
    )(slopes, proj3, proj3, proj3, proj3, proj3, proj3)


def _ssd_kernel(x_ref, z_ref, b_ref, c_ref, sm_ref, cw_ref, cb_ref, dtb_ref, alog_ref, dskip_ref, nw_ref,
                tri_ref, expand_ref, o_ref, tx_ref, tb_ref, tc_ref, st_ref, *, tl):
    L = SSD_CHUNK
    N = SSD_STATE
    nchunk = tl // L

    @pl.when(pl.program_id(1) == 0)
    def _():
        tx_ref[...] = jnp.zeros_like(tx_ref)
        tb_ref[...] = jnp.zeros_like(tb_ref)
        tc_ref[...] = jnp.zeros_like(tc_ref)
        st_ref[...] = jnp.zeros_like(st_ref)

    cw = cw_ref[...]
    cb = cb_ref[...]
    xs = _causal_conv_silu(x_ref, tx_ref[...], cw[:, :W_MIX], cb[:, :W_MIX])
    Bm = _causal_conv_silu(b_ref, tb_ref[...], cw[:, W_MIX:W_MIX + 2 * N], cb[:, W_MIX:W_MIX + 2 * N])
    Cm = _causal_conv_silu(c_ref, tc_ref[...], cw[:, W_MIX + 2 * N:], cb[:, W_MIX + 2 * N:])
    tx_ref[...] = x_ref[tl - TAIL:, :]
    tb_ref[...] = b_ref[tl - TAIL:, :]
    tc_ref[...] = c_ref[tl - TAIL:, :]

    dt = _softplus(sm_ref[...] + dtb_ref[...])
    a = dt * (-jnp.exp(alog_ref[...]))
    a_cum = _xdot(tri_ref[...], a)
    expand = expand_ref[...]
    dt_w = _xdot_r(dt, expand)
    acum_w = _xdot_r(a_cum, expand)
    xdt = xs * dt_w
    a_cum_t = a_cum.T
    e_acum = jnp.exp(acum_w)

    lane = lax.broadcasted_iota(jnp.int32, (1, LANES), 1)
    lane_w = lax.broadcasted_iota(jnp.int32, (1, W_MIX), 1)
    in_g0 = lane_w < W_MIX // SSD_GROUPS
    row = lax.broadcasted_iota(jnp.int32, (L, L), 0)
    col = lax.broadcasted_iota(jnp.int32, (L, L), 1)
    causal = col <= row
    chunks = [slice(c * L, (c + 1) * L) for c in range(nchunk)]
    heads_per_group = N_HEADS // SSD_GROUPS

    scores = [[_bdot_nt(Cm[rs, g * N:(g + 1) * N], Bm[rs, g * N:(g + 1) * N]) for g in range(SSD_GROUPS)]
              for rs in chunks]
    atot = [acum_w[rs.stop - 1:rs.stop, :] for rs in chunks]
    xdec = [xdt[rs, :] * jnp.exp(atot[c] - acum_w[rs, :]) for c, rs in enumerate(chunks)]
    contrib = [jnp.where(in_g0, _bdot_tn(Bm[rs, :N], xdec[c]), _bdot_tn(Bm[rs, N:], xdec[c]))
               for c, rs in enumerate(chunks)]
    lmats = [[jnp.exp(jnp.where(causal, a_cum[rs, h:h + 1] - a_cum_t[h:h + 1, rs], NEG)) for h in range(N_HEADS)]
             for rs in chunks]
    diag = [[_bdot(scores[c][h // heads_per_group] * lmats[c][h], xdt[rs, (h // 2) * LANES:(h // 2 + 1) * LANES])
             for h in range(N_HEADS)] for c, rs in enumerate(chunks)]

    states = [st_ref[...]]
    for c in range(nchunk):
        states.append(states[c] * jnp.exp(atot[c]) + contrib[c])
    st_ref[...] = states[nchunk]
    sb = [s.astype(BF16) for s in states[:nchunk]]
    off = [jnp.where(in_g0, jnp.dot(Cm[rs, :N].astype(BF16), sb[c], preferred_element_type=F32),
                     jnp.dot(Cm[rs, N:].astype(BF16), sb[c], preferred_element_type=F32))
           for c, rs in enumerate(chunks)]

    for c, rs in enumerate(chunks):
        y_diag = jnp.concatenate([jnp.where(lane < HEAD_DIM, diag[c][2 * p], diag[c][2 * p + 1])
                                  for p in range(N_PAIRS)], axis=1)
        y = y_diag + off[c] * e_acum[rs, :] + dskip_ref[...] * xs[rs, :]
        y = y * _silu(z_ref[rs, :])
        o_ref[rs, :] = (y * lax.rsqrt(jnp.mean(y * y, axis=1, keepdims=True) + RMS_EPS) * nw_ref[...]).astype(BF16)


def _pad_lanes(v, offset):
    return jnp.zeros((1, LANES), F32).at[0, offset:offset + v.shape[0]].set(v.astype(F32))


def _ssd(proj3, conv_w, conv_b, dt_bias, a_log, d_skip, norm_w):
    B, S, _ = proj3.shape
    L = SSD_TILE
    N2 = 2 * SSD_STATE
    full = lambda shape: pl.BlockSpec(shape, lambda b, c: (0, 0))
    t = np.arange(L)
    tri = (t[None, :] <= t[:, None]) & (t[:, None] // SSD_CHUNK == t[None, :] // SSD_CHUNK)
    expand = np.arange(W_MIX)[None, :] // HEAD_DIM == np.arange(LANES)[:, None]
    return pl.pallas_call(
        functools.partial(_ssd_kernel, tl=L),
        out_shape=jax.ShapeDtypeStruct((B, S, W_MIX), BF16),
        grid=(B, S // L),
        in_specs=[pl.BlockSpec((None, L, W_MIX), lambda b, c: (b, c, C_SSD_X // W_MIX)),
                  pl.BlockSpec((None, L, W_MIX), lambda b, c: (b, c, C_SSD_Z // W_MIX)),
                  pl.BlockSpec((None, L, N2), lambda b, c: (b, c, C_SSD_B // N2)),
                  pl.BlockSpec((None, L, N2), lambda b, c: (b, c, C_SSD_C // N2)),
                  pl.BlockSpec((None, L, LANES), lambda b, c: (b, c, C_SMALL_A // LANES)),
                  full((CONV_WIDTH, SSD_XBC)), full((1, SSD_XBC)), full((1, LANES)), full((1, LANES)),
                  full((1, W_MIX)), full((1, W_MIX)), full((L, L)), full((LANES, W_MIX))],
        out_specs=pl.BlockSpec((None, L, W_MIX), lambda b, c: (b, c, 0)),
        scratch_shapes=[pltpu.VMEM((TAIL, W_MIX), F32), pltpu.VMEM((TAIL, N2), F32),
                        pltpu.VMEM((TAIL, N2), F32), pltpu.VMEM((SSD_STATE, W_MIX), F32)],
        compiler_params=pltpu.CompilerParams(
            dimension_semantics=("parallel", "arbitrary"), vmem_limit_bytes=VMEM_LIMIT),
        name="ssd",
    )(proj3, proj3, proj3, proj3, proj3, conv_w, conv_b.reshape(1, -1),
      _pad_lanes(dt_bias, SM_DT), _pad_lanes(a_log, SM_DT),
      jnp.repeat(d_skip.astype(F32), HEAD_DIM).reshape(1, -1), norm_w.reshape(1, -1),
      jnp.asarray(tri, BF16), jnp.asarray(expand, BF16))


INV_LEAF = 16


def _pair_blockdiag(x):
    lo = lax.broadcasted_iota(jnp.int32, (1, LANES), 1) < HEAD_DIM
    xb = x.astype(BF16)
    zero = jnp.zeros_like(xb)
    return jnp.concatenate([jnp.where(lo, xb, zero), jnp.where(lo, zero, xb)], axis=0)


def _pair_mm(a, b):
    return jnp.dot(a.astype(BF16), _pair_blockdiag(b), preferred_element_type=F32)


def _pair_unit_lower_inverse(lms, eye, on_diag_block):
    mm = lambda xs, ys: [_pair_mm(x, y) for x, y in zip(xs, ys)]
    d = [jnp.where(on_diag_block, lm, 0.0) for lm in lms]
    off = [lm - x for lm, x in zip(lms, d)]
    d2 = mm(d, d)
    d4 = mm(d2, d2)
    d8 = mm(d4, d4)
    left = mm([eye - x for x in d], [eye + x for x in d2])
    right = mm([eye + x for x in d4], [eye + x for x in d8])
    dinv = mm(left, right)
    m = mm(dinv, off)
    m2 = mm(m, m)
    return mm(mm([eye - x for x in m], [eye + x for x in m2]), dinv)


def _split3(x):
    hi = x.astype(BF16)
    r = x - hi.astype(F32)
    mid = r.astype(BF16)
    lo = (r - mid.astype(F32)).astype(BF16)
    return hi, mid, lo


def _xdot(a01, x):
    a = a01.astype(BF16)
    return sum(jnp.dot(a, t, preferred_element_type=F32) for t in _split3(x))


def _xdot_r(x, b01):
    b = b01.astype(BF16)
    return sum(jnp.dot(t, b, preferred_element_type=F32) for t in _split3(x))


def _gdn_kernel(qkv_ref, g_ref, sm_ref, cw_ref, cb_ref, dtb_ref, alog_ref, nw_ref, same_ref, tri_ref, eb_ref, ed_ref,
                o_ref, tail_ref, st_ref, *, tl):
    C = DELTA_CHUNK

    @pl.when(pl.program_id(1) == 0)
    def _():
        tail_ref[...] = jnp.zeros_like(tail_ref)
        st_ref[...] = jnp.zeros_like(st_ref)

    qkv = _causal_conv_silu(qkv_ref, tail_ref[...], cw_ref[...], cb_ref[...])
    tail_ref[...] = qkv_ref[tl - TAIL:, :]
    q, k, v = qkv[:, :W_MIX], qkv[:, W_MIX:2 * W_MIX], qkv[:, 2 * W_MIX:]

    same_head = same_ref[...]
    q = q * lax.rsqrt(_bdot(q * q, same_head) + RMS_EPS) * ATTN_SCALE
    k = k * lax.rsqrt(_bdot(k * k, same_head) + RMS_EPS)

    sm = sm_ref[...]
    beta = jax.nn.sigmoid(sm)
    gdec = -jnp.exp(alog_ref[...]) * _softplus(sm + dtb_ref[...])
    gc = _xdot(tri_ref[...], gdec)
    beta_w = _xdot_r(beta, eb_ref[...])
    gc_w = _xdot_r(gc, ed_ref[...])

    kb = k * beta_w
    vb = v * beta_w
    egc = jnp.exp(gc_w)
    kbe = kb * egc
    qe = q * egc

    row = lax.broadcasted_iota(jnp.int32, (C, LANES), 0)
    col = lax.broadcasted_iota(jnp.int32, (C, LANES), 1) % HEAD_DIM
    eye = (row == col).astype(F32)
    incl = col <= row
    strict = col < row
    on_diag_block = (row // INV_LEAF) == (col // INV_LEAF)
    s_row = lax.broadcasted_iota(jnp.int32, (LANES, LANES), 0)
    s_col = lax.broadcasted_iota(jnp.int32, (LANES, LANES), 1)
    pair_diag = (s_row // HEAD_DIM) == (s_col // HEAD_DIM)

    tiles = [(slice(c * C, (c + 1) * C), slice(p * LANES, (p + 1) * LANES))
             for c in range(tl // C) for p in range(N_PAIRS)]
    grams, decays = [], []
    for rows, cols in tiles:
        gcol = gc_w[rows, cols]
        grow = jnp.sum(gcol * eye, axis=0, keepdims=True)
        decays.append(jnp.exp(jnp.where(incl, gcol - grow, NEG)))
        grams.append(lax.dot_general(jnp.concatenate([kb[rows, cols], q[rows, cols]], axis=0).astype(BF16),
                                     _pair_blockdiag(k[rows, cols]), (((1,), (1,)), ((), ())),
                                     preferred_element_type=F32))
    lms = [jnp.where(strict, g[:C] * d, 0.0) for g, d in zip(grams, decays)]
    attns = [jnp.where(incl, g[C:] * d, 0.0) for g, d in zip(grams, decays)]
    t_invs = _pair_unit_lower_inverse(lms, eye, on_diag_block)
    uws = [jnp.dot(t.astype(BF16),
                   jnp.concatenate([_pair_blockdiag(vb[rows, cols]), _pair_blockdiag(kbe[rows, cols])], axis=1),
                   preferred_element_type=F32) for t, (rows, cols) in zip(t_invs, tiles)]
    u2s = [uw[:, :LANES] for uw in uws]
    w2s = [uw[:, LANES:] for uw in uws]
    g_lasts = [gc_w[rows.stop - 1:rows.stop, cols] for rows, cols in tiles]
    k_decs = [k[rows, cols] * jnp.exp(gl - gc_w[rows, cols]) for (rows, cols), gl in zip(tiles, g_lasts)]
    kd_uw = [_bdot_tn(kd, jnp.concatenate([u2, w2], axis=1)) for kd, u2, w2 in zip(k_decs, u2s, w2s)]
    kd_u = [jnp.where(pair_diag, x[:, :LANES], 0.0) for x in kd_uw]
    kd_w = [jnp.where(pair_diag, x[:, LANES:], 0.0).astype(BF16) for x in kd_uw]

    state = [st_ref[p] for p in range(N_PAIRS)]
    for c in range(tl // C):
        ts = range(c * N_PAIRS, (c + 1) * N_PAIRS)
        sbs = [s.astype(BF16) for s in state]
        state = [s * jnp.exp(g_lasts[t]) + kd_u[t] - jnp.dot(kd_w[t], sb, preferred_element_type=F32)
                 for t, s, sb in zip(ts, state, sbs)]
        v_news = [u2s[t] - jnp.dot(w2s[t].astype(BF16), sb, preferred_element_type=F32) for t, sb in zip(ts, sbs)]
        o2s = [jnp.dot(qe[tiles[t][0], tiles[t][1]].astype(BF16), sb, preferred_element_type=F32)
               + _pair_mm(attns[t], v_new) for t, sb, v_new in zip(ts, sbs, v_news)]
        o_ref[c * C:(c + 1) * C, :] = jnp.concatenate(o2s, axis=1)
    for p in range(N_PAIRS):
        st_ref[p] = state[p]

    o = o_ref[...]
    o = o * lax.rsqrt(_bdot(o * o, same_head) * (1.0 / HEAD_DIM) + RMS_EPS) * nw_ref[...]
    o_ref[...] = o * _silu(g_ref[...])


def _gdn(proj3, conv_w, conv_b, dt_bias, a_log, norm_w):
    B, S, _ = proj3.shape
    tl = GDN_TILE
    full = lambda shape: pl.BlockSpec(shape, lambda b, c: (0, 0))
    t = np.arange(tl)
    tri = (t[None, :] <= t[:, None]) & (t[:, None] // DELTA_CHUNK == t[None, :] // DELTA_CHUNK)
    head_of = np.arange(W_MIX) // HEAD_DIM
    same = head_of[:, None] == head_of[None, :]
    lanes = np.arange(LANES)[:, None]
    return pl.pallas_call(
        functools.partial(_gdn_kernel, tl=tl),
        out_shape=jax.ShapeDtypeStruct((B, S, W_MIX), F32),
        grid=(B, S // tl),
        in_specs=[pl.BlockSpec((None, tl, 3 * W_MIX), lambda b, c: (b, c, C_DN_QKV // (3 * W_MIX))),
                  pl.BlockSpec((None, tl, W_MIX), lambda b, c: (b, c, C_DN_G // W_MIX)),
                  pl.BlockSpec((None, tl, LANES), lambda b, c: (b, c, C_SMALL_B // LANES)),
                  full((CONV_WIDTH, 3 * W_MIX)), full((1, 3 * W_MIX)), full((1, LANES)), full((1, LANES)),
                  full((1, W_MIX)), full((W_MIX, W_MIX)), full((tl, tl)), full((LANES, W_MIX)), full((LANES, W_MIX))],
        out_specs=pl.BlockSpec((None, tl, W_MIX), lambda b, c: (b, c, 0)),
        scratch_shapes=[pltpu.VMEM((TAIL, 3 * W_MIX), F32), pltpu.VMEM((N_PAIRS, LANES, LANES), F32)],
        compiler_params=pltpu.CompilerParams(
            dimension_semantics=("parallel", "arbitrary"), vmem_limit_bytes=VMEM_LIMIT),
        name="gdn",
    )(proj3, proj3, proj3, conv_w, conv_b.reshape(1, -1),
      _pad_lanes(dt_bias, SM_DECAY), _pad_lanes(a_log, SM_DECAY), jnp.tile(norm_w.astype(F32), N_HEADS).reshape(1, -1),
      jnp.asarray(same, BF16), jnp.asarray(tri, BF16),
      jnp.asarray(head_of[None, :] + SM_BETA == lanes, BF16), jnp.asarray(head_of[None, :] + SM_DECAY == lanes, BF16))


def _outproj_kernel(x_ref, ya_ref, yb_ref, yc_ref, yd_ref, w_ref, lg_ref, lb_ref, o_ref):
    mix = (_bdot(ya_ref[...], w_ref[0]) + _bdot(yb_ref[...], w_ref[1])
           + _bdot(yc_ref[...], w_ref[2]) + _bdot(yd_ref[...], w_ref[3]))
    r = DEEPNORM_ALPHA * x_ref[...] + mix
    mu = jnp.mean(r, axis=1, keepdims=True)
    var = jnp.mean(jnp.square(r - mu), axis=1, keepdims=True)
    o_ref[...] = (r - mu) * lax.rsqrt(var + LN_EPS) * lg_ref[...] + lb_ref[...]


def _outproj(x2d, ya, yb, yc, yd, w_out, ln_g, ln_b):
    T = x2d.shape[0]
    tm = min(OUTPROJ_ROWS, T)
    rowblk = lambda width: pl.BlockSpec((tm, width), lambda i: (i, 0))
    return pl.pallas_call(
        _outproj_kernel,
        out_shape=jax.ShapeDtypeStruct((T, D_MODEL), F32),
        grid=(T // tm,),
        in_specs=[rowblk(D_MODEL), rowblk(W_MIX), rowblk(W_MIX), rowblk(W_MIX), rowblk(W_MIX),
                  pl.BlockSpec((4, W_MIX, D_MODEL), lambda i: (0, 0, 0)),
                  pl.BlockSpec((1, D_MODEL), lambda i: (0, 0)), pl.BlockSpec((1, D_MODEL), lambda i: (0, 0))],
        out_specs=rowblk(D_MODEL),
        compiler_params=pltpu.CompilerParams(
            dimension_semantics=("parallel",), vmem_limit_bytes=VMEM_LIMIT),
        name="outproj",
    )(x2d, ya, yb, yc, yd, w_out, ln_g.reshape(1, -1), ln_b.reshape(1, -1))


def _alibi_slopes():
    n = 2 * N_HEADS
    s = (2.0 ** (-8.0 * (np.arange(n) + 1) / n)).astype(np.float32)
    return jnp.asarray(s[:N_HEADS]), jnp.asarray(s[N_HEADS:])


def _layer(x, w_in_t, layer, w_out, ssm_conv_w, ssm_conv_b, ssm_dt_bias, ssm_A_log, ssm_D, ssm_norm_w,
           dn_conv_w, dn_conv_b, dn_dt_bias, dn_A_log, dn_norm_w, ln_g, ln_b):
    B, S, _ = x.shape
    T = B * S
    slopes_dil, slopes_moba = _alibi_slopes()
    x2d = x.reshape(T, D_MODEL)
    proj2d = _inproj(x2d, w_in_t, layer)
    proj3 = proj2d.reshape(B, S, PROJ_COLS)

    ya = _moba(proj3, slopes_moba)
    yb = _ssd(proj3, ssm_conv_w, ssm_conv_b, ssm_dt_bias, ssm_A_log, ssm_D, ssm_norm_w)
    yd = _gdn(proj3, dn_conv_w, dn_conv_b, dn_dt_bias, dn_A_log, dn_norm_w)

    yc = _dilated(proj3, slopes_dil)

    out = _outproj(x2d, ya.reshape(T, W_MIX), yb.reshape(T, W_MIX), yc.reshape(T, W_MIX), yd.reshape(T, W_MIX),
                   w_out.reshape(4, W_MIX, D_MODEL).astype(BF16), ln_g, ln_b)
    return out.reshape(B, S, D_MODEL)


def kernel(x, w_in, w_out, ssm_conv_w, ssm_conv_b, ssm_dt_bias, ssm_A_log, ssm_D, ssm_norm_w,
           dn_conv_w, dn_conv_b, dn_dt_bias, dn_A_log, dn_norm_w, ln_g, ln_b):
    assert x.shape[1] % 2048 == 0 and x.shape[2] == D_MODEL
    w_in_t = _repack_w_in(jnp.transpose(w_in, (2, 0, 1)))
    for l in range(DEPTH):
        x = _layer(x, w_in_t, l, w_out[l], ssm_conv_w[l], ssm_conv_b[l], ssm_dt_bias[l], ssm_A_log[l],
                   ssm_D[l], ssm_norm_w[l], dn_conv_w[l], dn_conv_b[l], dn_dt_bias[l], dn_A_log[l],
                   dn_norm_w[l], ln_g[l], ln_b[l])
    return x
```
